```python
import math
import jax
import jax.numpy as jnp
from jax import lax
import numpy as np

D_MODEL = 2048
BATCH = 1
SEQ = 16384
DEPTH = 1
DEC_BATCH = 8
DEC_SEQ = 16
PAST_LEN = 4096

CHUNK = 64
MIX_W = D_MODEL
ATT_HEADS = 16
ATT_KV_HEADS = 2
HEAD_DIM = 64
ATT_GROUP = ATT_HEADS // ATT_KV_HEADS
ATT_W = ATT_HEADS * HEAD_DIM
KV_W = ATT_KV_HEADS * HEAD_DIM
ATT_PROJ = ATT_W + 2 * KV_W
WINDOW = 128
WIN_CHUNKS = WINDOW // CHUNK
BAND = WINDOW + CHUNK
NUM_BUCKETS = 32
MAX_DISTANCE = 128
RW_HEAD = 64
RW_W = MIX_W - ATT_W
RW_HEADS = RW_W // RW_HEAD
DECAY_LORA = 96
ICLR_LORA = 96
GATE_LORA = 128
RW_PROJ = 3 * RW_W + DECAY_LORA + ICLR_LORA + GATE_LORA
RW_SPLITS = (RW_W, 2 * RW_W, 3 * RW_W, 3 * RW_W + DECAY_LORA, 3 * RW_W + DECAY_LORA + ICLR_LORA)
IN_W = ATT_PROJ + RW_PROJ
GN_EPS = 64e-5
LN_EPS = 1e-5
N_EXPERTS = 32
TOP_K = 4
D_FF = D_MODEL
SWIGLU_LIMIT = 7.0
SWIGLU_ALPHA = 1.702
MOE_BLOCK = 256
DN_ALPHA = (2 * DEPTH) ** 0.25
DN_BETA = (8 * DEPTH) ** -0.25

kernel_name = 'hybrid_swa_rwkv7_moe_deepnorm_stream_step'


def layer_norm(x, g, b):
    xf = x.astype(jnp.float32)
    mu = jnp.mean(xf, axis=-1, keepdims=True)
    var = jnp.mean(jnp.square(xf - mu), axis=-1, keepdims=True)
    return ((xf - mu) * lax.rsqrt(var + LN_EPS) * g + b).astype(x.dtype)


def t5_bucket(rel):
    half = NUM_BUCKETS // 2
    exact = half // 2
    n = jnp.abs(rel)
    log_part = exact + (jnp.log(jnp.maximum(n, 1).astype(jnp.float32) / exact)
                        / math.log(MAX_DISTANCE / exact) * (half - exact)).astype(jnp.int32)
    log_part = jnp.minimum(log_part, half - 1)
    return jnp.where(rel > 0, half, 0) + jnp.where(n < exact, n, log_part)


def band_bias(rel_bias):
    qi = jnp.arange(CHUNK)[:, None]
    km = jnp.arange(BAND)[None, :]
    bucket = t5_bucket(km - WINDOW - qi)
    return jnp.transpose(rel_bias[bucket], (2, 0, 1)).astype(jnp.float32)


def window_attention(q, k, v, k_hist, v_hist, hist_valid, bias, sinks):
    b, t = q.shape[:2]
    n_chunks = (t + CHUNK - 1) // CHUNK
    pad = n_chunks * CHUNK - t
    qc = jnp.pad(q, ((0, 0), (0, pad), (0, 0), (0, 0))).reshape(
        b, n_chunks, CHUNK, ATT_KV_HEADS, ATT_GROUP, HEAD_DIM)

    def bands(new, hist):
        full = jnp.concatenate([hist, jnp.pad(new, ((0, 0), (0, pad), (0, 0), (0, 0)))], axis=1)
        full = full.reshape(b, n_chunks + WIN_CHUNKS, CHUNK, ATT_KV_HEADS, HEAD_DIM)
        return jnp.concatenate([full[:, j:j + n_chunks] for j in range(WIN_CHUNKS + 1)], axis=2)

    kb = bands(k, k_hist)
    vb = bands(v, v_hist)
    idx = jnp.arange(n_chunks)[:, None] * CHUNK + jnp.arange(BAND)[None, :]
    valid = jnp.where(idx < WINDOW, hist_valid, idx - WINDOW < t)
    s = jnp.einsum('bcqkgd,bcskd->bckgqs', qc, kb,
                   preferred_element_type=jnp.float32) * (HEAD_DIM ** -0.5)
    s = s + bias.reshape(ATT_KV_HEADS, ATT_GROUP, CHUNK, BAND)[None, None]
    s = jnp.where(valid[None, :, None, None, None, :], s, -1e30)
    sink = sinks.astype(jnp.float32).reshape(1, 1, ATT_KV_HEADS, ATT_GROUP, 1, 1)
    m = jnp.maximum(jnp.max(s, axis=-1, keepdims=True), sink)
    p = jnp.exp(s - m)
    p = p / (jnp.sum(p, axis=-1, keepdims=True) + jnp.exp(sink - m))
    o = jnp.einsum('bckgqs,bcskd->bcqkgd', p.astype(vb.dtype), vb)
    return o.reshape(b, n_chunks * CHUNK, ATT_W)[:, :t]


def rwkv7_time_mix(p, p_prev, s0, lp):
    b, t = p.shape[:2]
    f32 = jnp.float32
    shifted = jnp.concatenate([p_prev, p[:, :-1]], axis=1)
    xm = (p + (shifted - p) * lp['rw_mu']).astype(f32)
    r, k, v, dw, da, dg = jnp.split(xm, RW_SPLITS, axis=-1)
    w_log = lp['rw_w0'] + jnp.tanh(dw) @ lp['rw_decay_up']
    decay = jnp.exp(-jnp.exp(-jax.nn.softplus(-w_log) - 0.5))
    a = jax.nn.sigmoid(lp['rw_a0'] + da @ lp['rw_iclr_up'])
    g = jax.nn.sigmoid(dg) @ lp['rw_gate_up']

    def heads(z):
        return z.reshape(b, t, RW_HEADS, RW_HEAD)

    kk = heads(k * lp['rw_k_k'])
    kk = kk / jnp.maximum(jnp.sqrt(jnp.sum(kk * kk, axis=-1, keepdims=True)), 1e-12)
    a = heads(a)
    k_a = lp['rw_k_a'].reshape(RW_HEADS, RW_HEAD)
    k = heads(k) * (1.0 + (a - 1.0) * k_a)
    r, v, decay = heads(r), heads(v), heads(decay)

    def step(S, inp):
        r_t, w_t, k_t, v_t, kk_t, a_t = inp
        sa = jnp.einsum('bhvk,bhk->bhv', S, kk_t)
        S = (S * w_t[:, :, None, :] - sa[..., None] * (kk_t * a_t)[:, :, None, :]
             + v_t[..., None] * k_t[:, :, None, :])
        return S, jnp.einsum('bhvk,bhk->bhv', S, r_t)

    xs = tuple(jnp.moveaxis(z, 1, 0) for z in (r, decay, k, v, kk, a))
    s_final, o = lax.scan(step, s0.astype(f32), xs)
    o = jnp.moveaxis(o, 0, 1)
    mu = jnp.mean(o, axis=-1, keepdims=True)
    var = jnp.mean(jnp.square(o - mu), axis=-1, keepdims=True)
    o = ((o - mu) * lax.rsqrt(var + GN_EPS)).reshape(b, t, RW_W) * lp['rw_lnx_g'] + lp['rw_lnx_b']
    bonus = jnp.sum(r * k * lp['rw_r_k'], axis=-1, keepdims=True) * v
    o = (o + bonus.reshape(b, t, RW_W)) * g
    return o, s_final, p[:, -1:]


def _moe_block_rows(n_assign):
    blk = 8
    while blk < MOE_BLOCK and blk * N_EXPERTS < n_assign:
        blk *= 2
    return blk


def moe_ffn(x, lp):
    b, t, d = x.shape
    n = b * t
    x2 = x.reshape(n, d)
    w_up, b_up, w_down, b_down = lp['w_up'], lp['b_up'], lp['w_down'], lp['b_down']
    logits = (x2 @ lp['w_router']).astype(jnp.float32) + lp['b_router'].astype(jnp.float32)
    top_logit, top_idx = lax.top_k(logits, TOP_K)
    gate = jax.nn.softmax(top_logit, axis=-1)
    n_assign = n * TOP_K
    blk = _moe_block_rows(n_assign)
    n_blocks = (n_assign + N_EXPERTS * (blk - 1) + blk - 1) // blk
    rows = n_blocks * blk
    flat_e = top_idx.reshape(-1)
    order = jnp.argsort(flat_e)
    e_sorted = flat_e[order]
    counts = jnp.bincount(flat_e, length=N_EXPERTS)
    padded = (counts + blk - 1) // blk * blk
    pad_end = jnp.cumsum(padded)
    rank = jnp.arange(n_assign) - (jnp.cumsum(counts) - counts)[e_sorted]
    dest = (pad_end - padded)[e_sorted] + rank
    row_tok = jnp.zeros((rows,), jnp.int32).at[dest].set((order // TOP_K).astype(jnp.int32))
    row_gate = jnp.zeros((rows,), jnp.float32).at[dest].set(gate.reshape(-1)[order])
    block_e = jnp.minimum(jnp.searchsorted(pad_end, jnp.arange(n_blocks) * blk, side='right'),
                          N_EXPERTS - 1)

    def expert_block(args):
        tok, gw, e = args
        h = x2[tok] @ w_up[e] + b_up[e]
        glu = jnp.minimum(h[:, :D_FF], SWIGLU_LIMIT)
        lin = jnp.clip(h[:, D_FF:], -SWIGLU_LIMIT, SWIGLU_LIMIT)
        act = glu * jax.nn.sigmoid(SWIGLU_ALPHA * glu) * (lin + 1.0)
        return (act @ w_down[e] + b_down[e]) * gw[:, None].astype(x2.dtype)

    out = lax.map(expert_block, (row_tok.reshape(n_blocks, blk), row_gate.reshape(n_blocks, blk), block_e))
    y = jax.ops.segment_sum(out.reshape(rows, d), row_tok, num_segments=n)
    return y.reshape(b, t, d).astype(x.dtype)


def trunk_layer(x, k_hist, v_hist, hist_valid, wkv0, shift0, bias, lp):
    b, t, _ = x.shape
    proj = x @ lp['w_in']
    q = proj[..., :ATT_W].reshape(b, t, ATT_HEADS, HEAD_DIM)
    k = proj[..., ATT_W:ATT_W + KV_W].reshape(b, t, ATT_KV_HEADS, HEAD_DIM)
    v = proj[..., ATT_W + KV_W:ATT_PROJ].reshape(b, t, ATT_KV_HEADS, HEAD_DIM)
    p_rw = proj[..., ATT_PROJ:]
    att = window_attention(q, k, v, k_hist, v_hist, hist_valid, bias, lp['attn_sinks'])
    rw, wkv, shift = rwkv7_time_mix(p_rw, shift0, wkv0, lp)
    mix = jnp.concatenate([att.astype(x.dtype), rw.astype(x.dtype)], axis=-1) @ lp['w_out']
    h = layer_norm(DN_ALPHA * x + mix, lp['ln1_g'], lp['ln1_b'])
    y = layer_norm(DN_ALPHA * h + moe_ffn(h, lp), lp['ln2_g'], lp['ln2_b'])
    new_k = jnp.concatenate([k_hist.astype(k.dtype), k], axis=1)[:, -WINDOW:]
    new_v = jnp.concatenate([v_hist.astype(v.dtype), v], axis=1)[:, -WINDOW:]
    return y, new_k, new_v, wkv, shift


def setup_inputs(seed: int = 0) -> dict:
    key = jax.random.key(seed)
    ks = jax.random.split(key, 40)
    f32 = jnp.float32
    L = DEPTH

    def nrm(k, shape, scale):
        return jax.random.normal(k, shape, f32) * scale

    return {
        'x_prompt': nrm(ks[0], (BATCH, SEQ, D_MODEL), 1.0),
        'x_sample': nrm(ks[1], (DEC_BATCH, DEC_SEQ, D_MODEL), 1.0),
        'cache_k': nrm(ks[2], (L, DEC_BATCH, WINDOW, ATT_KV_HEADS, HEAD_DIM), 1.0),
        'cache_v': nrm(ks[3], (L, DEC_BATCH, WINDOW, ATT_KV_HEADS, HEAD_DIM), 1.0),
        'state_wkv': nrm(ks[4], (L, DEC_BATCH, RW_HEADS, RW_HEAD, RW_HEAD), 0.3),
        'state_shift': nrm(ks[5], (L, DEC_BATCH, 1, RW_PROJ), 1.0),
        'rel_bias': nrm(ks[6], (NUM_BUCKETS, ATT_HEADS), 0.5),
        'w_in': nrm(ks[7], (L, D_MODEL, IN_W), D_MODEL ** -0.5),
        'attn_sinks': nrm(ks[8], (L, ATT_HEADS), 0.5),
        'rw_mu': jax.random.uniform(ks[9], (L, RW_PROJ), f32),
        'rw_w0': jax.random.uniform(ks[10], (L, RW_W), f32, -4.0, 1.0),
        'rw_decay_up': nrm(ks[11], (L, DECAY_LORA, RW_W), 0.1),
        'rw_a0': nrm(ks[12], (L, RW_W), 0.5),
        'rw_iclr_up': nrm(ks[13], (L, ICLR_LORA, RW_W), 0.1),
        'rw_gate_up': nrm(ks[14], (L, GATE_LORA, RW_W), GATE_LORA ** -0.5),
        'rw_k_k': 0.85 + nrm(ks[15], (L, RW_W), 0.05),
        'rw_k_a': 1.0 + nrm(ks[16], (L, RW_W), 0.05),
        'rw_r_k': nrm(ks[17], (L, RW_HEADS, RW_HEAD), 0.1),
        'rw_lnx_g': 1.0 + nrm(ks[18], (L, RW_W), 0.05),
        'rw_lnx_b': nrm(ks[19], (L, RW_W), 0.05),
        'w_out': nrm(ks[20], (L, MIX_W, D_MODEL), MIX_W ** -0.5 * DN_BETA),
        'ln1_g': 1.0 + nrm(ks[21], (L, D_MODEL), 0.05),
        'ln1_b': nrm(ks[22], (L, D_MODEL), 0.05),
        'w_router': nrm(ks[23], (L, D_MODEL, N_EXPERTS), D_MODEL ** -0.5),
        'b_router': nrm(ks[24], (L, N_EXPERTS), 0.01),
        'w_up': nrm(ks[25], (L, N_EXPERTS, D_MODEL, 2 * D_FF), D_MODEL ** -0.5),
        'b_up': nrm(ks[26], (L, N_EXPERTS, 2 * D_FF), 0.01),
        'w_down': nrm(ks[27], (L, N_EXPERTS, D_FF, D_MODEL), D_FF ** -0.5 * DN_BETA),
        'b_down': nrm(ks[28], (L, N_EXPERTS, D_MODEL), 0.01),
        'ln2_g': 1.0 + nrm(ks[29], (L, D_MODEL), 0.05),
        'ln2_b': nrm(ks[30], (L, D_MODEL), 0.05),
    }


def reference(x_prompt, x_sample, cache_k, cache_v, state_wkv, state_shift, rel_bias,
              w_in, attn_sinks, rw_mu, rw_w0, rw_decay_up, rw_a0, rw_iclr_up, rw_gate_up,
              rw_k_k, rw_k_a, rw_r_k, rw_lnx_g, rw_lnx_b, w_out, ln1_g, ln1_b,
              w_router, b_router, w_up, b_up, w_down, b_down, ln2_g, ln2_b):
    bias = band_bias(rel_bias)
    bp = x_prompt.shape[0]
    yp, ys = x_prompt, x_sample
    kp, vp, wp, sp = [], [], [], []
    kq, vq, wq, sq = [], [], [], []
    for l in range(DEPTH):
        lp = {
            'w_in': w_in[l], 'attn_sinks': attn_sinks[l], 'rw_mu': rw_mu[l], 'rw_w0': rw_w0[l],
            'rw_decay_up': rw_decay_up[l], 'rw_a0': rw_a0[l], 'rw_iclr_up': rw_iclr_up[l],
            'rw_gate_up': rw_gate_up[l], 'rw_k_k': rw_k_k[l], 'rw_k_a': rw_k_a[l], 'rw_r_k': rw_r_k[l],
            'rw_lnx_g': rw_lnx_g[l], 'rw_lnx_b': rw_lnx_b[l], 'w_out': w_out[l],
            'ln1_g': ln1_g[l], 'ln1_b': ln1_b[l], 'w_router': w_router[l], 'b_router': b_router[l],
            'w_up': w_up[l], 'b_up': b_up[l], 'w_down': w_down[l], 'b_down': b_down[l],
            'ln2_g': ln2_g[l], 'ln2_b': ln2_b[l],
        }
        zero_kv = jnp.zeros((bp, WINDOW, ATT_KV_HEADS, HEAD_DIM), x_prompt.dtype)
        zero_wkv = jnp.zeros((bp, RW_HEADS, RW_HEAD, RW_HEAD), jnp.float32)
        zero_shift = jnp.zeros((bp, 1, RW_PROJ), x_prompt.dtype)
        yp, k1, v1, w1, s1 = trunk_layer(yp, zero_kv, zero_kv, False, zero_wkv, zero_shift, bias, lp)
        ys, k2, v2, w2, s2 = trunk_layer(ys, cache_k[l], cache_v[l], True, state_wkv[l],
                                         state_shift[l], bias, lp)
        kp.append(k1); vp.append(v1); wp.append(w1); sp.append(s1)
        kq.append(k2); vq.append(v2); wq.append(w2); sq.append(s2)
    return (yp, ys, jnp.stack(kp), jnp.stack(vp), jnp.stack(wp), jnp.stack(sp),
            jnp.stack(kq), jnp.stack(vq), jnp.stack(wq), jnp.stack(sq))
```

```python
import functools
import math

import jax
import jax.numpy as jnp
from jax import lax
from jax.experimental import pallas as pl
from jax.experimental.pallas import tpu as pltpu

f32 = jnp.float32
bf16 = jnp.bfloat16

D_MODEL = 2048
CHUNK = 64
ATT_HEADS = 16
ATT_KV_HEADS = 2
HEAD_DIM = 64
ATT_GROUP = ATT_HEADS // ATT_KV_HEADS
ATT_W = ATT_HEADS * HEAD_DIM
KV_W = ATT_KV_HEADS * HEAD_DIM
ATT_PROJ = ATT_W + 2 * KV_W
WINDOW = 128
BAND = WINDOW + CHUNK
NUM_BUCKETS = 32
MAX_DISTANCE = 128
RW_HEAD = 64
RW_W = 1024
RW_HEADS = RW_W // RW_HEAD
DECAY_LORA = 96
ICLR_LORA = 96
GATE_LORA = 128
RW_PROJ = 3 * RW_W + DECAY_LORA + ICLR_LORA + GATE_LORA
GN_EPS = 64e-5
LN_EPS = 1e-5
N_EXPERTS = 32
TOP_K = 4
D_FF = D_MODEL
SWIGLU_LIMIT = 7.0
SWIGLU_ALPHA = 1.702
DEPTH = 1
DN_ALPHA = (2 * DEPTH) ** 0.25

LANES = 128
VMEM_LIMIT = 56 * 1024 * 1024

LORA_W = DECAY_LORA + ICLR_LORA + GATE_LORA
LORA_PAD = -(-LORA_W // LANES) * LANES
RW_PAD = 3 * RW_W + LORA_PAD
IN_PAD = ATT_PROJ + RW_PAD
N_PAIRS = RW_HEADS // 2
NEG_BIG = -1e30


def _cparams(sem):
    return pltpu.CompilerParams(dimension_semantics=sem, vmem_limit_bytes=VMEM_LIMIT)


def _inproj_kernel(x_ref, w_ref, q_ref, kv_ref, rw_ref):
    acc = jnp.dot(x_ref[...].astype(bf16), w_ref[...], preferred_element_type=f32)
    scale = HEAD_DIM ** -0.5
    for h in range(ATT_HEADS):
        q_ref[h] = (acc[:, h * HEAD_DIM:(h + 1) * HEAD_DIM] * scale).astype(bf16)
    kv_ref[...] = acc[:, ATT_W:ATT_PROJ]
    rw_ref[...] = acc[:, ATT_PROJ:IN_PAD]


def _inproj(x2, w_pad, tm):
    rows = x2.shape[0]
    assert rows % tm == 0
    return pl.pallas_call(
        _inproj_kernel,
        grid=(rows // tm,),
        in_specs=[
            pl.BlockSpec((tm, D_MODEL), lambda i: (i, 0)),
            pl.BlockSpec((D_MODEL, IN_PAD), lambda i: (0, 0), pipeline_mode=pl.Buffered(1)),
        ],
        out_specs=[
            pl.BlockSpec((ATT_HEADS, tm, HEAD_DIM), lambda i: (0, i, 0)),
            pl.BlockSpec((tm, 2 * KV_W), lambda i: (i, 0)),
            pl.BlockSpec((tm, RW_PAD), lambda i: (i, 0)),
        ],
        out_shape=[
            jax.ShapeDtypeStruct((ATT_HEADS, rows, HEAD_DIM), bf16),
            jax.ShapeDtypeStruct((rows, 2 * KV_W), f32),
            jax.ShapeDtypeStruct((rows, RW_PAD), f32),
        ],
        compiler_params=_cparams(("parallel",)),
    )(x2, w_pad)


def _attn_kernel(q_ref, kvm_ref, kva_ref, kvb_ref, bias_ref, sink_ref, o_ref, kvbuf,
                 *, nc, t_valid, hist_valid):
    j = pl.program_id(1)
    kvbuf[0:nc * CHUNK] = kvm_ref[...].astype(bf16)
    kvbuf[nc * CHUNK:(nc + 1) * CHUNK] = kva_ref[...].astype(bf16)
    kvbuf[(nc + 1) * CHUNK:(nc + 2) * CHUNK] = kvb_ref[...].astype(bf16)
    m_idx = lax.broadcasted_iota(jnp.int32, (1, 1, BAND), 2)

    def chunk(c, carry):
        r0 = pl.multiple_of(c * CHUNK, CHUNK)
        band = kvbuf[pl.ds(r0, BAND), :]
        idx = (j * nc + c) * CHUNK + m_idx
        valid = idx - WINDOW < t_valid
        if not hist_valid:
            valid = jnp.logical_and(valid, idx >= WINDOW)
        outs = []
        for g in range(ATT_KV_HEADS):
            kb = band[:, g * HEAD_DIM:(g + 1) * HEAD_DIM]
            vb = band[:, KV_W + g * HEAD_DIM:KV_W + (g + 1) * HEAD_DIM]
            qg = q_ref[g * ATT_GROUP:(g + 1) * ATT_GROUP, pl.ds(r0, CHUNK), :]
            qg = qg.reshape(ATT_GROUP * CHUNK, HEAD_DIM)
            s = lax.dot_general(qg, kb, (((1,), (1,)), ((), ())), preferred_element_type=f32)
            s = s.reshape(ATT_GROUP, CHUNK, BAND) + bias_ref[g * ATT_GROUP:(g + 1) * ATT_GROUP]
            s = jnp.where(valid, s, NEG_BIG)
            sk = sink_ref[g * ATT_GROUP:(g + 1) * ATT_GROUP]
            m = jnp.maximum(jnp.max(s, axis=-1, keepdims=True), sk)
            p = jnp.exp(s - m)
            den = jnp.sum(p, axis=-1, keepdims=True) + jnp.exp(sk - m)
            o = jnp.dot(p.reshape(ATT_GROUP * CHUNK, BAND).astype(bf16), vb,
                        preferred_element_type=f32)
            o = o.reshape(ATT_GROUP, CHUNK, HEAD_DIM) / den
            for h in range(ATT_GROUP):
                outs.append(o[h])
        o_ref[pl.ds(r0, CHUNK), :] = jnp.concatenate(outs, axis=-1).astype(bf16)
        return carry

    lax.fori_loop(0, nc, chunk, 0)


def _attention(q4, kvfull, bias, sinks3, *, nc, t_valid, hist_valid):
    _, b, tp, _ = q4.shape
    assert tp % (nc * CHUNK) == 0 and kvfull.shape[1] == tp + WINDOW
    nblk = tp // (nc * CHUNK)
    kern = functools.partial(_attn_kernel, nc=nc, t_valid=t_valid, hist_valid=hist_valid)
    return pl.pallas_call(
        kern,
        grid=(b, nblk),
        in_specs=[
            pl.BlockSpec((ATT_HEADS, None, nc * CHUNK, HEAD_DIM), lambda bi, j: (0, bi, j, 0)),
            pl.BlockSpec((None, nc * CHUNK, 2 * KV_W), lambda bi, j: (bi, j, 0)),
            pl.BlockSpec((None, CHUNK, 2 * KV_W), lambda bi, j: (bi, (j + 1) * nc, 0)),
            pl.BlockSpec((None, CHUNK, 2 * KV_W), lambda bi, j: (bi, (j + 1) * nc + 1, 0)),
            pl.BlockSpec((ATT_HEADS, CHUNK, BAND), lambda bi, j: (0, 0, 0)),
            pl.BlockSpec((ATT_HEADS, 1, 1), lambda bi, j: (0, 0, 0)),
        ],
        out_specs=pl.BlockSpec((None, nc * CHUNK, ATT_W), lambda bi, j: (bi, j, 0)),
        out_shape=jax.ShapeDtypeStruct((b, tp, ATT_W), bf16),
        scratch_shapes=[pltpu.VMEM(((nc + 2) * CHUNK, 2 * KV_W), bf16)],
        compiler_params=_cparams(("parallel", "parallel")),
    )(q4, kvfull, kvfull, kvfull, bias, sinks3)


def _rwkv_kernel(p_ref, shift0_ref, s0_ref, mu_ref, w0_ref, wd_ref, a0_ref, wa_ref, wg_ref,
                 kk_ref, ka_ref, rk_ref, lng_ref, lnb_ref, o_ref, sfin_ref, s_scr, last_scr,
                 *, t_valid, n_chunks):
    c = pl.program_id(1)
    L = CHUNK

    @pl.when(c == 0)
    def _():
        s_scr[...] = s0_ref[...]
        last_scr[...] = shift0_ref[...]

    p = p_ref[...]
    row = lax.broadcasted_iota(jnp.int32, (L, 1), 0)
    shifted = jnp.where(row == 0, last_scr[...], pltpu.roll(p, 1, axis=0))
    last_scr[...] = p[L - 1:L, :]
    xm = p + (shifted - p) * mu_ref[...]
    r = xm[:, 0:RW_W]
    k = xm[:, RW_W:2 * RW_W]
    v = xm[:, 2 * RW_W:3 * RW_W]
    tail = xm[:, 3 * RW_W:RW_PAD]

    def sigmoid(z):
        return 1.0 / (1.0 + jnp.exp(-z))

    w_log = w0_ref[...] + jnp.dot(jnp.tanh(tail).astype(bf16), wd_ref[...],
                                  preferred_element_type=f32)
    z = -w_log
    softplus = jnp.maximum(z, 0.0) + jnp.log(1.0 + jnp.exp(-jnp.abs(z)))
    ld = -jnp.exp(-softplus - 0.5)
    a = sigmoid(a0_ref[...] + jnp.dot(tail.astype(bf16), wa_ref[...], preferred_element_type=f32))
    g = jnp.dot(sigmoid(tail).astype(bf16), wg_ref[...], preferred_element_type=f32)

    lane = lax.broadcasted_iota(jnp.int32, (1, LANES), 1)
    lo_half = lane < RW_HEAD
    rr = lax.broadcasted_iota(jnp.int32, (LANES, LANES), 0)
    cc = lax.broadcasted_iota(jnp.int32, (LANES, LANES), 1)
    same_head = (rr // RW_HEAD) == (cc // RW_HEAD)
    ones_bd = jnp.where(same_head, 1.0, 0.0).astype(bf16)

    def seg_sums(xs):
        n = len(xs)
        x = jnp.concatenate(xs, axis=0) if n > 1 else xs[0]
        hi = x.astype(bf16)
        lo = (x - hi.astype(f32)).astype(bf16)
        both = jnp.concatenate([hi, lo], axis=0)
        m = 2 * n * L
        tiles = jnp.concatenate([both[:, LANES * t:LANES * (t + 1)] for t in range(N_PAIRS)], axis=0)
        res = jnp.dot(tiles, ones_bd, preferred_element_type=f32)
        y = jnp.concatenate([res[m * t:m * (t + 1)] for t in range(N_PAIRS)], axis=1)
        y = y[:n * L] + y[n * L:]
        return [y[i * L:(i + 1) * L] for i in range(n)]

    kk = k * kk_ref[...]
    k_mod = k * (1.0 + (a - 1.0) * ka_ref[...])
    nrm2, bonus_s = seg_sums([kk * kk, r * k_mod * rk_ref[...]])
    kk = kk / jnp.maximum(jnp.sqrt(nrm2), 1e-12)
    b = kk * a

    if t_valid % L != 0:
        live = (c * L + row) < t_valid
        ld = jnp.where(live, ld, 0.0)
        b = jnp.where(live, b, 0.0)
        k_mod = jnp.where(live, k_mod, 0.0)

    h1 = ld.astype(bf16)
    r1 = ld - h1.astype(f32)
    h2 = r1.astype(bf16)
    h3 = (r1 - h2.astype(f32)).astype(bf16)
    ti = lax.broadcasted_iota(jnp.int32, (L, 3 * L), 0)
    si = lax.broadcasted_iota(jnp.int32, (L, 3 * L), 1) % L
    tri3 = jnp.where(si <= ti, 1.0, 0.0).astype(bf16)
    cum = jnp.dot(tri3, jnp.concatenate([h1, h2, h3], axis=0), preferred_element_type=f32)
    cum_l = cum[L - 1:L, :]
    g_l = jnp.exp(cum_l)
    g_inv = jnp.exp(-cum)
    g_rest = jnp.exp(cum_l - cum)
    kq = (kk * jnp.exp(cum - ld)).astype(bf16)
    rq_f = r * jnp.exp(cum)
    rq = rq_f.astype(bf16)
    bt = (b * g_inv).astype(bf16)
    kt = (k_mod * g_inv).astype(bf16)
    bh = (b * g_rest).astype(bf16)
    kh = (k_mod * g_rest).astype(bf16)
    vb = v.astype(bf16)

    def bd(x):
        zero = jnp.zeros_like(x)
        return jnp.concatenate([jnp.where(lo_half, x, zero), jnp.where(lo_half, zero, x)], axis=0)

    def mm(x, y):
        return jnp.dot(x, y, preferred_element_type=f32)

    def mm_nt(x, y):
        return lax.dot_general(x, y, (((1,), (1,)), ((), ())), preferred_element_type=f32)

    def mm_tn(x, y):
        return lax.dot_general(x, y, (((0,), (0,)), ((), ())), preferred_element_type=f32)

    tt = lax.broadcasted_iota(jnp.int32, (L, LANES), 0)
    ss = lax.broadcasted_iota(jnp.int32, (L, LANES), 1) % RW_HEAD
    strict = ss < tt
    incl = ss <= tt
    eye_pair = jnp.where(ss == tt, 1.0, 0.0).astype(f32)

    o_tiles = []
    for t in range(N_PAIRS):
        sl = slice(LANES * t, LANES * (t + 1))
        kq_p, rq_p, bt_p, kt_p, bh_p, kh_p, v_p = (x[:, sl] for x in (kq, rq, bt, kt, bh, kh, vb))
        a_all = mm_nt(jnp.concatenate([kq_p, rq_p], axis=0),
                      jnp.concatenate([bd(bt_p), bd(kt_p)], axis=0))
        a_bk = jnp.where(strict, a_all[:L, :LANES], 0.0)
        a_kk = jnp.where(strict, a_all[:L, LANES:], 0.0)
        a_rb = jnp.where(incl, a_all[L:, :LANES], 0.0)
        a_rk = jnp.where(incl, a_all[L:, LANES:], 0.0)
        w_inv = eye_pair - a_bk
        pw = a_bk.astype(bf16)
        pw_bd = bd(pw)
        for _ in range(5):
            pw = mm(pw, pw_bd).astype(bf16)
            pw_bd = bd(pw)
            w_inv = w_inv + mm(w_inv.astype(bf16), pw_bd)
        bd_v = bd(v_p)
        akv = mm(a_kk.astype(bf16), bd_v).astype(bf16)
        qu = mm(w_inv.astype(bf16), jnp.concatenate([bd(kq_p), bd(akv)], axis=1))
        q_m = qu[:, :LANES].astype(bf16)
        u_m = qu[:, LANES:].astype(bf16)
        m_full = mm_tn(q_m, bh_p)
        neg_m = jnp.where(same_head, -m_full, 0.0).astype(bf16)
        c_full = mm_tn(jnp.concatenate([v_p, u_m], axis=0),
                       jnp.concatenate([kh_p, -bh_p], axis=0))
        c_pair = jnp.where(lo_half, c_full[:RW_HEAD], c_full[RW_HEAD:])
        go = mm(a_rb.astype(bf16), jnp.concatenate([bd(q_m), bd(u_m)], axis=1))
        g_m = (rq_f[:, sl] - go[:, :LANES]).astype(bf16)
        o_intra = mm(a_rk.astype(bf16), bd_v) - go[:, LANES:]
        s_old = s_scr[t]
        s_b = s_old.astype(bf16)
        o_tiles.append(mm_nt(g_m, bd(s_b)) + o_intra)
        s_scr[t] = s_old * g_l[:, sl] + mm(s_b, neg_m) + c_pair

    o = jnp.concatenate(o_tiles, axis=1)
    (o_sum,) = seg_sums([o])
    d = o - o_sum * (1.0 / RW_HEAD)
    (d2,) = seg_sums([d * d])
    on = d * lax.rsqrt(d2 * (1.0 / RW_HEAD) + GN_EPS) * lng_ref[...] + lnb_ref[...]
    o_ref[...] = ((on + bonus_s * v) * g).astype(bf16)

    @pl.when(c == n_chunks - 1)
    def _():
        sfin_ref[...] = s_scr[...]


def _rwkv(prw, shift0, s0_pair, wts, *, t_valid):
    b, tp, _ = prw.shape
    n_chunks = tp // CHUNK
    kern = functools.partial(_rwkv_kernel, t_valid=t_valid, n_chunks=n_chunks)
    const2 = lambda bi, c: (0, 0)
    row_spec = pl.BlockSpec((1, RW_W), const2)
    return pl.pallas_call(
        kern,
        grid=(b, n_chunks),
        in_specs=[
            pl.BlockSpec((None, CHUNK, RW_PAD), lambda bi, c: (bi, c, 0)),
            pl.BlockSpec((None, 1, RW_PAD), lambda bi, c: (bi, 0, 0)),
            pl.BlockSpec((None, N_PAIRS, RW_HEAD, LANES), lambda bi, c: (bi, 0, 0, 0)),
            pl.BlockSpec((1, RW_PAD), const2),
            row_spec,
            pl.BlockSpec((LORA_PAD, RW_W), const2),
            row_spec,
            pl.BlockSpec((LORA_PAD, RW_W), const2),
            pl.BlockSpec((LORA_PAD, RW_W), const2),
            row_spec, row_spec, row_spec, row_spec, row_spec,
        ],
        out_specs=[
            pl.BlockSpec((None, CHUNK, RW_W), lambda bi, c: (bi, c, 0)),
            pl.BlockSpec((None, N_PAIRS, RW_HEAD, LANES), lambda bi, c: (bi, 0, 0, 0)),
        ],
        out_shape=[
            jax.ShapeDtypeStruct((b, tp, RW_W), bf16),
            jax.ShapeDtypeStruct((b, N_PAIRS, RW_HEAD, LANES), f32),
        ],
        scratch_shapes=[pltpu.VMEM((N_PAIRS, RW_HEAD, LANES), f32), pltpu.VMEM((1, RW_PAD), f32)],
        compiler_params=_cparams(("parallel", "arbitrary")),
    )(prw, shift0, s0_pair, wts["mu"], wts["w0"], wts["wd"], wts["a0"], wts["wa"], wts["wg"],
      wts["k_k"], wts["k_a"], wts["r_k"], wts["lnx_g"], wts["lnx_b"])


def _layer_norm(z, g, b):
    mu = jnp.mean(z, axis=-1, keepdims=True)
    d = z - mu
    var = jnp.mean(d * d, axis=-1, keepdims=True)
    return d * lax.rsqrt(var + LN_EPS) * g + b


def _outproj_kernel(*refs, aliased):
    att_ref, rw_ref, x_ref, wo_ref, g_ref, b_ref, wrh_ref, wrl_ref, br_ref = refs[:9]
    h_ref, idx_ref, gate_ref = refs[9 + 3 * aliased:]
    mix = (jnp.dot(att_ref[...], wo_ref[0:ATT_W], preferred_element_type=f32)
           + jnp.dot(rw_ref[...], wo_ref[ATT_W:ATT_W + RW_W], preferred_element_type=f32))
    h = _layer_norm(DN_ALPHA * x_ref[...] + mix, g_ref[...], b_ref[...])
    h_ref[...] = h
    hh = h.astype(bf16)
    hl = (h - hh.astype(f32)).astype(bf16)
    logits = (jnp.dot(hh, wrh_ref[...], preferred_element_type=f32)
              + jnp.dot(hl, wrh_ref[...], preferred_element_type=f32)
              + jnp.dot(hh, wrl_ref[...], preferred_element_type=f32)) + br_ref[...]
    lane = lax.broadcasted_iota(jnp.int32, logits.shape, 1).astype(f32)
    vals, idxs = [], []
    cur = logits
    for _ in range(TOP_K):
        m = jnp.max(cur, axis=-1, keepdims=True)
        i = jnp.min(jnp.where(cur == m, lane, float(LANES)), axis=-1, keepdims=True)
        vals.append(m)
        idxs.append(i)
        cur = jnp.where(lane == i, -jnp.inf, cur)
    es = [jnp.exp(vv - vals[0]) for vv in vals]
    tot = es[0] + es[1] + es[2] + es[3]
    idx_ref[...] = jnp.concatenate(idxs, axis=-1).astype(jnp.int32)
    gate_ref[...] = jnp.concatenate([e / tot for e in es], axis=-1)


def _outproj(att2, rw2, x2, wts, *, tm, row0, total_rows, prev=None):
    rows = x2.shape[0]
    assert rows % tm == 0 and row0 % tm == 0
    blk0 = row0 // tm
    aliased = prev is not None
    const = lambda i: (0, 0)
    in_specs = [
        pl.BlockSpec((tm, ATT_W), lambda i: (i, 0)),
        pl.BlockSpec((tm, RW_W), lambda i: (i, 0)),
        pl.BlockSpec((tm, D_MODEL), lambda i: (i, 0)),
        pl.BlockSpec((D_MODEL, D_MODEL), const, pipeline_mode=pl.Buffered(1)),
        pl.BlockSpec((1, D_MODEL), const),
        pl.BlockSpec((1, D_MODEL), const),
        pl.BlockSpec((D_MODEL, LANES), const),
        pl.BlockSpec((D_MODEL, LANES), const),
        pl.BlockSpec((1, LANES), const),
    ]
    args = [att2, rw2, x2, wts["w_out"], wts["ln1_g"], wts["ln1_b"], wts["wr_hi"], wts["wr_lo"],
            wts["b_router"]]
    aliases = {}
    if aliased:
        in_specs += [pl.BlockSpec(memory_space=pl.ANY)] * 3
        args += list(prev)
        aliases = {9: 0, 10: 1, 11: 2}
    return pl.pallas_call(
        functools.partial(_outproj_kernel, aliased=int(aliased)),
        grid=(rows // tm,),
        in_specs=in_specs,
        out_specs=[
            pl.BlockSpec((tm, D_MODEL), lambda i: (blk0 + i, 0)),
            pl.BlockSpec((tm, TOP_K), lambda i: (blk0 + i, 0)),
            pl.BlockSpec((tm, TOP_K), lambda i: (blk0 + i, 0)),
        ],
        out_shape=[
            jax.ShapeDtypeStruct((total_rows, D_MODEL), f32),
            jax.ShapeDtypeStruct((total_rows, TOP_K), jnp.int32),
            jax.ShapeDtypeStruct((total_rows, TOP_K), f32),
        ],
        input_output_aliases=aliases,
        compiler_params=_cparams(("parallel",)),
    )(*args)


def _routing(top_idx, rb):
    n = top_idx.shape[0]
    n_assign = n * TOP_K
    flat_e = top_idx.reshape(-1)
    onehot = (flat_e[:, None] == jnp.arange(N_EXPERTS, dtype=jnp.int32)[None, :]).astype(jnp.int32)
    csum = jnp.cumsum(onehot, axis=0)
    counts = csum[-1]
    rank = jnp.take_along_axis(csum, flat_e[:, None], axis=1)[:, 0] - 1
    padded = (counts + rb - 1) // rb * rb
    pad_end = jnp.cumsum(padded)
    start = pad_end - padded
    dest = (start[flat_e] + rank).astype(jnp.int32)
    n_blocks = (n_assign + N_EXPERTS * (rb - 1) + rb - 1) // rb
    rows = n_blocks * rb
    row_tok = jnp.zeros((rows,), jnp.int32).at[dest].set(
        jnp.arange(n_assign, dtype=jnp.int32) // TOP_K)
    block_e = jnp.minimum(
        jnp.searchsorted(pad_end, jnp.arange(n_blocks, dtype=jnp.int32) * rb, side="right"),
        N_EXPERTS - 1).astype(jnp.int32)
    n_used = (pad_end[-1] // rb).astype(jnp.int32).reshape(1)
    return dest.reshape(n, TOP_K), row_tok, block_e, n_used, n_blocks


def _row_copy(src_hbm, src_row, dst_ref, dst_row, sem):
    return pltpu.make_async_copy(src_hbm.at[pl.ds(src_row, 1)], dst_ref.at[pl.ds(dst_row, 1)], sem)


def _gather_kernel(nused_ref, tok_ref, h_hbm, o_ref, sem, *, rb):
    i = pl.program_id(0)

    @pl.when(i < nused_ref[0])
    def _():
        def issue(r, carry):
            _row_copy(h_hbm, tok_ref[0, 0, r], o_ref, r, sem).start()
            return carry

        lax.fori_loop(0, rb, issue, 0)

        def drain(r, carry):
            _row_copy(h_hbm, 0, o_ref, r, sem).wait()
            return carry

        lax.fori_loop(0, rb, drain, 0)


def _gather_rows(h_all, row_tok, n_used, n_blocks, rb):
    kern = functools.partial(_gather_kernel, rb=rb)
    last = lambda i, nu: jnp.minimum(i, nu[0] - 1)
    return pl.pallas_call(
        kern,
        grid_spec=pltpu.PrefetchScalarGridSpec(
            num_scalar_prefetch=1,
            grid=(n_blocks,),
            in_specs=[
                pl.BlockSpec((1, 1, rb), lambda i, nu: (last(i, nu), 0, 0),
                             memory_space=pltpu.SMEM),
                pl.BlockSpec(memory_space=pl.ANY),
            ],
            out_specs=pl.BlockSpec((rb, D_MODEL), lambda i, nu: (last(i, nu), 0)),
            scratch_shapes=[pltpu.SemaphoreType.DMA(())],
        ),
        out_shape=jax.ShapeDtypeStruct((n_blocks * rb, D_MODEL), f32),
        compiler_params=_cparams(("arbitrary",)),
    )(n_used, row_tok.reshape(n_blocks, 1, rb), h_all)


def _expert_kernel(be_ref, nused_ref, x_ref, wg_ref, wl_ref, bg_ref, bl_ref, wd_ref, bd_ref,
                   o_ref, xb_scr):
    i = pl.program_id(0)
    j = pl.program_id(1)

    @pl.when(i < nused_ref[0])
    def _():
        @pl.when(j == 0)
        def _():
            xb_scr[...] = x_ref[...].astype(bf16)

        xb = xb_scr[...]
        hg = jnp.dot(xb, wg_ref[...].astype(bf16), preferred_element_type=f32) + bg_ref[...]
        hl = jnp.dot(xb, wl_ref[...].astype(bf16), preferred_element_type=f32) + bl_ref[...]
        glu = jnp.minimum(hg, SWIGLU_LIMIT)
        lin = jnp.clip(hl, -SWIGLU_LIMIT, SWIGLU_LIMIT)
        act = glu * (1.0 / (1.0 + jnp.exp(-SWIGLU_ALPHA * glu))) * (lin + 1.0)
        part = jnp.dot(act.astype(bf16), wd_ref[...].astype(bf16), preferred_element_type=f32)

        @pl.when(j == 0)
        def _():
            o_ref[...] = part + bd_ref[...]

        @pl.when(j != 0)
        def _():
            o_ref[...] += part


def _experts(xs, block_e, n_used, w_up, b_up, w_down, b_down, n_blocks, rb, tf):
    nf = D_FF // tf
    last = lambda i, nu: jnp.minimum(i, nu[0] - 1)
    b_up3 = b_up.reshape(N_EXPERTS, 1, 2 * D_FF)
    b_down3 = b_down.reshape(N_EXPERTS, 1, D_MODEL)
    return pl.pallas_call(
        _expert_kernel,
        grid_spec=pltpu.PrefetchScalarGridSpec(
            num_scalar_prefetch=2,
            grid=(n_blocks, nf),
            in_specs=[
                pl.BlockSpec((rb, D_MODEL), lambda i, j, be, nu: (last(i, nu), 0)),
                pl.BlockSpec((None, D_MODEL, tf), lambda i, j, be, nu: (be[last(i, nu)], 0, j)),
                pl.BlockSpec((None, D_MODEL, tf), lambda i, j, be, nu: (be[last(i, nu)], 0, nf + j)),
                pl.BlockSpec((None, 1, tf), lambda i, j, be, nu: (be[last(i, nu)], 0, j)),
                pl.BlockSpec((None, 1, tf), lambda i, j, be, nu: (be[last(i, nu)], 0, nf + j)),
                pl.BlockSpec((None, tf, D_MODEL), lambda i, j, be, nu: (be[last(i, nu)], j, 0)),
                pl.BlockSpec((None, 1, D_MODEL), lambda i, j, be, nu: (be[last(i, nu)], 0, 0)),
            ],
            out_specs=pl.BlockSpec((rb, D_MODEL), lambda i, j, be, nu: (last(i, nu), 0)),
            scratch_shapes=[pltpu.VMEM((rb, D_MODEL), bf16)],
        ),
        out_shape=jax.ShapeDtypeStruct((n_blocks * rb, D_MODEL), f32),
        compiler_params=_cparams(("arbitrary", "arbitrary")),
    )(block_e, n_used, xs, w_up, w_up, b_up3, b_up3, w_down, b_down3)


def _combine_kernel(dest_ref, gate_ref, h_ref, rows_hbm, g_ref, b_ref, yp_ref, ys_ref, buf, sem,
                    *, tm, n_first):
    i = pl.program_id(0)

    def issue(t, carry):
        for jx in range(TOP_K):
            _row_copy(rows_hbm, dest_ref[0, 0, jx * tm + t], buf.at[jx], t, sem).start()
        return carry

    lax.fori_loop(0, tm, issue, 0)

    def drain(t, carry):
        for jx in range(TOP_K):
            _row_copy(rows_hbm, 0, buf.at[jx], t, sem).wait()
        return carry

    lax.fori_loop(0, tm, drain, 0)
    gate = gate_ref[...]
    y = gate[:, 0:1] * buf[0]
    for jx in range(1, TOP_K):
        y = y + gate[:, jx:jx + 1] * buf[jx]
    out = _layer_norm(DN_ALPHA * h_ref[...] + y, g_ref[...], b_ref[...])

    @pl.when(i < n_first)
    def _():
        yp_ref[...] = out

    @pl.when(i >= n_first)
    def _():
        ys_ref[...] = out


def _combine(rows_out, dest, gate, h_all, ln_g, ln_b, *, tm, n_first_rows):
    n = h_all.shape[0]
    assert n % tm == 0 and n_first_rows % tm == 0
    nblk = n // tm
    n_first = n_first_rows // tm
    dest_blk = dest.reshape(nblk, tm, TOP_K).transpose(0, 2, 1).reshape(nblk, 1, TOP_K * tm)
    kern = functools.partial(_combine_kernel, tm=tm, n_first=n_first)
    const = lambda i: (0, 0)
    return pl.pallas_call(
        kern,
        grid=(nblk,),
        in_specs=[
            pl.BlockSpec((1, 1, TOP_K * tm), lambda i: (i, 0, 0), memory_space=pltpu.SMEM),
            pl.BlockSpec((tm, TOP_K), lambda i: (i, 0)),
            pl.BlockSpec((tm, D_MODEL), lambda i: (i, 0)),
            pl.BlockSpec(memory_space=pl.ANY),
            pl.BlockSpec((1, D_MODEL), const),
            pl.BlockSpec((1, D_MODEL), const),
        ],
        out_specs=[
            pl.BlockSpec((tm, D_MODEL), lambda i: (jnp.minimum(i, n_first - 1), 0)),
            pl.BlockSpec((tm, D_MODEL), lambda i: (jnp.maximum(i - n_first, 0), 0)),
        ],
        out_shape=[
            jax.ShapeDtypeStruct((n_first_rows, D_MODEL), f32),
            jax.ShapeDtypeStruct((n - n_first_rows, D_MODEL), f32),
        ],
        scratch_shapes=[pltpu.VMEM((TOP_K, tm, D_MODEL), f32), pltpu.SemaphoreType.DMA(())],
        compiler_params=_cparams(("arbitrary",)),
    )(dest_blk, gate, h_all, rows_out, ln_g, ln_b)


def _t5_bucket(rel):
    half = NUM_BUCKETS // 2
    exact = half // 2
    n = jnp.abs(rel)
    log_part = exact + (jnp.log(jnp.maximum(n, 1).astype(jnp.float32) / exact)
                        / math.log(MAX_DISTANCE / exact) * (half - exact)).astype(jnp.int32)
    log_part = jnp.minimum(log_part, half - 1)
    return jnp.where(rel > 0, half, 0) + jnp.where(n < exact, n, log_part)


def _band_bias(rel_bias):
    qi = jnp.arange(CHUNK)[:, None]
    km = jnp.arange(BAND)[None, :]
    bucket = _t5_bucket(km - WINDOW - qi)
    return jnp.transpose(rel_bias[bucket], (2, 0, 1)).astype(jnp.float32)


def _pad_cols(a, width):
    return jnp.pad(a, ((0, 0), (0, width - a.shape[-1])))


def _pair_state(s):
    b = s.shape[0]
    return s.reshape(b, N_PAIRS, 2, RW_HEAD, RW_HEAD).transpose(0, 1, 3, 2, 4).reshape(
        b, N_PAIRS, RW_HEAD, LANES)


def _unpair_state(s):
    b = s.shape[0]
    return s.reshape(b, N_PAIRS, RW_HEAD, 2, RW_HEAD).transpose(0, 1, 3, 2, 4).reshape(
        b, RW_HEADS, RW_HEAD, RW_HEAD)


def _mix_group(x, k_hist, v_hist, hist_valid, wkv0, shift0, bias, sinks3, w_in_pad, rw_wts,
               *, in_tm, attn_nc):
    b, t, _ = x.shape
    q, kv, prw = _inproj(x.reshape(b * t, D_MODEL), w_in_pad, in_tm)
    tp = -(-t // (attn_nc * CHUNK)) * (attn_nc * CHUNK)
    q4 = q.reshape(ATT_HEADS, b, t, HEAD_DIM)
    kv3 = kv.reshape(b, t, 2 * KV_W)
    prw3 = prw.reshape(b, t, RW_PAD)
    if tp != t:
        q4 = jnp.pad(q4, ((0, 0), (0, 0), (0, tp - t), (0, 0)))
        prw3 = jnp.pad(prw3, ((0, 0), (0, tp - t), (0, 0)))
    hist = jnp.concatenate([k_hist.reshape(b, WINDOW, KV_W), v_hist.reshape(b, WINDOW, KV_W)], axis=-1)
    kvfull = jnp.concatenate([hist, kv3, jnp.zeros((b, tp - t, 2 * KV_W), f32)], axis=1)
    att = _attention(q4, kvfull, bias, sinks3, nc=attn_nc, t_valid=t, hist_valid=hist_valid)
    rw, s_fin = _rwkv(prw3, _pad_cols(shift0.reshape(b, RW_PROJ), RW_PAD).reshape(b, 1, RW_PAD),
                      _pair_state(wkv0.astype(f32)), rw_wts, t_valid=t)
    new_kv = kvfull[:, t:t + WINDOW]
    new_k = new_kv[..., :KV_W].reshape(b, WINDOW, ATT_KV_HEADS, HEAD_DIM)
    new_v = new_kv[..., KV_W:].reshape(b, WINDOW, ATT_KV_HEADS, HEAD_DIM)
    shift = prw3[:, t - 1:t, :RW_PROJ]
    return (att[:, :t].reshape(b * t, ATT_W), rw[:, :t].reshape(b * t, RW_W),
            new_k, new_v, _unpair_state(s_fin), shift)


MOE_ROWS = 512
MOE_FF_TILE = 512
COMBINE_TM = 128


def kernel(x_prompt, x_sample, cache_k, cache_v, state_wkv, state_shift, rel_bias, w_in, attn_sinks, rw_mu, rw_w0, rw_decay_up, rw_a0, rw_iclr_up, rw_gate_up, rw_k_k, rw_k_a, rw_r_k, rw_lnx_g, rw_lnx_b, w_out, ln1_g, ln1_b, w_router, b_router, w_up, b_up, w_down, b_down, ln2_g, ln2_b):
    assert w_in.shape[0] == DEPTH == 1
    l = 0
    bp, tp_, _ = x_prompt.shape
    bs, ts, _ = x_sample.shape
    bias = _band_bias(rel_bias)
    sinks3 = attn_sinks[l].astype(f32).reshape(ATT_HEADS, 1, 1)

    w_in_pad = _pad_cols(w_in[l], IN_PAD).astype(bf16)

    def lora_rows(w, row0):
        return jnp.zeros((LORA_PAD, RW_W), f32).at[row0:row0 + w.shape[0]].set(w).astype(bf16)

    rw_wts = {
        "mu": _pad_cols(rw_mu[l].reshape(1, RW_PROJ), RW_PAD),
        "w0": rw_w0[l].reshape(1, RW_W),
        "wd": lora_rows(rw_decay_up[l], 0),
        "a0": rw_a0[l].reshape(1, RW_W),
        "wa": lora_rows(rw_iclr_up[l], DECAY_LORA),
        "wg": lora_rows(rw_gate_up[l], DECAY_LORA + ICLR_LORA),
        "k_k": rw_k_k[l].reshape(1, RW_W),
        "k_a": rw_k_a[l].reshape(1, RW_W),
        "r_k": rw_r_k[l].reshape(1, RW_W),
        "lnx_g": rw_lnx_g[l].reshape(1, RW_W),
        "lnx_b": rw_lnx_b[l].reshape(1, RW_W),
    }
    wr = _pad_cols(w_router[l], LANES)
    wr_hi = wr.astype(bf16)
    op_wts = {
        "w_out": w_out[l].astype(bf16),
        "ln1_g": ln1_g[l].reshape(1, D_MODEL),
        "ln1_b": ln1_b[l].reshape(1, D_MODEL),
        "wr_hi": wr_hi,
        "wr_lo": (wr - wr_hi.astype(f32)).astype(bf16),
        "b_router": jnp.concatenate([b_router[l].astype(f32),
                                     jnp.full((LANES - N_EXPERTS,), NEG_BIG, f32)]).reshape(1, LANES),
    }

    zero_kv = jnp.zeros((bp, WINDOW, ATT_KV_HEADS, HEAD_DIM), f32)
    att_p, rwo_p, k1, v1, w1, s1 = _mix_group(
        x_prompt, zero_kv, zero_kv, False, jnp.zeros((bp, RW_HEADS, RW_HEAD, RW_HEAD), f32),
        jnp.zeros((bp, 1, RW_PROJ), f32), bias, sinks3, w_in_pad, rw_wts,
        in_tm=min(256, bp * tp_), attn_nc=min(8, -(-tp_ // CHUNK)))
    att_s, rwo_s, k2, v2, w2, s2 = _mix_group(
        x_sample, cache_k[l], cache_v[l], True, state_wkv[l], state_shift[l], bias, sinks3,
        w_in_pad, rw_wts, in_tm=min(256, bs * ts), attn_nc=1)

    n_p, n_s = bp * tp_, bs * ts
    n_all = n_p + n_s
    tm_p, tm_s = min(256, n_p), min(128, n_s)
    outs = _outproj(att_p, rwo_p, x_prompt.reshape(n_p, D_MODEL), op_wts,
                    tm=tm_p, row0=0, total_rows=n_all)
    h_all, top_idx, gate = _outproj(att_s, rwo_s, x_sample.reshape(n_s, D_MODEL), op_wts,
                                    tm=tm_s, row0=n_p, total_rows=n_all, prev=outs)

    dest, row_tok, block_e, n_used, n_blocks = _routing(top_idx, MOE_ROWS)
    xs = _gather_rows(h_all, row_tok, n_used, n_blocks, MOE_ROWS)
    rows_out = _experts(xs, block_e, n_used, w_up[l], b_up[l], w_down[l], b_down[l],
                        n_blocks, MOE_ROWS, MOE_FF_TILE)
    y_p, y_s = _combine(rows_out, dest, gate, h_all, ln2_g[l].reshape(1, D_MODEL),
                        ln2_b[l].reshape(1, D_MODEL), tm=min(COMBINE_TM, n_s), n_first_rows=n_p)

    return (y_p.reshape(bp, tp_, D_MODEL), y_s.reshape(bs, ts, D_MODEL),
            k1[None], v1[None], w1[None], s1[None], k2[None], v2[None], w2[None], s2[None])
```

```python
import functools
import math

import jax
import jax.numpy as jnp
from jax import lax
from jax.experimental import pallas as pl
from jax.experimental.pallas import tpu as pltpu

f32 = jnp.float32
bf16 = jnp.bfloat16

D_MODEL = 2048
CHUNK = 64
ATT_HEADS = 16
ATT_KV_HEADS = 2
HEAD_DIM = 64
ATT_GROUP = ATT_HEADS // ATT_KV_HEADS
ATT_W = ATT_HEADS * HEAD_DIM
KV_W = ATT_KV_HEADS * HEAD_DIM
ATT_PROJ = ATT_W + 2 * KV_W
WINDOW = 128
BAND = WINDOW + CHUNK
NUM_BUCKETS = 32
MAX_DISTANCE = 128
RW_HEAD = 64
RW_W = 1024
RW_HEADS = RW_W // RW_HEAD
DECAY_LORA = 96
ICLR_LORA = 96
GATE_LORA = 128
RW_PROJ = 3 * RW_W + DECAY_LORA + ICLR_LORA + GATE_LORA
GN_EPS = 64e-5
LN_EPS = 1e-5
N_EXPERTS = 32
TOP_K = 4
D_FF = D_MODEL
SWIGLU_LIMIT = 7.0
SWIGLU_ALPHA = 1.702
DEPTH = 1
DN_ALPHA = (2 * DEPTH) ** 0.25

LANES = 128
VMEM_LIMIT = 56 * 1024 * 1024

LORA_W = DECAY_LORA + ICLR_LORA + GATE_LORA
LORA_PAD = -(-LORA_W // LANES) * LANES
RW_PAD = 3 * RW_W + LORA_PAD
IN_PAD = ATT_PROJ + RW_PAD
N_PAIRS = RW_HEADS // 2
NEG_BIG = -1e30


def _cparams(sem):
    return pltpu.CompilerParams(dimension_semantics=sem, vmem_limit_bytes=VMEM_LIMIT)


def _inproj_kernel(x_ref, w_ref, q_ref, kv_ref, rw_ref):
    acc = jnp.dot(x_ref[...].astype(bf16), w_ref[...], preferred_element_type=f32)
    scale = HEAD_DIM ** -0.5
    for h in range(ATT_HEADS):
        q_ref[h] = (acc[:, h * HEAD_DIM:(h + 1) * HEAD_DIM] * scale).astype(bf16)
    kv_ref[...] = acc[:, ATT_W:ATT_PROJ]
    rw_ref[...] = acc[:, ATT_PROJ:IN_PAD]


def _inproj(x2, w_pad, tm):
    rows = x2.shape[0]
    assert rows % tm == 0
    return pl.pallas_call(
        _inproj_kernel,
        grid=(rows // tm,),
        in_specs=[
            pl.BlockSpec((tm, D_MODEL), lambda i: (i, 0)),
            pl.BlockSpec((D_MODEL, IN_PAD), lambda i: (0, 0), pipeline_mode=pl.Buffered(1)),
        ],
        out_specs=[
            pl.BlockSpec((ATT_HEADS, tm, HEAD_DIM), lambda i: (0, i, 0)),
            pl.BlockSpec((tm, 2 * KV_W), lambda i: (i, 0)),
            pl.BlockSpec((tm, RW_PAD), lambda i: (i, 0)),
        ],
        out_shape=[
            jax.ShapeDtypeStruct((ATT_HEADS, rows, HEAD_DIM), bf16),
            jax.ShapeDtypeStruct((rows, 2 * KV_W), f32),
            jax.ShapeDtypeStruct((rows, RW_PAD), f32),
        ],
        compiler_params=_cparams(("parallel",)),
    )(x2, w_pad)


def _attn_kernel(q_ref, kvm_ref, kva_ref, kvb_ref, bias_ref, sink_ref, o_ref, kvbuf,
                 *, nc, t_valid, hist_valid):
    j = pl.program_id(1)
    kvbuf[0:nc * CHUNK] = kvm_ref[...].astype(bf16)
    kvbuf[nc * CHUNK:(nc + 1) * CHUNK] = kva_ref[...].astype(bf16)
    kvbuf[(nc + 1) * CHUNK:(nc + 2) * CHUNK] = kvb_ref[...].astype(bf16)
    m_idx = lax.broadcasted_iota(jnp.int32, (1, 1, BAND), 2)

    def chunk(c, carry):
        r0 = pl.multiple_of(c * CHUNK, CHUNK)
        band = kvbuf[pl.ds(r0, BAND), :]
        idx = (j * nc + c) * CHUNK + m_idx
        valid = idx - WINDOW < t_valid
        if not hist_valid:
            valid = jnp.logical_and(valid, idx >= WINDOW)
        outs = []
        for g in range(ATT_KV_HEADS):
            kb = band[:, g * HEAD_DIM:(g + 1) * HEAD_DIM]
            vb = band[:, KV_W + g * HEAD_DIM:KV_W + (g + 1) * HEAD_DIM]
            qg = q_ref[g * ATT_GROUP:(g + 1) * ATT_GROUP, pl.ds(r0, CHUNK), :]
            qg = qg.reshape(ATT_GROUP * CHUNK, HEAD_DIM)
            s = lax.dot_general(qg, kb, (((1,), (1,)), ((), ())), preferred_element_type=f32)
            s = s.reshape(ATT_GROUP, CHUNK, BAND) + bias_ref[g * ATT_GROUP:(g + 1) * ATT_GROUP]
            s = jnp.where(valid, s, NEG_BIG)
            sk = sink_ref[g * ATT_GROUP:(g + 1) * ATT_GROUP]
            m = jnp.maximum(jnp.max(s, axis=-1, keepdims=True), sk)
            p = jnp.exp(s - m)
            den = jnp.sum(p, axis=-1, keepdims=True) + jnp.exp(sk - m)
            o = jnp.dot(p.reshape(ATT_GROUP * CHUNK, BAND).astype(bf16), vb,
                        preferred_element_type=f32)
            o = o.reshape(ATT_GROUP, CHUNK, HEAD_DIM) / den
            for h in range(ATT_GROUP):
                outs.append(o[h])
        o_ref[pl.ds(r0, CHUNK), :] = jnp.concatenate(outs, axis=-1).astype(bf16)
        return carry

    lax.fori_loop(0, nc, chunk, 0)


def _attention(q4, kvfull, bias, sinks3, *, nc, t_valid, hist_valid):
    _, b, tp, _ = q4.shape
    assert tp % (nc * CHUNK) == 0 and kvfull.shape[1] == tp + WINDOW
    nblk = tp // (nc * CHUNK)
    kern = functools.partial(_attn_kernel, nc=nc, t_valid=t_valid, hist_valid=hist_valid)
    return pl.pallas_call(
        kern,
        grid=(b, nblk),
        in_specs=[
            pl.BlockSpec((ATT_HEADS, None, nc * CHUNK, HEAD_DIM), lambda bi, j: (0, bi, j, 0)),
            pl.BlockSpec((None, nc * CHUNK, 2 * KV_W), lambda bi, j: (bi, j, 0)),
            pl.BlockSpec((None, CHUNK, 2 * KV_W), lambda bi, j: (bi, (j + 1) * nc, 0)),
            pl.BlockSpec((None, CHUNK, 2 * KV_W), lambda bi, j: (bi, (j + 1) * nc + 1, 0)),
            pl.BlockSpec((ATT_HEADS, CHUNK, BAND), lambda bi, j: (0, 0, 0)),
            pl.BlockSpec((ATT_HEADS, 1, 1), lambda bi, j: (0, 0, 0)),
        ],
        out_specs=pl.BlockSpec((None, nc * CHUNK, ATT_W), lambda bi, j: (bi, j, 0)),
        out_shape=jax.ShapeDtypeStruct((b, tp, ATT_W), bf16),
        scratch_shapes=[pltpu.VMEM(((nc + 2) * CHUNK, 2 * KV_W), bf16)],
        compiler_params=_cparams(("parallel", "parallel")),
    )(q4, kvfull, kvfull, kvfull, bias, sinks3)


def _rwkv_kernel(p_ref, shift0_ref, s0_ref, mu_ref, w0_ref, wd_ref, a0_ref, wa_ref, wg_ref,
                 kk_ref, ka_ref, rk_ref, lng_ref, lnb_ref, o_ref, sfin_ref, s_scr, last_scr,
                 *, t_valid, n_chunks):
    c = pl.program_id(1)
    L = CHUNK

    @pl.when(c == 0)
    def _():
        s_scr[...] = s0_ref[...]
        last_scr[...] = shift0_ref[...]

    p = p_ref[...]
    row = lax.broadcasted_iota(jnp.int32, (L, 1), 0)
    shifted = jnp.where(row == 0, last_scr[...], pltpu.roll(p, 1, axis=0))
    last_scr[...] = p[L - 1:L, :]
    xm = p + (shifted - p) * mu_ref[...]
    r = xm[:, 0:RW_W]
    k = xm[:, RW_W:2 * RW_W]
    v = xm[:, 2 * RW_W:3 * RW_W]
    tail = xm[:, 3 * RW_W:RW_PAD]

    def sigmoid(z):
        return 1.0 / (1.0 + jnp.exp(-z))

    w_log = w0_ref[...] + jnp.dot(jnp.tanh(tail).astype(bf16), wd_ref[...],
                                  preferred_element_type=f32)
    z = -w_log
    softplus = jnp.maximum(z, 0.0) + jnp.log(1.0 + jnp.exp(-jnp.abs(z)))
    ld = -jnp.exp(-softplus - 0.5)
    a = sigmoid(a0_ref[...] + jnp.dot(tail.astype(bf16), wa_ref[...], preferred_element_type=f32))
    g = jnp.dot(sigmoid(tail).astype(bf16), wg_ref[...], preferred_element_type=f32)

    lane = lax.broadcasted_iota(jnp.int32, (1, LANES), 1)
    lo_half = lane < RW_HEAD
    rr = lax.broadcasted_iota(jnp.int32, (LANES, LANES), 0)
    cc = lax.broadcasted_iota(jnp.int32, (LANES, LANES), 1)
    same_head = (rr // RW_HEAD) == (cc // RW_HEAD)
    ones_bd = jnp.where(same_head, 1.0, 0.0).astype(bf16)

    def seg_sums(xs):
        n = len(xs)
        x = jnp.concatenate(xs, axis=0) if n > 1 else xs[0]
        hi = x.astype(bf16)
        lo = (x - hi.astype(f32)).astype(bf16)
        both = jnp.concatenate([hi, lo], axis=0)
        m = 2 * n * L
        tiles = jnp.concatenate([both[:, LANES * t:LANES * (t + 1)] for t in range(N_PAIRS)], axis=0)
        res = jnp.dot(tiles, ones_bd, preferred_element_type=f32)
        y = jnp.concatenate([res[m * t:m * (t + 1)] for t in range(N_PAIRS)], axis=1)
        y = y[:n * L] + y[n * L:]
        return [y[i * L:(i + 1) * L] for i in range(n)]

    kk = k * kk_ref[...]
    k_mod = k * (1.0 + (a - 1.0) * ka_ref[...])
    nrm2, bonus_s = seg_sums([kk * kk, r * k_mod * rk_ref[...]])
    kk = kk / jnp.maximum(jnp.sqrt(nrm2), 1e-12)
    b = kk * a

    if t_valid % L != 0:
        live = (c * L + row) < t_valid
        ld = jnp.where(live, ld, 0.0)
        b = jnp.where(live, b, 0.0)
        k_mod = jnp.where(live, k_mod, 0.0)

    h1 = ld.astype(bf16)
    r1 = ld - h1.astype(f32)
    h2 = r1.astype(bf16)
    h3 = (r1 - h2.astype(f32)).astype(bf16)
    ti = lax.broadcasted_iota(jnp.int32, (L, 3 * L), 0)
    si = lax.broadcasted_iota(jnp.int32, (L, 3 * L), 1) % L
    tri3 = jnp.where(si <= ti, 1.0, 0.0).astype(bf16)
    cum = jnp.dot(tri3, jnp.concatenate([h1, h2, h3], axis=0), preferred_element_type=f32)
    cum_l = cum[L - 1:L, :]
    g_l = jnp.exp(cum_l)
    g_inv = jnp.exp(-cum)
    g_rest = jnp.exp(cum_l - cum)
    kq = (kk * jnp.exp(cum - ld)).astype(bf16)
    rq_f = r * jnp.exp(cum)
    rq = rq_f.astype(bf16)
    bt = (b * g_inv).astype(bf16)
    kt = (k_mod * g_inv).astype(bf16)
    bh = (b * g_rest).astype(bf16)
    kh = (k_mod * g_rest).astype(bf16)
    vb = v.astype(bf16)

    def bd(x):
        zero = jnp.zeros_like(x)
        return jnp.concatenate([jnp.where(lo_half, x, zero), jnp.where(lo_half, zero, x)], axis=0)

    def mm(x, y):
        return jnp.dot(x, y, preferred_element_type=f32)

    def mm_nt(x, y):
        return lax.dot_general(x, y, (((1,), (1,)), ((), ())), preferred_element_type=f32)

    def mm_tn(x, y):
        return lax.dot_general(x, y, (((0,), (0,)), ((), ())), preferred_element_type=f32)

    tt = lax.broadcasted_iota(jnp.int32, (L, LANES), 0)
    ss = lax.broadcasted_iota(jnp.int32, (L, LANES), 1) % RW_HEAD
    strict = ss < tt
    incl = ss <= tt
    eye_pair = jnp.where(ss == tt, 1.0, 0.0).astype(f32)

    P = range(N_PAIRS)
    sl = [slice(LANES * t, LANES * (t + 1)) for t in P]
    a_all = [mm_nt(jnp.concatenate([kq[:, sl[t]], rq[:, sl[t]]], axis=0),
                   jnp.concatenate([bd(bt[:, sl[t]]), bd(kt[:, sl[t]])], axis=0)) for t in P]
    a_bk = [jnp.where(strict, a_all[t][:L, :LANES], 0.0) for t in P]
    a_kk = [jnp.where(strict, a_all[t][:L, LANES:], 0.0).astype(bf16) for t in P]
    a_rb = [jnp.where(incl, a_all[t][L:, :LANES], 0.0).astype(bf16) for t in P]
    a_rk = [jnp.where(incl, a_all[t][L:, LANES:], 0.0).astype(bf16) for t in P]
    bd_v = [bd(vb[:, sl[t]]) for t in P]
    akv = [mm(a_kk[t], bd_v[t]).astype(bf16) for t in P]
    w_inv = [eye_pair - a_bk[t] for t in P]
    pw = [a_bk[t].astype(bf16) for t in P]
    pw_bd = [bd(pw[t]) for t in P]
    for _ in range(5):
        pw = [mm(pw[t], pw_bd[t]).astype(bf16) for t in P]
        pw_bd = [bd(pw[t]) for t in P]
        w_inv = [w_inv[t] + mm(w_inv[t].astype(bf16), pw_bd[t]) for t in P]
    qu = [mm(w_inv[t].astype(bf16), jnp.concatenate([bd(kq[:, sl[t]]), bd(akv[t])], axis=1))
          for t in P]
    q_m = [qu[t][:, :LANES].astype(bf16) for t in P]
    u_m = [qu[t][:, LANES:].astype(bf16) for t in P]
    m_full = [mm_tn(q_m[t], bh[:, sl[t]]) for t in P]
    neg_m = [jnp.where(same_head, -m_full[t], 0.0).astype(bf16) for t in P]
    c_full = [mm_tn(jnp.concatenate([vb[:, sl[t]], u_m[t]], axis=0),
                    jnp.concatenate([kh[:, sl[t]], -bh[:, sl[t]]], axis=0)) for t in P]
    go = [mm(a_rb[t], jnp.concatenate([bd(q_m[t]), bd(u_m[t])], axis=1)) for t in P]
    o_rk = [mm(a_rk[t], bd_v[t]) for t in P]
    s_old = [s_scr[t] for t in P]
    s_b = [s_old[t].astype(bf16) for t in P]
    g_m = [(rq_f[:, sl[t]] - go[t][:, :LANES]).astype(bf16) for t in P]
    o_tiles = [mm_nt(g_m[t], bd(s_b[t])) + (o_rk[t] - go[t][:, LANES:]) for t in P]
    s_upd = [mm(s_b[t], neg_m[t]) for t in P]
    for t in P:
        c_pair = jnp.where(lo_half, c_full[t][:RW_HEAD], c_full[t][RW_HEAD:])
        s_scr[t] = s_old[t] * g_l[:, sl[t]] + s_upd[t] + c_pair

    o = jnp.concatenate(o_tiles, axis=1)
    (o_sum,) = seg_sums([o])
    d = o - o_sum * (1.0 / RW_HEAD)
    (d2,) = seg_sums([d * d])
    on = d * lax.rsqrt(d2 * (1.0 / RW_HEAD) + GN_EPS) * lng_ref[...] + lnb_ref[...]
    o_ref[...] = ((on + bonus_s * v) * g).astype(bf16)

    @pl.when(c == n_chunks - 1)
    def _():
        sfin_ref[...] = s_scr[...]


def _rwkv(prw, shift0, s0_pair, wts, *, t_valid):
    b, tp, _ = prw.shape
    n_chunks = tp // CHUNK
    kern = functools.partial(_rwkv_kernel, t_valid=t_valid, n_chunks=n_chunks)
    const2 = lambda bi, c: (0, 0)
    row_spec = pl.BlockSpec((1, RW_W), const2)
    return pl.pallas_call(
        kern,
        grid=(b, n_chunks),
        in_specs=[
            pl.BlockSpec((None, CHUNK, RW_PAD), lambda bi, c: (bi, c, 0)),
            pl.BlockSpec((None, 1, RW_PAD), lambda bi, c: (bi, 0, 0)),
            pl.BlockSpec((None, N_PAIRS, RW_HEAD, LANES), lambda bi, c: (bi, 0, 0, 0)),
            pl.BlockSpec((1, RW_PAD), const2),
            row_spec,
            pl.BlockSpec((LORA_PAD, RW_W), const2),
            row_spec,
            pl.BlockSpec((LORA_PAD, RW_W), const2),
            pl.BlockSpec((LORA_PAD, RW_W), const2),
            row_spec, row_spec, row_spec, row_spec, row_spec,
        ],
        out_specs=[
            pl.BlockSpec((None, CHUNK, RW_W), lambda bi, c: (bi, c, 0)),
            pl.BlockSpec((None, N_PAIRS, RW_HEAD, LANES), lambda bi, c: (bi, 0, 0, 0)),
        ],
        out_shape=[
            jax.ShapeDtypeStruct((b, tp, RW_W), bf16),
            jax.ShapeDtypeStruct((b, N_PAIRS, RW_HEAD, LANES), f32),
        ],
        scratch_shapes=[pltpu.VMEM((N_PAIRS, RW_HEAD, LANES), f32), pltpu.VMEM((1, RW_PAD), f32)],
        compiler_params=_cparams(("parallel", "arbitrary")),
    )(prw, shift0, s0_pair, wts["mu"], wts["w0"], wts["wd"], wts["a0"], wts["wa"], wts["wg"],
      wts["k_k"], wts["k_a"], wts["r_k"], wts["lnx_g"], wts["lnx_b"])


def _layer_norm(z, g, b):
    mu = jnp.mean(z, axis=-1, keepdims=True)
    d = z - mu
    var = jnp.mean(d * d, axis=-1, keepdims=True)
    return d * lax.rsqrt(var + LN_EPS) * g + b


def _outproj_kernel(*refs, aliased):
    att_ref, rw_ref, x_ref, wo_ref, g_ref, b_ref, wrh_ref, wrl_ref, br_ref = refs[:9]
    h_ref, idx_ref, gate_ref = refs[9 + 3 * aliased:]
    mix = (jnp.dot(att_ref[...], wo_ref[0:ATT_W], preferred_element_type=f32)
           + jnp.dot(rw_ref[...], wo_ref[ATT_W:ATT_W + RW_W], preferred_element_type=f32))
    h = _layer_norm(DN_ALPHA * x_ref[...] + mix, g_ref[...], b_ref[...])
    h_ref[...] = h
    hh = h.astype(bf16)
    hl = (h - hh.astype(f32)).astype(bf16)
    logits = (jnp.dot(hh, wrh_ref[...], preferred_element_type=f32)
              + jnp.dot(hl, wrh_ref[...], preferred_element_type=f32)
              + jnp.dot(hh, wrl_ref[...], preferred_element_type=f32)) + br_ref[...]
    lane = lax.broadcasted_iota(jnp.int32, logits.shape, 1).astype(f32)
    vals, idxs = [], []
    cur = logits
    for _ in range(TOP_K):
        m = jnp.max(cur, axis=-1, keepdims=True)
        i = jnp.min(jnp.where(cur == m, lane, float(LANES)), axis=-1, keepdims=True)
        vals.append(m)
        idxs.append(i)
        cur = jnp.where(lane == i, -jnp.inf, cur)
    es = [jnp.exp(vv - vals[0]) for vv in vals]
    tot = es[0] + es[1] + es[2] + es[3]
    idx_ref[...] = jnp.concatenate(idxs, axis=-1).astype(jnp.int32)
    gate_ref[...] = jnp.concatenate([e / tot for e in es], axis=-1)


def _outproj(att2, rw2, x2, wts, *, tm, row0, total_rows, prev=None):
    rows = x2.shape[0]
    assert rows % tm == 0 and row0 % tm == 0
    blk0 = row0 // tm
    aliased = prev is not None
    const = lambda i: (0, 0)
    in_specs = [
        pl.BlockSpec((tm, ATT_W), lambda i: (i, 0)),
        pl.BlockSpec((tm, RW_W), lambda i: (i, 0)),
        pl.BlockSpec((tm, D_MODEL), lambda i: (i, 0)),
        pl.BlockSpec((D_MODEL, D_MODEL), const, pipeline_mode=pl.Buffered(1)),
        pl.BlockSpec((1, D_MODEL), const),
        pl.BlockSpec((1, D_MODEL), const),
        pl.BlockSpec((D_MODEL, LANES), const),
        pl.BlockSpec((D_MODEL, LANES), const),
        pl.BlockSpec((1, LANES), const),
    ]
    args = [att2, rw2, x2, wts["w_out"], wts["ln1_g"], wts["ln1_b"], wts["wr_hi"], wts["wr_lo"],
            wts["b_router"]]
    aliases = {}
    if aliased:
        in_specs += [pl.BlockSpec(memory_space=pl.ANY)] * 3
        args += list(prev)
        aliases = {9: 0, 10: 1, 11: 2}
    return pl.pallas_call(
        functools.partial(_outproj_kernel, aliased=int(aliased)),
        grid=(rows // tm,),
        in_specs=in_specs,
        out_specs=[
            pl.BlockSpec((tm, D_MODEL), lambda i: (blk0 + i, 0)),
            pl.BlockSpec((tm, TOP_K), lambda i: (blk0 + i, 0)),
            pl.BlockSpec((tm, TOP_K), lambda i: (blk0 + i, 0)),
        ],
        out_shape=[
            jax.ShapeDtypeStruct((total_rows, D_MODEL), f32),
            jax.ShapeDtypeStruct((total_rows, TOP_K), jnp.int32),
            jax.ShapeDtypeStruct((total_rows, TOP_K), f32),
        ],
        input_output_aliases=aliases,
        compiler_params=_cparams(("parallel",)),
    )(*args)


MOE_SUB = 256
MOE_NSUB = 4
MOE_SUPER = MOE_SUB * MOE_NSUB
MOE_FF_TILE = 256
MOE_MAX_PAD = N_EXPERTS * (MOE_SUB - 1)
X_SUBL = D_MODEL // (2 * LANES)
O_SUBL = D_MODEL // LANES


def _routing(top_idx):
    n = top_idx.shape[0]
    n_assign = n * TOP_K
    flat_e = top_idx.reshape(-1)
    onehot = (flat_e[:, None] == jnp.arange(N_EXPERTS, dtype=jnp.int32)[None, :]).astype(jnp.int32)
    csum = jnp.cumsum(onehot, axis=0)
    counts = csum[-1]
    rank = jnp.take_along_axis(csum, flat_e[:, None], axis=1)[:, 0] - 1
    padded = (counts + MOE_SUPER - 1) // MOE_SUPER * MOE_SUPER
    pad_end = jnp.cumsum(padded)
    start = pad_end - padded
    dest = (start[flat_e] + rank).astype(jnp.int32)
    n_super = (n_assign + N_EXPERTS * (MOE_SUPER - 1) + MOE_SUPER - 1) // MOE_SUPER
    s_row0 = jnp.arange(n_super, dtype=jnp.int32) * MOE_SUPER
    super_e = jnp.minimum(jnp.searchsorted(pad_end, s_row0, side="right"),
                          N_EXPERTS - 1).astype(jnp.int32)
    rows_here = jnp.clip(counts[super_e] - (s_row0 - start[super_e]), 0, MOE_SUPER)
    rows_here = jnp.where(s_row0 < pad_end[-1], rows_here, 0)
    n_sub = ((rows_here + MOE_SUB - 1) // MOE_SUB).astype(jnp.int32)
    n_used = (pad_end[-1] // MOE_SUPER).astype(jnp.int32).reshape(1)
    n_pad_e = (counts + MOE_SUB - 1) // MOE_SUB * MOE_SUB - counts
    pad_cum = jnp.cumsum(n_pad_e)
    kk = jnp.arange(MOE_MAX_PAD, dtype=jnp.int32)
    pe = jnp.minimum(jnp.searchsorted(pad_cum, kk, side="right"), N_EXPERTS - 1)
    pad_dest = (start[pe] + counts[pe] + (kk - (pad_cum[pe] - n_pad_e[pe]))).astype(jnp.int32)
    pad_dest = jnp.where(kk < pad_cum[-1], pad_dest, 0)
    n_pad = pad_cum[-1].astype(jnp.int32).reshape(1)
    return dest.reshape(n, TOP_K), super_e, n_sub, n_used, pad_dest, n_pad, n_super


def _tile_copy(src_ref, src_tok, dst_ref, dst_tok, subl, sem):
    s0 = pl.multiple_of(src_tok * subl, subl)
    d0 = pl.multiple_of(dst_tok * subl, subl)
    return pltpu.make_async_copy(src_ref.at[pl.ds(s0, subl)], dst_ref.at[pl.ds(d0, subl)], sem)


def _scatter_kernel(npad_ref, dest_ref, pad_ref, h_ref, xs_hbm, stage, zero, sems, pad_sem,
                    *, tm, n_steps):
    i = pl.program_id(0)
    slot = i % 2
    n_copy = TOP_K * tm

    def drain(sl):
        def body(t, carry):
            _tile_copy(stage.at[sl], 0, xs_hbm, 0, X_SUBL, sems.at[sl]).wait()
            return carry
        lax.fori_loop(0, n_copy, body, 0)

    @pl.when(i >= 2)
    def _():
        drain(slot)

    h = h_ref[...]
    half = D_MODEL // 2
    hi = lax.bitcast_convert_type(h[:, :half].astype(bf16).astype(f32), jnp.uint32)
    lo = lax.bitcast_convert_type(h[:, half:].astype(bf16).astype(f32), jnp.uint32)
    packed = hi | (lo >> 16)
    for l in range(X_SUBL):
        stage[slot, pl.ds(l, tm, stride=X_SUBL), :] = packed[:, LANES * l:LANES * (l + 1)]

    def issue(t, carry):
        for jx in range(TOP_K):
            _tile_copy(stage.at[slot], t, xs_hbm, dest_ref[0, 0, jx * tm + t], X_SUBL,
                       sems.at[slot]).start()
        return carry

    lax.fori_loop(0, tm, issue, 0)

    @pl.when(i == 0)
    def _():
        zero[...] = jnp.zeros_like(zero)

        def fill(k, carry):
            _tile_copy(zero, 0, xs_hbm, pad_ref[k], X_SUBL, pad_sem).start()
            return carry

        lax.fori_loop(0, npad_ref[0], fill, 0)

        def fill_wait(k, carry):
            _tile_copy(zero, 0, xs_hbm, 0, X_SUBL, pad_sem).wait()
            return carry

        lax.fori_loop(0, npad_ref[0], fill_wait, 0)

    @pl.when(i == n_steps - 1)
    def _():
        drain(slot)
        if n_steps > 1:
            drain(1 - slot)


def _scatter_rows(h_all, dest, pad_dest, n_pad, n_rows, tm):
    n = h_all.shape[0]
    assert n % tm == 0
    n_steps = n // tm
    dest_blk = dest.reshape(n_steps, tm, TOP_K).transpose(0, 2, 1).reshape(n_steps, 1, TOP_K * tm)
    kern = functools.partial(_scatter_kernel, tm=tm, n_steps=n_steps)
    return pl.pallas_call(
        kern,
        grid_spec=pltpu.PrefetchScalarGridSpec(
            num_scalar_prefetch=1,
            grid=(n_steps,),
            in_specs=[
                pl.BlockSpec((1, 1, TOP_K * tm), lambda i, npad: (i, 0, 0), memory_space=pltpu.SMEM),
                pl.BlockSpec(memory_space=pltpu.SMEM),
                pl.BlockSpec((tm, D_MODEL), lambda i, npad: (i, 0)),
            ],
            out_specs=pl.BlockSpec(memory_space=pl.ANY),
            scratch_shapes=[
                pltpu.VMEM((2, tm * X_SUBL, LANES), jnp.uint32),
                pltpu.VMEM((X_SUBL, LANES), jnp.uint32),
                pltpu.SemaphoreType.DMA((2,)),
                pltpu.SemaphoreType.DMA(()),
            ],
        ),
        out_shape=jax.ShapeDtypeStruct((n_rows * X_SUBL, LANES), jnp.uint32),
        compiler_params=_cparams(("arbitrary",)),
    )(n_pad, dest_blk, pad_dest, h_all)


def _expert_kernel(se_ref, nsub_ref, nused_ref, x_ref, wg_ref, wl_ref, bg_ref, bl_ref, wd_ref,
                   bd_ref, o_ref, xb_scr, acc_scr, *, nf):
    s = pl.program_id(0)
    j = pl.program_id(1)
    n_sub = nsub_ref[s]
    half = D_MODEL // 2
    for r in range(MOE_NSUB):
        r0 = r * MOE_SUB

        @pl.when(r < n_sub)
        def _(r0=r0):
            rows = pl.ds(r0, MOE_SUB)

            @pl.when(j == 0)
            def _():
                for l in range(X_SUBL):
                    u = x_ref[pl.ds(r0 * X_SUBL + l, MOE_SUB, stride=X_SUBL), :]
                    hi = lax.bitcast_convert_type(u & jnp.uint32(0xFFFF0000), f32)
                    lo = lax.bitcast_convert_type(u << 16, f32)
                    xb_scr[rows, LANES * l:LANES * (l + 1)] = hi.astype(bf16)
                    xb_scr[rows, half + LANES * l:half + LANES * (l + 1)] = lo.astype(bf16)

            xb = xb_scr[rows, :]
            hg = jnp.dot(xb, wg_ref[...].astype(bf16), preferred_element_type=f32) + bg_ref[...]
            hl = jnp.dot(xb, wl_ref[...].astype(bf16), preferred_element_type=f32) + bl_ref[...]
            glu = jnp.minimum(hg, SWIGLU_LIMIT)
            lin = jnp.clip(hl, -SWIGLU_LIMIT, SWIGLU_LIMIT)
            act = glu * (1.0 / (1.0 + jnp.exp(-SWIGLU_ALPHA * glu))) * (lin + 1.0)
            part = jnp.dot(act.astype(bf16), wd_ref[...].astype(bf16), preferred_element_type=f32)

            @pl.when(j == 0)
            def _():
                acc_scr[rows, :] = part + bd_ref[...]

            @pl.when(j != 0)
            def _():
                acc_scr[rows, :] += part

            @pl.when(j == nf - 1)
            def _():
                a = acc_scr[rows, :]
                for l in range(O_SUBL):
                    o_ref[pl.ds(r0 * O_SUBL + l, MOE_SUB, stride=O_SUBL), :] = (
                        a[:, LANES * l:LANES * (l + 1)])


def _experts(xs, super_e, n_sub, n_used, w_up, b_up, w_down, b_down, n_super):
    tf = MOE_FF_TILE
    nf = D_FF // tf
    last = lambda s, nu: jnp.minimum(s, nu[0] - 1)
    b_up3 = b_up.reshape(N_EXPERTS, 1, 2 * D_FF)
    b_down3 = b_down.reshape(N_EXPERTS, 1, D_MODEL)
    e_of = lambda s, se, nu: se[last(s, nu)]
    return pl.pallas_call(
        functools.partial(_expert_kernel, nf=nf),
        grid_spec=pltpu.PrefetchScalarGridSpec(
            num_scalar_prefetch=3,
            grid=(n_super, nf),
            in_specs=[
                pl.BlockSpec((MOE_SUPER * X_SUBL, LANES), lambda s, j, se, ns, nu: (last(s, nu), 0)),
                pl.BlockSpec((None, D_MODEL, tf), lambda s, j, se, ns, nu: (e_of(s, se, nu), 0, j)),
                pl.BlockSpec((None, D_MODEL, tf),
                             lambda s, j, se, ns, nu: (e_of(s, se, nu), 0, nf + j)),
                pl.BlockSpec((None, 1, tf), lambda s, j, se, ns, nu: (e_of(s, se, nu), 0, j)),
                pl.BlockSpec((None, 1, tf), lambda s, j, se, ns, nu: (e_of(s, se, nu), 0, nf + j)),
                pl.BlockSpec((None, tf, D_MODEL), lambda s, j, se, ns, nu: (e_of(s, se, nu), j, 0)),
                pl.BlockSpec((None, 1, D_MODEL), lambda s, j, se, ns, nu: (e_of(s, se, nu), 0, 0)),
            ],
            out_specs=pl.BlockSpec((MOE_SUPER * O_SUBL, LANES),
                                   lambda s, j, se, ns, nu: (last(s, nu), 0)),
            scratch_shapes=[pltpu.VMEM((MOE_SUPER, D_MODEL), bf16),
                            pltpu.VMEM((MOE_SUPER, D_MODEL), f32)],
        ),
        out_shape=jax.ShapeDtypeStruct((n_super * MOE_SUPER * O_SUBL, LANES), f32),
        compiler_params=_cparams(("arbitrary", "arbitrary")),
    )(super_e, n_sub, n_used, xs, w_up, w_up, b_up3, b_up3, w_down, b_down3)


def _combine_kernel(dest_ref, nxt_ref, gate_ref, h_ref, rows_hbm, g_ref, b_ref, yp_ref, ys_ref,
                    buf, sems, *, tm, n_first, n_steps):
    i = pl.program_id(0)
    slot = i % 2

    def fetch(idx_ref, sl):
        def body(t, carry):
            for jx in range(TOP_K):
                _tile_copy(rows_hbm, idx_ref[0, 0, jx * tm + t], buf.at[sl, jx], t, O_SUBL,
                           sems.at[sl]).start()
            return carry
        lax.fori_loop(0, tm, body, 0)

    @pl.when(i == 0)
    def _():
        fetch(dest_ref, slot)

    @pl.when(i + 1 < n_steps)
    def _():
        fetch(nxt_ref, 1 - slot)

    def drain(t, carry):
        for jx in range(TOP_K):
            _tile_copy(rows_hbm, 0, buf.at[slot, jx], t, O_SUBL, sems.at[slot]).wait()
        return carry

    lax.fori_loop(0, tm, drain, 0)
    gate = gate_ref[...]
    cols = []
    for l in range(O_SUBL):
        acc = gate[:, 0:1] * buf[slot, 0, pl.ds(l, tm, stride=O_SUBL), :]
        for jx in range(1, TOP_K):
            acc = acc + gate[:, jx:jx + 1] * buf[slot, jx, pl.ds(l, tm, stride=O_SUBL), :]
        cols.append(acc)
    y = jnp.concatenate(cols, axis=-1)
    out = _layer_norm(DN_ALPHA * h_ref[...] + y, g_ref[...], b_ref[...])

    @pl.when(i < n_first)
    def _():
        yp_ref[...] = out

    @pl.when(i >= n_first)
    def _():
        ys_ref[...] = out


def _combine(rows_out, dest, gate, h_all, ln_g, ln_b, *, tm, n_first_rows):
    n = h_all.shape[0]
    assert n % tm == 0 and n_first_rows % tm == 0
    nblk = n // tm
    n_first = n_first_rows // tm
    dest_blk = dest.reshape(nblk, tm, TOP_K).transpose(0, 2, 1).reshape(nblk, 1, TOP_K * tm)
    kern = functools.partial(_combine_kernel, tm=tm, n_first=n_first, n_steps=nblk)
    const = lambda i: (0, 0)
    return pl.pallas_call(
        kern,
        grid=(nblk,),
        in_specs=[
            pl.BlockSpec((1, 1, TOP_K * tm), lambda i: (i, 0, 0), memory_space=pltpu.SMEM),
            pl.BlockSpec((1, 1, TOP_K * tm), lambda i: (jnp.minimum(i + 1, nblk - 1), 0, 0),
                         memory_space=pltpu.SMEM),
            pl.BlockSpec((tm, TOP_K), lambda i: (i, 0)),
            pl.BlockSpec((tm, D_MODEL), lambda i: (i, 0)),
            pl.BlockSpec(memory_space=pl.ANY),
            pl.BlockSpec((1, D_MODEL), const),
            pl.BlockSpec((1, D_MODEL), const),
        ],
        out_specs=[
            pl.BlockSpec((tm, D_MODEL), lambda i: (jnp.minimum(i, n_first - 1), 0)),
            pl.BlockSpec((tm, D_MODEL), lambda i: (jnp.maximum(i - n_first, 0), 0)),
        ],
        out_shape=[
            jax.ShapeDtypeStruct((n_first_rows, D_MODEL), f32),
            jax.ShapeDtypeStruct((n - n_first_rows, D_MODEL), f32),
        ],
        scratch_shapes=[pltpu.VMEM((2, TOP_K, tm * O_SUBL, LANES), f32),
                        pltpu.SemaphoreType.DMA((2,))],
        compiler_params=_cparams(("arbitrary",)),
    )(dest_blk, dest_blk, gate, h_all, rows_out, ln_g, ln_b)


def _t5_bucket(rel):
    half = NUM_BUCKETS // 2
    exact = half // 2
    n = jnp.abs(rel)
    log_part = exact + (jnp.log(jnp.maximum(n, 1).astype(jnp.float32) / exact)
                        / math.log(MAX_DISTANCE / exact) * (half - exact)).astype(jnp.int32)
    log_part = jnp.minimum(log_part, half - 1)
    return jnp.where(rel > 0, half, 0) + jnp.where(n < exact, n, log_part)


def _band_bias(rel_bias):
    qi = jnp.arange(CHUNK)[:, None]
    km = jnp.arange(BAND)[None, :]
    bucket = _t5_bucket(km - WINDOW - qi)
    return jnp.transpose(rel_bias[bucket], (2, 0, 1)).astype(jnp.float32)


def _pad_cols(a, width):
    return jnp.pad(a, ((0, 0), (0, width - a.shape[-1])))


def _pair_state(s):
    b = s.shape[0]
    return s.reshape(b, N_PAIRS, 2, RW_HEAD, RW_HEAD).transpose(0, 1, 3, 2, 4).reshape(
        b, N_PAIRS, RW_HEAD, LANES)


def _unpair_state(s):
    b = s.shape[0]
    return s.reshape(b, N_PAIRS, RW_HEAD, 2, RW_HEAD).transpose(0, 1, 3, 2, 4).reshape(
        b, RW_HEADS, RW_HEAD, RW_HEAD)


def _mix_group(x, k_hist, v_hist, hist_valid, wkv0, shift0, bias, sinks3, w_in_pad, rw_wts,
               *, in_tm, attn_nc):
    b, t, _ = x.shape
    q, kv, prw = _inproj(x.reshape(b * t, D_MODEL), w_in_pad, in_tm)
    tp = -(-t // (attn_nc * CHUNK)) * (attn_nc * CHUNK)
    q4 = q.reshape(ATT_HEADS, b, t, HEAD_DIM)
    kv3 = kv.reshape(b, t, 2 * KV_W)
    prw3 = prw.reshape(b, t, RW_PAD)
    if tp != t:
        q4 = jnp.pad(q4, ((0, 0), (0, 0), (0, tp - t), (0, 0)))
        prw3 = jnp.pad(prw3, ((0, 0), (0, tp - t), (0, 0)))
    hist = jnp.concatenate([k_hist.reshape(b, WINDOW, KV_W), v_hist.reshape(b, WINDOW, KV_W)], axis=-1)
    kvfull = jnp.concatenate([hist, kv3, jnp.zeros((b, tp - t, 2 * KV_W), f32)], axis=1)
    att = _attention(q4, kvfull, bias, sinks3, nc=attn_nc, t_valid=t, hist_valid=hist_valid)
    rw, s_fin = _rwkv(prw3, _pad_cols(shift0.reshape(b, RW_PROJ), RW_PAD).reshape(b, 1, RW_PAD),
                      _pair_state(wkv0.astype(f32)), rw_wts, t_valid=t)
    new_kv = kvfull[:, t:t + WINDOW]
    new_k = new_kv[..., :KV_W].reshape(b, WINDOW, ATT_KV_HEADS, HEAD_DIM)
    new_v = new_kv[..., KV_W:].reshape(b, WINDOW, ATT_KV_HEADS, HEAD_DIM)
    shift = prw3[:, t - 1:t, :RW_PROJ]
    return (att[:, :t].reshape(b * t, ATT_W), rw[:, :t].reshape(b * t, RW_W),
            new_k, new_v, _unpair_state(s_fin), shift)


SCATTER_TM = 128
COMBINE_TM = 128


def kernel(x_prompt, x_sample, cache_k, cache_v, state_wkv, state_shift, rel_bias, w_in, attn_sinks, rw_mu, rw_w0, rw_decay_up, rw_a0, rw_iclr_up, rw_gate_up, rw_k_k, rw_k_a, rw_r_k, rw_lnx_g, rw_lnx_b, w_out, ln1_g, ln1_b, w_router, b_router, w_up, b_up, w_down, b_down, ln2_g, ln2_b):
    assert w_in.shape[0] == DEPTH == 1
    l = 0
    bp, tp_, _ = x_prompt.shape
    bs, ts, _ = x_sample.shape
    bias = _band_bias(rel_bias)
    sinks3 = attn_sinks[l].astype(f32).reshape(ATT_HEADS, 1, 1)

    w_in_pad = _pad_cols(w_in[l], IN_PAD).astype(bf16)

    def lora_rows(w, row0):
        return jnp.zeros((LORA_PAD, RW_W), f32).at[row0:row0 + w.shape[0]].set(w).astype(bf16)

    rw_wts = {
        "mu": _pad_cols(rw_mu[l].reshape(1, RW_PROJ), RW_PAD),
        "w0": rw_w0[l].reshape(1, RW_W),
        "wd": lora_rows(rw_decay_up[l], 0),
        "a0": rw_a0[l].reshape(1, RW_W),
        "wa": lora_rows(rw_iclr_up[l], DECAY_LORA),
        "wg": lora_rows(rw_gate_up[l], DECAY_LORA + ICLR_LORA),
        "k_k": rw_k_k[l].reshape(1, RW_W),
        "k_a": rw_k_a[l].reshape(1, RW_W),
        "r_k": rw_r_k[l].reshape(1, RW_W),
        "lnx_g": rw_lnx_g[l].reshape(1, RW_W),
        "lnx_b": rw_lnx_b[l].reshape(1, RW_W),
    }
    wr = _pad_cols(w_router[l], LANES)
    wr_hi = wr.astype(bf16)
    op_wts = {
        "w_out": w_out[l].astype(bf16),
        "ln1_g": ln1_g[l].reshape(1, D_MODEL),
        "ln1_b": ln1_b[l].reshape(1, D_MODEL),
        "wr_hi": wr_hi,
        "wr_lo": (wr - wr_hi.astype(f32)).astype(bf16),
        "b_router": jnp.concatenate([b_router[l].astype(f32),
                                     jnp.full((LANES - N_EXPERTS,), NEG_BIG, f32)]).reshape(1, LANES),
    }

    zero_kv = jnp.zeros((bp, WINDOW, ATT_KV_HEADS, HEAD_DIM), f32)
    att_p, rwo_p, k1, v1, w1, s1 = _mix_group(
        x_prompt, zero_kv, zero_kv, False, jnp.zeros((bp, RW_HEADS, RW_HEAD, RW_HEAD), f32),
        jnp.zeros((bp, 1, RW_PROJ), f32), bias, sinks3, w_in_pad, rw_wts,
        in_tm=min(256, bp * tp_), attn_nc=min(8, -(-tp_ // CHUNK)))
    att_s, rwo_s, k2, v2, w2, s2 = _mix_group(
        x_sample, cache_k[l], cache_v[l], True, state_wkv[l], state_shift[l], bias, sinks3,
        w_in_pad, rw_wts, in_tm=min(256, bs * ts), attn_nc=1)

    n_p, n_s = bp * tp_, bs * ts
    n_all = n_p + n_s
    tm_p, tm_s = min(256, n_p), min(128, n_s)
    outs = _outproj(att_p, rwo_p, x_prompt.reshape(n_p, D_MODEL), op_wts,
                    tm=tm_p, row0=0, total_rows=n_all)
    h_all, top_idx, gate = _outproj(att_s, rwo_s, x_sample.reshape(n_s, D_MODEL), op_wts,
                                    tm=tm_s, row0=n_p, total_rows=n_all, prev=outs)

    dest, super_e, n_sub, n_used, pad_dest, n_pad, n_super = _routing(top_idx)
    xs = _scatter_rows(h_all, dest, pad_dest, n_pad, n_super * MOE_SUPER, min(SCATTER_TM, n_s))
    rows_out = _experts(xs, super_e, n_sub, n_used, w_up[l], b_up[l], w_down[l], b_down[l],
                        n_super)
    y_p, y_s = _combine(rows_out, dest, gate, h_all, ln2_g[l].reshape(1, D_MODEL),
                        ln2_b[l].reshape(1, D_MODEL), tm=min(COMBINE_TM, n_s), n_first_rows=n_p)

    return (y_p.reshape(bp, tp_, D_MODEL), y_s.reshape(bs, ts, D_MODEL),
            k1[None], v1[None], w1[None], s1[None], k2[None], v2[None], w2[None], s2[None])
```

```python
import functools
import math

import jax
import jax.numpy as jnp
from jax import lax
from jax.experimental import pallas as pl
from jax.experimental.pallas import tpu as pltpu

f32 = jnp.float32
bf16 = jnp.bfloat16

D_MODEL = 2048
CHUNK = 64
ATT_HEADS = 16
ATT_KV_HEADS = 2
HEAD_DIM = 64
ATT_GROUP = ATT_HEADS // ATT_KV_HEADS
ATT_W = ATT_HEADS * HEAD_DIM
KV_W = ATT_KV_HEADS * HEAD_DIM
ATT_PROJ = ATT_W + 2 * KV_W
WINDOW = 128
BAND = WINDOW + CHUNK
NUM_BUCKETS = 32
MAX_DISTANCE = 128
RW_HEAD = 64
RW_W = 1024
RW_HEADS = RW_W // RW_HEAD
DECAY_LORA = 96
ICLR_LORA = 96
GATE_LORA = 128
RW_PROJ = 3 * RW_W + DECAY_LORA + ICLR_LORA + GATE_LORA
GN_EPS = 64e-5
LN_EPS = 1e-5
N_EXPERTS = 32
TOP_K = 4
D_FF = D_MODEL
SWIGLU_LIMIT = 7.0
SWIGLU_ALPHA = 1.702
DEPTH = 1
DN_ALPHA = (2 * DEPTH) ** 0.25

LANES = 128
VMEM_LIMIT = 56 * 1024 * 1024

LORA_W = DECAY_LORA + ICLR_LORA + GATE_LORA
LORA_PAD = -(-LORA_W // LANES) * LANES
RW_PAD = 3 * RW_W + LORA_PAD
IN_PAD = ATT_PROJ + RW_PAD
N_PAIRS = RW_HEADS // 2
NEG_BIG = -1e30


def _cparams(sem):
    return pltpu.CompilerParams(dimension_semantics=sem, vmem_limit_bytes=VMEM_LIMIT)


def _inproj_kernel(x_ref, w_ref, q_ref, kv_ref, rw_ref):
    acc = jnp.dot(x_ref[...].astype(bf16), w_ref[...], preferred_element_type=f32)
    scale = HEAD_DIM ** -0.5
    for h in range(ATT_HEADS):
        q_ref[h] = (acc[:, h * HEAD_DIM:(h + 1) * HEAD_DIM] * scale).astype(bf16)
    kv_ref[...] = acc[:, ATT_W:ATT_PROJ]
    rw_ref[...] = acc[:, ATT_PROJ:IN_PAD]


def _inproj(x2, w_pad, tm):
    rows = x2.shape[0]
    assert rows % tm == 0
    return pl.pallas_call(
        _inproj_kernel,
        grid=(rows // tm,),
        in_specs=[
            pl.BlockSpec((tm, D_MODEL), lambda i: (i, 0)),
            pl.BlockSpec((D_MODEL, IN_PAD), lambda i: (0, 0), pipeline_mode=pl.Buffered(1)),
        ],
        out_specs=[
            pl.BlockSpec((ATT_HEADS, tm, HEAD_DIM), lambda i: (0, i, 0)),
            pl.BlockSpec((tm, 2 * KV_W), lambda i: (i, 0)),
            pl.BlockSpec((tm, RW_PAD), lambda i: (i, 0)),
        ],
        out_shape=[
            jax.ShapeDtypeStruct((ATT_HEADS, rows, HEAD_DIM), bf16),
            jax.ShapeDtypeStruct((rows, 2 * KV_W), f32),
            jax.ShapeDtypeStruct((rows, RW_PAD), f32),
        ],
        compiler_params=_cparams(("parallel",)),
    )(x2, w_pad)


ATT_UNROLL = 4


def _attn_kernel(q_ref, kvm_ref, kva_ref, kvb_ref, bias_ref, sink_ref, o_ref, kvbuf,
                 *, nc, t_valid, hist_valid):
    j = pl.program_id(1)
    kvbuf[0:nc * CHUNK] = kvm_ref[...].astype(bf16)
    kvbuf[nc * CHUNK:(nc + 1) * CHUNK] = kva_ref[...].astype(bf16)
    kvbuf[(nc + 1) * CHUNK:(nc + 2) * CHUNK] = kvb_ref[...].astype(bf16)
    m_idx = lax.broadcasted_iota(jnp.int32, (1, 1, BAND), 2)
    for c0 in range(0, nc, ATT_UNROLL):
        items = [(c, g) for c in range(c0, min(c0 + ATT_UNROLL, nc)) for g in range(ATT_KV_HEADS)]
        bands = {c: kvbuf[c * CHUNK:c * CHUNK + BAND, :] for c, _ in items}
        valid = {}
        for c in bands:
            idx = (j * nc + c) * CHUNK + m_idx
            v = idx - WINDOW < t_valid
            valid[c] = v if hist_valid else jnp.logical_and(v, idx >= WINDOW)
        s = [lax.dot_general(
                q_ref[g * ATT_GROUP:(g + 1) * ATT_GROUP, c * CHUNK:(c + 1) * CHUNK, :].reshape(
                    ATT_GROUP * CHUNK, HEAD_DIM),
                bands[c][:, g * HEAD_DIM:(g + 1) * HEAD_DIM],
                (((1,), (1,)), ((), ())), preferred_element_type=f32) for c, g in items]
        s = [jnp.where(valid[c], s[i].reshape(ATT_GROUP, CHUNK, BAND)
                       + bias_ref[g * ATT_GROUP:(g + 1) * ATT_GROUP], NEG_BIG)
             for i, (c, g) in enumerate(items)]
        sk = [sink_ref[g * ATT_GROUP:(g + 1) * ATT_GROUP] for _, g in items]
        m = [jnp.maximum(jnp.max(s[i], axis=-1, keepdims=True), sk[i]) for i in range(len(items))]
        p = [jnp.exp(s[i] - m[i]) for i in range(len(items))]
        den = [jnp.sum(p[i], axis=-1, keepdims=True) + jnp.exp(sk[i] - m[i])
               for i in range(len(items))]
        o = [jnp.dot(p[i].reshape(ATT_GROUP * CHUNK, BAND).astype(bf16),
                     bands[c][:, KV_W + g * HEAD_DIM:KV_W + (g + 1) * HEAD_DIM],
                     preferred_element_type=f32).reshape(ATT_GROUP, CHUNK, HEAD_DIM) / den[i]
             for i, (c, g) in enumerate(items)]
        for c in bands:
            heads = [o[i][h] for i, (ci, _) in enumerate(items) if ci == c for h in range(ATT_GROUP)]
            o_ref[c * CHUNK:(c + 1) * CHUNK, :] = jnp.concatenate(heads, axis=-1).astype(bf16)


def _attention(q4, kvfull, bias, sinks3, *, nc, t_valid, hist_valid):
    _, b, tp, _ = q4.shape
    assert tp % (nc * CHUNK) == 0 and kvfull.shape[1] == tp + WINDOW
    nblk = tp // (nc * CHUNK)
    kern = functools.partial(_attn_kernel, nc=nc, t_valid=t_valid, hist_valid=hist_valid)
    return pl.pallas_call(
        kern,
        grid=(b, nblk),
        in_specs=[
            pl.BlockSpec((ATT_HEADS, None, nc * CHUNK, HEAD_DIM), lambda bi, j: (0, bi, j, 0)),
            pl.BlockSpec((None, nc * CHUNK, 2 * KV_W), lambda bi, j: (bi, j, 0)),
            pl.BlockSpec((None, CHUNK, 2 * KV_W), lambda bi, j: (bi, (j + 1) * nc, 0)),
            pl.BlockSpec((None, CHUNK, 2 * KV_W), lambda bi, j: (bi, (j + 1) * nc + 1, 0)),
            pl.BlockSpec((ATT_HEADS, CHUNK, BAND), lambda bi, j: (0, 0, 0)),
            pl.BlockSpec((ATT_HEADS, 1, 1), lambda bi, j: (0, 0, 0)),
        ],
        out_specs=pl.BlockSpec((None, nc * CHUNK, ATT_W), lambda bi, j: (bi, j, 0)),
        out_shape=jax.ShapeDtypeStruct((b, tp, ATT_W), bf16),
        scratch_shapes=[pltpu.VMEM(((nc + 2) * CHUNK, 2 * KV_W), bf16)],
        compiler_params=_cparams(("parallel", "parallel")),
    )(q4, kvfull, kvfull, kvfull, bias, sinks3)


def _rwkv_kernel(p_ref, shift0_ref, s0_ref, mu_ref, w0_ref, wd_ref, a0_ref, wa_ref, wg_ref,
                 kk_ref, ka_ref, rk_ref, lng_ref, lnb_ref, o_ref, sfin_ref, s_scr, last_scr,
                 *, t_valid, n_chunks):
    c = pl.program_id(1)
    L = CHUNK

    @pl.when(c == 0)
    def _():
        s_scr[...] = s0_ref[...]
        last_scr[...] = shift0_ref[...]

    p = p_ref[...]
    row = lax.broadcasted_iota(jnp.int32, (L, 1), 0)
    shifted = jnp.where(row == 0, last_scr[...], pltpu.roll(p, 1, axis=0))
    last_scr[...] = p[L - 1:L, :]
    xm = p + (shifted - p) * mu_ref[...]
    r = xm[:, 0:RW_W]
    k = xm[:, RW_W:2 * RW_W]
    v = xm[:, 2 * RW_W:3 * RW_W]
    tail = xm[:, 3 * RW_W:RW_PAD]

    def sigmoid(z):
        return 1.0 / (1.0 + jnp.exp(-z))

    w_log = w0_ref[...] + jnp.dot(jnp.tanh(tail).astype(bf16), wd_ref[...],
                                  preferred_element_type=f32)
    z = -w_log
    softplus = jnp.maximum(z, 0.0) + jnp.log(1.0 + jnp.exp(-jnp.abs(z)))
    ld = -jnp.exp(-softplus - 0.5)
    a = sigmoid(a0_ref[...] + jnp.dot(tail.astype(bf16), wa_ref[...], preferred_element_type=f32))
    g = jnp.dot(sigmoid(tail).astype(bf16), wg_ref[...], preferred_element_type=f32)

    lane = lax.broadcasted_iota(jnp.int32, (1, LANES), 1)
    lo_half = lane < RW_HEAD
    rr = lax.broadcasted_iota(jnp.int32, (LANES, LANES), 0)
    cc = lax.broadcasted_iota(jnp.int32, (LANES, LANES), 1)
    same_head = (rr // RW_HEAD) == (cc // RW_HEAD)
    ones_bd = jnp.where(same_head, 1.0, 0.0).astype(bf16)

    def seg_sums(xs):
        n = len(xs)
        x = jnp.concatenate(xs, axis=0) if n > 1 else xs[0]
        hi = x.astype(bf16)
        lo = (x - hi.astype(f32)).astype(bf16)
        both = jnp.concatenate([hi, lo], axis=0)
        m = 2 * n * L
        tiles = jnp.concatenate([both[:, LANES * t:LANES * (t + 1)] for t in range(N_PAIRS)], axis=0)
        res = jnp.dot(tiles, ones_bd, preferred_element_type=f32)
        y = jnp.concatenate([res[m * t:m * (t + 1)] for t in range(N_PAIRS)], axis=1)
        y = y[:n * L] + y[n * L:]
        return [y[i * L:(i + 1) * L] for i in range(n)]

    kk = k * kk_ref[...]
    k_mod = k * (1.0 + (a - 1.0) * ka_ref[...])
    nrm2, bonus_s = seg_sums([kk * kk, r * k_mod * rk_ref[...]])
    kk = kk / jnp.maximum(jnp.sqrt(nrm2), 1e-12)
    b = kk * a

    if t_valid % L != 0:
        live = (c * L + row) < t_valid
        ld = jnp.where(live, ld, 0.0)
        b = jnp.where(live, b, 0.0)
        k_mod = jnp.where(live, k_mod, 0.0)

    h1 = ld.astype(bf16)
    r1 = ld - h1.astype(f32)
    h2 = r1.astype(bf16)
    h3 = (r1 - h2.astype(f32)).astype(bf16)
    ti = lax.broadcasted_iota(jnp.int32, (L, 3 * L), 0)
    si = lax.broadcasted_iota(jnp.int32, (L, 3 * L), 1) % L
    tri3 = jnp.where(si <= ti, 1.0, 0.0).astype(bf16)
    cum = jnp.dot(tri3, jnp.concatenate([h1, h2, h3], axis=0), preferred_element_type=f32)
    cum_l = cum[L - 1:L, :]
    g_l = jnp.exp(cum_l)
    g_inv = jnp.exp(-cum)
    g_rest = jnp.exp(cum_l - cum)
    kq = (kk * jnp.exp(cum - ld)).astype(bf16)
    rq_f = r * jnp.exp(cum)
    rq = rq_f.astype(bf16)
    bt = (b * g_inv).astype(bf16)
    kt = (k_mod * g_inv).astype(bf16)
    bh = (b * g_rest).astype(bf16)
    kh = (k_mod * g_rest).astype(bf16)
    vb = v.astype(bf16)

    def bd(x):
        zero = jnp.zeros_like(x)
        return jnp.concatenate([jnp.where(lo_half, x, zero), jnp.where(lo_half, zero, x)], axis=0)

    def mm(x, y):
        return jnp.dot(x, y, preferred_element_type=f32)

    def mm_nt(x, y):
        return lax.dot_general(x, y, (((1,), (1,)), ((), ())), preferred_element_type=f32)

    def mm_tn(x, y):
        return lax.dot_general(x, y, (((0,), (0,)), ((), ())), preferred_element_type=f32)

    tt = lax.broadcasted_iota(jnp.int32, (L, LANES), 0)
    ss = lax.broadcasted_iota(jnp.int32, (L, LANES), 1) % RW_HEAD
    strict = ss < tt
    incl = ss <= tt
    eye_pair = jnp.where(ss == tt, 1.0, 0.0).astype(f32)

    P = range(N_PAIRS)
    sl = [slice(LANES * t, LANES * (t + 1)) for t in P]
    a_all = [mm_nt(jnp.concatenate([kq[:, sl[t]], rq[:, sl[t]]], axis=0),
                   jnp.concatenate([bd(bt[:, sl[t]]), bd(kt[:, sl[t]])], axis=0)) for t in P]
    a_bk = [jnp.where(strict, a_all[t][:L, :LANES], 0.0) for t in P]
    a_kk = [jnp.where(strict, a_all[t][:L, LANES:], 0.0).astype(bf16) for t in P]
    a_rb = [jnp.where(incl, a_all[t][L:, :LANES], 0.0).astype(bf16) for t in P]
    a_rk = [jnp.where(incl, a_all[t][L:, LANES:], 0.0).astype(bf16) for t in P]
    bd_v = [bd(vb[:, sl[t]]) for t in P]
    akv = [mm(a_kk[t], bd_v[t]).astype(bf16) for t in P]
    w_inv = [eye_pair - a_bk[t] for t in P]
    pw = [a_bk[t].astype(bf16) for t in P]
    pw_bd = [bd(pw[t]) for t in P]
    for _ in range(5):
        pw = [mm(pw[t], pw_bd[t]).astype(bf16) for t in P]
        pw_bd = [bd(pw[t]) for t in P]
        w_inv = [w_inv[t] + mm(w_inv[t].astype(bf16), pw_bd[t]) for t in P]
    qu = [mm(w_inv[t].astype(bf16), jnp.concatenate([bd(kq[:, sl[t]]), bd(akv[t])], axis=1))
          for t in P]
    q_m = [qu[t][:, :LANES].astype(bf16) for t in P]
    u_m = [qu[t][:, LANES:].astype(bf16) for t in P]
    m_full = [mm_tn(q_m[t], bh[:, sl[t]]) for t in P]
    neg_m = [jnp.where(same_head, -m_full[t], 0.0).astype(bf16) for t in P]
    c_full = [mm_tn(jnp.concatenate([vb[:, sl[t]], u_m[t]], axis=0),
                    jnp.concatenate([kh[:, sl[t]], -bh[:, sl[t]]], axis=0)) for t in P]
    go = [mm(a_rb[t], jnp.concatenate([bd(q_m[t]), bd(u_m[t])], axis=1)) for t in P]
    o_rk = [mm(a_rk[t], bd_v[t]) for t in P]
    s_old = [s_scr[t] for t in P]
    s_b = [s_old[t].astype(bf16) for t in P]
    g_m = [(rq_f[:, sl[t]] - go[t][:, :LANES]).astype(bf16) for t in P]
    o_tiles = [mm_nt(g_m[t], bd(s_b[t])) + (o_rk[t] - go[t][:, LANES:]) for t in P]
    s_upd = [mm(s_b[t], neg_m[t]) for t in P]
    for t in P:
        c_pair = jnp.where(lo_half, c_full[t][:RW_HEAD], c_full[t][RW_HEAD:])
        s_scr[t] = s_old[t] * g_l[:, sl[t]] + s_upd[t] + c_pair

    o = jnp.concatenate(o_tiles, axis=1)
    (o_sum,) = seg_sums([o])
    d = o - o_sum * (1.0 / RW_HEAD)
    (d2,) = seg_sums([d * d])
    on = d * lax.rsqrt(d2 * (1.0 / RW_HEAD) + GN_EPS) * lng_ref[...] + lnb_ref[...]
    o_ref[...] = ((on + bonus_s * v) * g).astype(bf16)

    @pl.when(c == n_chunks - 1)
    def _():
        sfin_ref[...] = s_scr[...]


def _rwkv(prw, shift0, s0_pair, wts, *, t_valid):
    b, tp, _ = prw.shape
    n_chunks = tp // CHUNK
    kern = functools.partial(_rwkv_kernel, t_valid=t_valid, n_chunks=n_chunks)
    const2 = lambda bi, c: (0, 0)
    row_spec = pl.BlockSpec((1, RW_W), const2)
    return pl.pallas_call(
        kern,
        grid=(b, n_chunks),
        in_specs=[
            pl.BlockSpec((None, CHUNK, RW_PAD), lambda bi, c: (bi, c, 0)),
            pl.BlockSpec((None, 1, RW_PAD), lambda bi, c: (bi, 0, 0)),
            pl.BlockSpec((None, N_PAIRS, RW_HEAD, LANES), lambda bi, c: (bi, 0, 0, 0)),
            pl.BlockSpec((1, RW_PAD), const2),
            row_spec,
            pl.BlockSpec((LORA_PAD, RW_W), const2),
            row_spec,
            pl.BlockSpec((LORA_PAD, RW_W), const2),
            pl.BlockSpec((LORA_PAD, RW_W), const2),
            row_spec, row_spec, row_spec, row_spec, row_spec,
        ],
        out_specs=[
            pl.BlockSpec((None, CHUNK, RW_W), lambda bi, c: (bi, c, 0)),
            pl.BlockSpec((None, N_PAIRS, RW_HEAD, LANES), lambda bi, c: (bi, 0, 0, 0)),
        ],
        out_shape=[
            jax.ShapeDtypeStruct((b, tp, RW_W), bf16),
            jax.ShapeDtypeStruct((b, N_PAIRS, RW_HEAD, LANES), f32),
        ],
        scratch_shapes=[pltpu.VMEM((N_PAIRS, RW_HEAD, LANES), f32), pltpu.VMEM((1, RW_PAD), f32)],
        compiler_params=_cparams(("parallel", "arbitrary")),
    )(prw, shift0, s0_pair, wts["mu"], wts["w0"], wts["wd"], wts["a0"], wts["wa"], wts["wg"],
      wts["k_k"], wts["k_a"], wts["r_k"], wts["lnx_g"], wts["lnx_b"])


def _layer_norm(z, g, b):
    mu = jnp.mean(z, axis=-1, keepdims=True)
    d = z - mu
    var = jnp.mean(d * d, axis=-1, keepdims=True)
    return d * lax.rsqrt(var + LN_EPS) * g + b


N_SHARED_OUT = 4


def _outproj_kernel(*refs, aliased, n_steps):
    att_ref, rw_ref, x_ref, wo_ref, g_ref, b_ref, wrh_ref, wrl_ref, br_ref, cnt0_ref = refs[:10]
    h_ref, idx_ref, gate_ref, rank_ref, cnt_ref, carry = refs[10 + N_SHARED_OUT * aliased:]
    step = pl.program_id(0)

    @pl.when(step == 0)
    def _():
        carry[...] = cnt0_ref[...]

    mix = (jnp.dot(att_ref[...], wo_ref[0:ATT_W], preferred_element_type=f32)
           + jnp.dot(rw_ref[...], wo_ref[ATT_W:ATT_W + RW_W], preferred_element_type=f32))
    h = _layer_norm(DN_ALPHA * x_ref[...] + mix, g_ref[...], b_ref[...])
    h_ref[...] = h
    hh = h.astype(bf16)
    hl = (h - hh.astype(f32)).astype(bf16)
    logits = (jnp.dot(hh, wrh_ref[...], preferred_element_type=f32)
              + jnp.dot(hl, wrh_ref[...], preferred_element_type=f32)
              + jnp.dot(hh, wrl_ref[...], preferred_element_type=f32)) + br_ref[...]
    lane = lax.broadcasted_iota(jnp.int32, logits.shape, 1).astype(f32)
    vals, idxs = [], []
    cur = logits
    for _ in range(TOP_K):
        m = jnp.max(cur, axis=-1, keepdims=True)
        i = jnp.min(jnp.where(cur == m, lane, float(LANES)), axis=-1, keepdims=True)
        vals.append(m)
        idxs.append(i)
        cur = jnp.where(lane == i, -jnp.inf, cur)
    es = [jnp.exp(vv - vals[0]) for vv in vals]
    tot = es[0] + es[1] + es[2] + es[3]
    idx_ref[...] = jnp.concatenate(idxs, axis=-1).astype(jnp.int32)
    gate_ref[...] = jnp.concatenate([e / tot for e in es], axis=-1)
    tm = logits.shape[0]
    hits = [jnp.where(lane == i, 1.0, 0.0) for i in idxs]
    multi = hits[0] + hits[1] + hits[2] + hits[3]
    ti = lax.broadcasted_iota(jnp.int32, (tm, tm), 0)
    si = lax.broadcasted_iota(jnp.int32, (tm, tm), 1)
    before = jnp.where(si < ti, 1.0, 0.0).astype(bf16)
    base = carry[...] + jnp.dot(before, multi.astype(bf16), preferred_element_type=f32)
    ranks = [jnp.sum(hh_ * base, axis=-1, keepdims=True) for hh_ in hits]
    rank_ref[...] = jnp.concatenate(ranks, axis=-1).astype(jnp.int32)
    carry[...] = carry[...] + jnp.sum(multi, axis=0, keepdims=True)

    @pl.when(step == n_steps - 1)
    def _():
        cnt_ref[...] = carry[...]


def _outproj(att2, rw2, x2, wts, *, tm, row0, total_rows, prev=None, counts0=None):
    rows = x2.shape[0]
    assert rows % tm == 0 and row0 % tm == 0
    blk0 = row0 // tm
    aliased = prev is not None
    if counts0 is None:
        counts0 = jnp.zeros((1, LANES), f32)
    const = lambda i: (0, 0)
    in_specs = [
        pl.BlockSpec((tm, ATT_W), lambda i: (i, 0)),
        pl.BlockSpec((tm, RW_W), lambda i: (i, 0)),
        pl.BlockSpec((tm, D_MODEL), lambda i: (i, 0)),
        pl.BlockSpec((D_MODEL, D_MODEL), const, pipeline_mode=pl.Buffered(1)),
        pl.BlockSpec((1, D_MODEL), const),
        pl.BlockSpec((1, D_MODEL), const),
        pl.BlockSpec((D_MODEL, LANES), const),
        pl.BlockSpec((D_MODEL, LANES), const),
        pl.BlockSpec((1, LANES), const),
        pl.BlockSpec((1, LANES), const),
    ]
    args = [att2, rw2, x2, wts["w_out"], wts["ln1_g"], wts["ln1_b"], wts["wr_hi"], wts["wr_lo"],
            wts["b_router"], counts0]
    aliases = {}
    if aliased:
        in_specs += [pl.BlockSpec(memory_space=pl.ANY)] * N_SHARED_OUT
        args += list(prev)
        aliases = {len(args) - N_SHARED_OUT + k: k for k in range(N_SHARED_OUT)}
    n_steps = rows // tm
    return pl.pallas_call(
        functools.partial(_outproj_kernel, aliased=int(aliased), n_steps=n_steps),
        grid=(n_steps,),
        in_specs=in_specs,
        out_specs=[
            pl.BlockSpec((tm, D_MODEL), lambda i: (blk0 + i, 0)),
            pl.BlockSpec((tm, TOP_K), lambda i: (blk0 + i, 0)),
            pl.BlockSpec((tm, TOP_K), lambda i: (blk0 + i, 0)),
            pl.BlockSpec((tm, TOP_K), lambda i: (blk0 + i, 0)),
            pl.BlockSpec((1, LANES), const),
        ],
        out_shape=[
            jax.ShapeDtypeStruct((total_rows, D_MODEL), f32),
            jax.ShapeDtypeStruct((total_rows, TOP_K), jnp.int32),
            jax.ShapeDtypeStruct((total_rows, TOP_K), f32),
            jax.ShapeDtypeStruct((total_rows, TOP_K), jnp.int32),
            jax.ShapeDtypeStruct((1, LANES), f32),
        ],
        scratch_shapes=[pltpu.VMEM((1, LANES), f32)],
        input_output_aliases=aliases,
        compiler_params=_cparams(("arbitrary",)),
    )(*args)


MOE_SUB = 256
MOE_NSUB = 4
MOE_SUPER = MOE_SUB * MOE_NSUB
MOE_FF_TILE = 256
MOE_DOWN_N = 512
MOE_MAX_PAD = N_EXPERTS * (MOE_SUB - 1)
X_SUBL = D_MODEL // (2 * LANES)
O_SUBL = D_MODEL // LANES


def _routing(top_idx, rank, counts_f):
    n = top_idx.shape[0]
    n_assign = n * TOP_K
    counts = counts_f[0, :N_EXPERTS].astype(jnp.int32)
    padded = (counts + MOE_SUPER - 1) // MOE_SUPER * MOE_SUPER
    pad_end = jnp.cumsum(padded)
    start = pad_end - padded
    dest = (start[top_idx.reshape(-1)] + rank.reshape(-1)).astype(jnp.int32)
    n_super = (n_assign + N_EXPERTS * (MOE_SUPER - 1) + MOE_SUPER - 1) // MOE_SUPER
    s_row0 = jnp.arange(n_super, dtype=jnp.int32) * MOE_SUPER
    super_e = jnp.minimum(jnp.searchsorted(pad_end, s_row0, side="right"),
                          N_EXPERTS - 1).astype(jnp.int32)
    rows_here = jnp.clip(counts[super_e] - (s_row0 - start[super_e]), 0, MOE_SUPER)
    rows_here = jnp.where(s_row0 < pad_end[-1], rows_here, 0)
    n_sub = ((rows_here + MOE_SUB - 1) // MOE_SUB).astype(jnp.int32)
    n_used = (pad_end[-1] // MOE_SUPER).astype(jnp.int32).reshape(1)
    n_pad_e = (counts + MOE_SUB - 1) // MOE_SUB * MOE_SUB - counts
    pad_cum = jnp.cumsum(n_pad_e)
    kk = jnp.arange(MOE_MAX_PAD, dtype=jnp.int32)
    pe = jnp.minimum(jnp.searchsorted(pad_cum, kk, side="right"), N_EXPERTS - 1)
    pad_dest = (start[pe] + counts[pe] + (kk - (pad_cum[pe] - n_pad_e[pe]))).astype(jnp.int32)
    pad_dest = jnp.where(kk < pad_cum[-1], pad_dest, 0)
    n_pad = pad_cum[-1].astype(jnp.int32).reshape(1)
    return dest.reshape(n, TOP_K), super_e, n_sub, n_used, pad_dest, n_pad, n_super


def _tile_copy(src_ref, src_tok, dst_ref, dst_tok, subl, sem):
    s0 = pl.multiple_of(src_tok * subl, subl)
    d0 = pl.multiple_of(dst_tok * subl, subl)
    return pltpu.make_async_copy(src_ref.at[pl.ds(s0, subl)], dst_ref.at[pl.ds(d0, subl)], sem)


def _scatter_kernel(npad_ref, dest_ref, pad_ref, h_ref, xs_hbm, stage, zero, sems, pad_sem,
                    *, tm, n_steps):
    i = pl.program_id(0)
    slot = i % 2

    def drain(sl):
        for _ in range(TOP_K):
            pltpu.make_async_copy(stage.at[sl], stage.at[sl], sems.at[sl]).wait()

    @pl.when(i >= 2)
    def _():
        drain(slot)

    h = h_ref[...]
    half = D_MODEL // 2
    hi = lax.bitcast_convert_type(h[:, :half].astype(bf16).astype(f32), jnp.uint32)
    lo = lax.bitcast_convert_type(h[:, half:].astype(bf16).astype(f32), jnp.uint32)
    packed = hi | (lo >> 16)
    for l in range(X_SUBL):
        stage[slot, pl.ds(l, tm, stride=X_SUBL), :] = packed[:, LANES * l:LANES * (l + 1)]

    def issue(t, carry):
        for jx in range(TOP_K):
            _tile_copy(stage.at[slot], t, xs_hbm, dest_ref[0, 0, jx * tm + t], X_SUBL,
                       sems.at[slot]).start()
        return carry

    lax.fori_loop(0, tm, issue, 0)

    @pl.when(i == 0)
    def _():
        zero[...] = jnp.zeros_like(zero)

        def fill(k, carry):
            _tile_copy(zero, 0, xs_hbm, pad_ref[k], X_SUBL, pad_sem).start()
            return carry

        lax.fori_loop(0, npad_ref[0], fill, 0)

        def fill_wait(k, carry):
            _tile_copy(zero, 0, xs_hbm, 0, X_SUBL, pad_sem).wait()
            return carry

        lax.fori_loop(0, npad_ref[0], fill_wait, 0)

    @pl.when(i == n_steps - 1)
    def _():
        drain(slot)
        if n_steps > 1:
            drain(1 - slot)


def _scatter_rows(h_all, dest, pad_dest, n_pad, n_rows, tm):
    n = h_all.shape[0]
    assert n % tm == 0
    n_steps = n // tm
    dest_blk = dest.reshape(n_steps, tm, TOP_K).transpose(0, 2, 1).reshape(n_steps, 1, TOP_K * tm)
    kern = functools.partial(_scatter_kernel, tm=tm, n_steps=n_steps)
    return pl.pallas_call(
        kern,
        grid_spec=pltpu.PrefetchScalarGridSpec(
            num_scalar_prefetch=1,
            grid=(n_steps,),
            in_specs=[
                pl.BlockSpec((1, 1, TOP_K * tm), lambda i, npad: (i, 0, 0), memory_space=pltpu.SMEM),
                pl.BlockSpec(memory_space=pltpu.SMEM),
                pl.BlockSpec((tm, D_MODEL), lambda i, npad: (i, 0)),
            ],
            out_specs=pl.BlockSpec(memory_space=pl.ANY),
            scratch_shapes=[
                pltpu.VMEM((2, tm * X_SUBL, LANES), jnp.uint32),
                pltpu.VMEM((X_SUBL, LANES), jnp.uint32),
                pltpu.SemaphoreType.DMA((2,)),
                pltpu.SemaphoreType.DMA(()),
            ],
        ),
        out_shape=jax.ShapeDtypeStruct((n_rows * X_SUBL, LANES), jnp.uint32),
        compiler_params=_cparams(("arbitrary",)),
    )(n_pad, dest_blk, pad_dest, h_all)


def _expert_kernel(se_ref, nsub_ref, nused_ref, x_ref, wg_ref, wl_ref, bg_ref, bl_ref, wd_ref,
                   bd_ref, o_ref, xb_scr, acc_scr, *, nf):
    s = pl.program_id(0)
    j = pl.program_id(1)
    n_sub = nsub_ref[s]
    half = D_MODEL // 2

    def step(m):
        rows = slice(0, m)

        @pl.when(j == 0)
        def _():
            for l in range(X_SUBL):
                u = x_ref[pl.ds(l, m, stride=X_SUBL), :]
                hi = lax.bitcast_convert_type(u & jnp.uint32(0xFFFF0000), f32)
                lo = lax.bitcast_convert_type(u << 16, f32)
                xb_scr[rows, LANES * l:LANES * (l + 1)] = hi.astype(bf16)
                xb_scr[rows, half + LANES * l:half + LANES * (l + 1)] = lo.astype(bf16)
            acc_scr[rows, :] = jnp.broadcast_to(bd_ref[...], (m, D_MODEL))

        xb = xb_scr[rows, :]
        hg = jnp.dot(xb, wg_ref[...].astype(bf16), preferred_element_type=f32) + bg_ref[...]
        hl = jnp.dot(xb, wl_ref[...].astype(bf16), preferred_element_type=f32) + bl_ref[...]
        glu = jnp.minimum(hg, SWIGLU_LIMIT)
        lin = jnp.clip(hl, -SWIGLU_LIMIT, SWIGLU_LIMIT)
        act = (glu * (1.0 / (1.0 + jnp.exp(-SWIGLU_ALPHA * glu))) * (lin + 1.0)).astype(bf16)
        wd = wd_ref[...].astype(bf16)
        for n0 in range(0, D_MODEL, MOE_DOWN_N):
            acc_scr[rows, n0:n0 + MOE_DOWN_N] += jnp.dot(
                act, wd[:, n0:n0 + MOE_DOWN_N], preferred_element_type=f32)

        @pl.when(j == nf - 1)
        def _():
            for l in range(O_SUBL):
                o_ref[pl.ds(l, m, stride=O_SUBL), :] = acc_scr[rows, LANES * l:LANES * (l + 1)]

    for k in range(1, MOE_NSUB + 1):
        pl.when(n_sub == k)(functools.partial(step, k * MOE_SUB))


def _experts(xs, super_e, n_sub, n_used, w_up, b_up, w_down, b_down, n_super):
    tf = MOE_FF_TILE
    nf = D_FF // tf
    last = lambda s, nu: jnp.minimum(s, nu[0] - 1)
    b_up3 = b_up.reshape(N_EXPERTS, 1, 2 * D_FF)
    b_down3 = b_down.reshape(N_EXPERTS, 1, D_MODEL)
    e_of = lambda s, se, nu: se[last(s, nu)]
    return pl.pallas_call(
        functools.partial(_expert_kernel, nf=nf),
        grid_spec=pltpu.PrefetchScalarGridSpec(
            num_scalar_prefetch=3,
            grid=(n_super, nf),
            in_specs=[
                pl.BlockSpec((MOE_SUPER * X_SUBL, LANES), lambda s, j, se, ns, nu: (last(s, nu), 0)),
                pl.BlockSpec((None, D_MODEL, tf), lambda s, j, se, ns, nu: (e_of(s, se, nu), 0, j)),
                pl.BlockSpec((None, D_MODEL, tf),
                             lambda s, j, se, ns, nu: (e_of(s, se, nu), 0, nf + j)),
                pl.BlockSpec((None, 1, tf), lambda s, j, se, ns, nu: (e_of(s, se, nu), 0, j)),
                pl.BlockSpec((None, 1, tf), lambda s, j, se, ns, nu: (e_of(s, se, nu), 0, nf + j)),
                pl.BlockSpec((None, tf, D_MODEL), lambda s, j, se, ns, nu: (e_of(s, se, nu), j, 0)),
                pl.BlockSpec((None, 1, D_MODEL), lambda s, j, se, ns, nu: (e_of(s, se, nu), 0, 0)),
            ],
            out_specs=pl.BlockSpec((MOE_SUPER * O_SUBL, LANES),
                                   lambda s, j, se, ns, nu: (last(s, nu), 0)),
            scratch_shapes=[pltpu.VMEM((MOE_SUPER, D_MODEL), bf16),
                            pltpu.VMEM((MOE_SUPER, D_MODEL), f32)],
        ),
        out_shape=jax.ShapeDtypeStruct((n_super * MOE_SUPER * O_SUBL, LANES), f32),
        compiler_params=_cparams(("arbitrary", "arbitrary")),
    )(super_e, n_sub, n_used, xs, w_up, w_up, b_up3, b_up3, w_down, b_down3)


def _combine_kernel(dest_ref, nxt_ref, gate_ref, h_ref, rows_hbm, g_ref, b_ref, yp_ref, ys_ref,
                    buf, sems, *, tm, n_first, n_steps):
    i = pl.program_id(0)
    slot = i % 2

    def fetch(idx_ref, sl):
        def body(t, carry):
            for jx in range(TOP_K):
                _tile_copy(rows_hbm, idx_ref[0, 0, jx * tm + t], buf.at[sl, jx], t, O_SUBL,
                           sems.at[sl]).start()
            return carry
        lax.fori_loop(0, tm, body, 0)

    @pl.when(i == 0)
    def _():
        fetch(dest_ref, slot)

    @pl.when(i + 1 < n_steps)
    def _():
        fetch(nxt_ref, 1 - slot)

    for jx in range(TOP_K):
        pltpu.make_async_copy(buf.at[slot, jx], buf.at[slot, jx], sems.at[slot]).wait()
    gate = gate_ref[...]
    cols = []
    for l in range(O_SUBL):
        acc = gate[:, 0:1] * buf[slot, 0, pl.ds(l, tm, stride=O_SUBL), :]
        for jx in range(1, TOP_K):
            acc = acc + gate[:, jx:jx + 1] * buf[slot, jx, pl.ds(l, tm, stride=O_SUBL), :]
        cols.append(acc)
    y = jnp.concatenate(cols, axis=-1)
    out = _layer_norm(DN_ALPHA * h_ref[...] + y, g_ref[...], b_ref[...])

    @pl.when(i < n_first)
    def _():
        yp_ref[...] = out

    @pl.when(i >= n_first)
    def _():
        ys_ref[...] = out


def _combine(rows_out, dest, gate, h_all, ln_g, ln_b, *, tm, n_first_rows):
    n = h_all.shape[0]
    assert n % tm == 0 and n_first_rows % tm == 0
    nblk = n // tm
    n_first = n_first_rows // tm
    dest_blk = dest.reshape(nblk, tm, TOP_K).transpose(0, 2, 1).reshape(nblk, 1, TOP_K * tm)
    kern = functools.partial(_combine_kernel, tm=tm, n_first=n_first, n_steps=nblk)
    const = lambda i: (0, 0)
    return pl.pallas_call(
        kern,
        grid=(nblk,),
        in_specs=[
            pl.BlockSpec((1, 1, TOP_K * tm), lambda i: (i, 0, 0), memory_space=pltpu.SMEM),
            pl.BlockSpec((1, 1, TOP_K * tm), lambda i: (jnp.minimum(i + 1, nblk - 1), 0, 0),
                         memory_space=pltpu.SMEM),
            pl.BlockSpec((tm, TOP_K), lambda i: (i, 0)),
            pl.BlockSpec((tm, D_MODEL), lambda i: (i, 0)),
            pl.BlockSpec(memory_space=pl.ANY),
            pl.BlockSpec((1, D_MODEL), const),
            pl.BlockSpec((1, D_MODEL), const),
        ],
        out_specs=[
            pl.BlockSpec((tm, D_MODEL), lambda i: (jnp.minimum(i, n_first - 1), 0)),
            pl.BlockSpec((tm, D_MODEL), lambda i: (jnp.maximum(i - n_first, 0), 0)),
        ],
        out_shape=[
            jax.ShapeDtypeStruct((n_first_rows, D_MODEL), f32),
            jax.ShapeDtypeStruct((n - n_first_rows, D_MODEL), f32),
        ],
        scratch_shapes=[pltpu.VMEM((2, TOP_K, tm * O_SUBL, LANES), f32),
                        pltpu.SemaphoreType.DMA((2,))],
        compiler_params=_cparams(("arbitrary",)),
    )(dest_blk, dest_blk, gate, h_all, rows_out, ln_g, ln_b)


def _t5_bucket(rel):
    half = NUM_BUCKETS // 2
    exact = half // 2
    n = jnp.abs(rel)
    log_part = exact + (jnp.log(jnp.maximum(n, 1).astype(jnp.float32) / exact)
                        / math.log(MAX_DISTANCE / exact) * (half - exact)).astype(jnp.int32)
    log_part = jnp.minimum(log_part, half - 1)
    return jnp.where(rel > 0, half, 0) + jnp.where(n < exact, n, log_part)


def _band_bias(rel_bias):
    qi = jnp.arange(CHUNK)[:, None]
    km = jnp.arange(BAND)[None, :]
    bucket = _t5_bucket(km - WINDOW - qi)
    return jnp.transpose(rel_bias[bucket], (2, 0, 1)).astype(jnp.float32)


def _pad_cols(a, width):
    return jnp.pad(a, ((0, 0), (0, width - a.shape[-1])))


def _pair_state(s):
    b = s.shape[0]
    return s.reshape(b, N_PAIRS, 2, RW_HEAD, RW_HEAD).transpose(0, 1, 3, 2, 4).reshape(
        b, N_PAIRS, RW_HEAD, LANES)


def _unpair_state(s):
    b = s.shape[0]
    return s.reshape(b, N_PAIRS, RW_HEAD, 2, RW_HEAD).transpose(0, 1, 3, 2, 4).reshape(
        b, RW_HEADS, RW_HEAD, RW_HEAD)


def _mix_group(x, k_hist, v_hist, hist_valid, wkv0, shift0, bias, sinks3, w_in_pad, rw_wts,
               *, in_tm, attn_nc):
    b, t, _ = x.shape
    q, kv, prw = _inproj(x.reshape(b * t, D_MODEL), w_in_pad, in_tm)
    tp = -(-t // (attn_nc * CHUNK)) * (attn_nc * CHUNK)
    q4 = q.reshape(ATT_HEADS, b, t, HEAD_DIM)
    kv3 = kv.reshape(b, t, 2 * KV_W)
    prw3 = prw.reshape(b, t, RW_PAD)
    if tp != t:
        q4 = jnp.pad(q4, ((0, 0), (0, 0), (0, tp - t), (0, 0)))
        prw3 = jnp.pad(prw3, ((0, 0), (0, tp - t), (0, 0)))
    hist = jnp.concatenate([k_hist.reshape(b, WINDOW, KV_W), v_hist.reshape(b, WINDOW, KV_W)], axis=-1)
    kvfull = jnp.concatenate([hist, kv3, jnp.zeros((b, tp - t, 2 * KV_W), f32)], axis=1)
    att = _attention(q4, kvfull, bias, sinks3, nc=attn_nc, t_valid=t, hist_valid=hist_valid)
    rw, s_fin = _rwkv(prw3, _pad_cols(shift0.reshape(b, RW_PROJ), RW_PAD).reshape(b, 1, RW_PAD),
                      _pair_state(wkv0.astype(f32)), rw_wts, t_valid=t)
    new_kv = kvfull[:, t:t + WINDOW]
    new_k = new_kv[..., :KV_W].reshape(b, WINDOW, ATT_KV_HEADS, HEAD_DIM)
    new_v = new_kv[..., KV_W:].reshape(b, WINDOW, ATT_KV_HEADS, HEAD_DIM)
    shift = prw3[:, t - 1:t, :RW_PROJ]
    return (att[:, :t].reshape(b * t, ATT_W), rw[:, :t].reshape(b * t, RW_W),
            new_k, new_v, _unpair_state(s_fin), shift)


SCATTER_TM = 128
COMBINE_TM = 128


def kernel(x_prompt, x_sample, cache_k, cache_v, state_wkv, state_shift, rel_bias, w_in, attn_sinks, rw_mu, rw_w0, rw_decay_up, rw_a0, rw_iclr_up, rw_gate_up, rw_k_k, rw_k_a, rw_r_k, rw_lnx_g, rw_lnx_b, w_out, ln1_g, ln1_b, w_router, b_router, w_up, b_up, w_down, b_down, ln2_g, ln2_b):
    assert w_in.shape[0] == DEPTH == 1
    l = 0
    bp, tp_, _ = x_prompt.shape
    bs, ts, _ = x_sample.shape
    bias = _band_bias(rel_bias)
    sinks3 = attn_sinks[l].astype(f32).reshape(ATT_HEADS, 1, 1)

    w_in_pad = _pad_cols(w_in[l], IN_PAD).astype(bf16)

    def lora_rows(w, row0):
        return jnp.zeros((LORA_PAD, RW_W), f32).at[row0:row0 + w.shape[0]].set(w).astype(bf16)

    rw_wts = {
        "mu": _pad_cols(rw_mu[l].reshape(1, RW_PROJ), RW_PAD),
        "w0": rw_w0[l].reshape(1, RW_W),
        "wd": lora_rows(rw_decay_up[l], 0),
        "a0": rw_a0[l].reshape(1, RW_W),
        "wa": lora_rows(rw_iclr_up[l], DECAY_LORA),
        "wg": lora_rows(rw_gate_up[l], DECAY_LORA + ICLR_LORA),
        "k_k": rw_k_k[l].reshape(1, RW_W),
        "k_a": rw_k_a[l].reshape(1, RW_W),
        "r_k": rw_r_k[l].reshape(1, RW_W),
        "lnx_g": rw_lnx_g[l].reshape(1, RW_W),
        "lnx_b": rw_lnx_b[l].reshape(1, RW_W),
    }
    wr = _pad_cols(w_router[l], LANES)
    wr_hi = wr.astype(bf16)
    op_wts = {
        "w_out": w_out[l].astype(bf16),
        "ln1_g": ln1_g[l].reshape(1, D_MODEL),
        "ln1_b": ln1_b[l].reshape(1, D_MODEL),
        "wr_hi": wr_hi,
        "wr_lo": (wr - wr_hi.astype(f32)).astype(bf16),
        "b_router": jnp.concatenate([b_router[l].astype(f32),
                                     jnp.full((LANES - N_EXPERTS,), NEG_BIG, f32)]).reshape(1, LANES),
    }

    zero_kv = jnp.zeros((bp, WINDOW, ATT_KV_HEADS, HEAD_DIM), f32)
    att_p, rwo_p, k1, v1, w1, s1 = _mix_group(
        x_prompt, zero_kv, zero_kv, False, jnp.zeros((bp, RW_HEADS, RW_HEAD, RW_HEAD), f32),
        jnp.zeros((bp, 1, RW_PROJ), f32), bias, sinks3, w_in_pad, rw_wts,
        in_tm=min(256, bp * tp_), attn_nc=min(8, -(-tp_ // CHUNK)))
    att_s, rwo_s, k2, v2, w2, s2 = _mix_group(
        x_sample, cache_k[l], cache_v[l], True, state_wkv[l], state_shift[l], bias, sinks3,
        w_in_pad, rw_wts, in_tm=min(256, bs * ts), attn_nc=1)

    n_p, n_s = bp * tp_, bs * ts
    n_all = n_p + n_s
    tm_p, tm_s = min(256, n_p), min(128, n_s)
    *outs, counts_p = _outproj(att_p, rwo_p, x_prompt.reshape(n_p, D_MODEL), op_wts,
                               tm=tm_p, row0=0, total_rows=n_all)
    h_all, top_idx, gate, rank, counts = _outproj(
        att_s, rwo_s, x_sample.reshape(n_s, D_MODEL), op_wts,
        tm=tm_s, row0=n_p, total_rows=n_all, prev=outs, counts0=counts_p)

    dest, super_e, n_sub, n_used, pad_dest, n_pad, n_super = _routing(top_idx, rank, counts)
    xs = _scatter_rows(h_all, dest, pad_dest, n_pad, n_super * MOE_SUPER, min(SCATTER_TM, n_s))
    rows_out = _experts(xs, super_e, n_sub, n_used, w_up[l], b_up[l], w_down[l], b_down[l],
                        n_super)
    y_p, y_s = _combine(rows_out, dest, gate, h_all, ln2_g[l].reshape(1, D_MODEL),
                        ln2_b[l].reshape(1, D_MODEL), tm=min(COMBINE_TM, n_s), n_first_rows=n_p)

    return (y_p.reshape(bp, tp_, D_MODEL), y_s.reshape(bs, ts, D_MODEL),
            k1[None], v1[None], w1[None], s1[None], k2[None], v2[None], w2[None], s2[None])
```

```python
import functools
import math

import jax
import jax.numpy as jnp
from jax import lax
from jax.experimental import pallas as pl
from jax.experimental.pallas import tpu as pltpu

f32 = jnp.float32
bf16 = jnp.bfloat16

D_MODEL = 2048
CHUNK = 64
ATT_HEADS = 16
ATT_KV_HEADS = 2
HEAD_DIM = 64
ATT_GROUP = ATT_HEADS // ATT_KV_HEADS
ATT_W = ATT_HEADS * HEAD_DIM
KV_W = ATT_KV_HEADS * HEAD_DIM
ATT_PROJ = ATT_W + 2 * KV_W
WINDOW = 128
BAND = WINDOW + CHUNK
NUM_BUCKETS = 32
MAX_DISTANCE = 128
RW_HEAD = 64
RW_W = 1024
RW_HEADS = RW_W // RW_HEAD
DECAY_LORA = 96
ICLR_LORA = 96
GATE_LORA = 128
RW_PROJ = 3 * RW_W + DECAY_LORA + ICLR_LORA + GATE_LORA
GN_EPS = 64e-5
LN_EPS = 1e-5
N_EXPERTS = 32
TOP_K = 4
D_FF = D_MODEL
SWIGLU_LIMIT = 7.0
SWIGLU_ALPHA = 1.702
DEPTH = 1
DN_ALPHA = (2 * DEPTH) ** 0.25

LANES = 128
VMEM_LIMIT = 56 * 1024 * 1024

LORA_W = DECAY_LORA + ICLR_LORA + GATE_LORA
LORA_PAD = -(-LORA_W // LANES) * LANES
RW_PAD = 3 * RW_W + LORA_PAD
IN_PAD = ATT_PROJ + RW_PAD
N_PAIRS = RW_HEADS // 2
NEG_BIG = -1e30


def _cparams(sem):
    return pltpu.CompilerParams(dimension_semantics=sem, vmem_limit_bytes=VMEM_LIMIT)


def _inproj_kernel(x_ref, w_ref, q_ref, kv_ref, rw_ref):
    acc = jnp.dot(x_ref[...].astype(bf16), w_ref[...], preferred_element_type=f32)
    scale = HEAD_DIM ** -0.5
    for h in range(ATT_HEADS):
        q_ref[h] = (acc[:, h * HEAD_DIM:(h + 1) * HEAD_DIM] * scale).astype(bf16)
    kv_ref[...] = acc[:, ATT_W:ATT_PROJ]
    rw_ref[...] = acc[:, ATT_PROJ:IN_PAD]


def _inproj(x2, w_pad, tm):
    rows = x2.shape[0]
    assert rows % tm == 0
    return pl.pallas_call(
        _inproj_kernel,
        grid=(rows // tm,),
        in_specs=[
            pl.BlockSpec((tm, D_MODEL), lambda i: (i, 0)),
            pl.BlockSpec((D_MODEL, IN_PAD), lambda i: (0, 0), pipeline_mode=pl.Buffered(1)),
        ],
        out_specs=[
            pl.BlockSpec((ATT_HEADS, tm, HEAD_DIM), lambda i: (0, i, 0)),
            pl.BlockSpec((tm, 2 * KV_W), lambda i: (i, 0)),
            pl.BlockSpec((tm, RW_PAD), lambda i: (i, 0)),
        ],
        out_shape=[
            jax.ShapeDtypeStruct((ATT_HEADS, rows, HEAD_DIM), bf16),
            jax.ShapeDtypeStruct((rows, 2 * KV_W), f32),
            jax.ShapeDtypeStruct((rows, RW_PAD), f32),
        ],
        compiler_params=_cparams(("parallel",)),
    )(x2, w_pad)


ATT_UNROLL = 4


def _attn_kernel(q_ref, kvm_ref, kva_ref, kvb_ref, bias_ref, sink_ref, o_ref, kvbuf,
                 *, nc, t_valid, hist_valid):
    j = pl.program_id(1)
    kvbuf[0:nc * CHUNK] = kvm_ref[...].astype(bf16)
    kvbuf[nc * CHUNK:(nc + 1) * CHUNK] = kva_ref[...].astype(bf16)
    kvbuf[(nc + 1) * CHUNK:(nc + 2) * CHUNK] = kvb_ref[...].astype(bf16)
    m_idx = lax.broadcasted_iota(jnp.int32, (1, 1, BAND), 2)
    for c0 in range(0, nc, ATT_UNROLL):
        items = [(c, g) for c in range(c0, min(c0 + ATT_UNROLL, nc)) for g in range(ATT_KV_HEADS)]
        bands = {c: kvbuf[c * CHUNK:c * CHUNK + BAND, :] for c, _ in items}
        valid = {}
        for c in bands:
            idx = (j * nc + c) * CHUNK + m_idx
            v = idx - WINDOW < t_valid
            valid[c] = v if hist_valid else jnp.logical_and(v, idx >= WINDOW)
        s = [lax.dot_general(
                q_ref[g * ATT_GROUP:(g + 1) * ATT_GROUP, c * CHUNK:(c + 1) * CHUNK, :].reshape(
                    ATT_GROUP * CHUNK, HEAD_DIM),
                bands[c][:, g * HEAD_DIM:(g + 1) * HEAD_DIM],
                (((1,), (1,)), ((), ())), preferred_element_type=f32) for c, g in items]
        s = [jnp.where(valid[c], s[i].reshape(ATT_GROUP, CHUNK, BAND)
                       + bias_ref[g * ATT_GROUP:(g + 1) * ATT_GROUP], NEG_BIG)
             for i, (c, g) in enumerate(items)]
        sk = [sink_ref[g * ATT_GROUP:(g + 1) * ATT_GROUP] for _, g in items]
        m = [jnp.maximum(jnp.max(s[i], axis=-1, keepdims=True), sk[i]) for i in range(len(items))]
        p = [jnp.exp(s[i] - m[i]) for i in range(len(items))]
        den = [jnp.sum(p[i], axis=-1, keepdims=True) + jnp.exp(sk[i] - m[i])
               for i in range(len(items))]
        o = [jnp.dot(p[i].reshape(ATT_GROUP * CHUNK, BAND).astype(bf16),
                     bands[c][:, KV_W + g * HEAD_DIM:KV_W + (g + 1) * HEAD_DIM],
                     preferred_element_type=f32).reshape(ATT_GROUP, CHUNK, HEAD_DIM) / den[i]
             for i, (c, g) in enumerate(items)]
        for c in bands:
            heads = [o[i][h] for i, (ci, _) in enumerate(items) if ci == c for h in range(ATT_GROUP)]
            o_ref[c * CHUNK:(c + 1) * CHUNK, :] = jnp.concatenate(heads, axis=-1).astype(bf16)


def _attention(q4, kvfull, bias, sinks3, *, nc, t_valid, hist_valid):
    _, b, tp, _ = q4.shape
    assert tp % (nc * CHUNK) == 0 and kvfull.shape[1] == tp + WINDOW
    nblk = tp // (nc * CHUNK)
    kern = functools.partial(_attn_kernel, nc=nc, t_valid=t_valid, hist_valid=hist_valid)
    return pl.pallas_call(
        kern,
        grid=(b, nblk),
        in_specs=[
            pl.BlockSpec((ATT_HEADS, None, nc * CHUNK, HEAD_DIM), lambda bi, j: (0, bi, j, 0)),
            pl.BlockSpec((None, nc * CHUNK, 2 * KV_W), lambda bi, j: (bi, j, 0)),
            pl.BlockSpec((None, CHUNK, 2 * KV_W), lambda bi, j: (bi, (j + 1) * nc, 0)),
            pl.BlockSpec((None, CHUNK, 2 * KV_W), lambda bi, j: (bi, (j + 1) * nc + 1, 0)),
            pl.BlockSpec((ATT_HEADS, CHUNK, BAND), lambda bi, j: (0, 0, 0)),
            pl.BlockSpec((ATT_HEADS, 1, 1), lambda bi, j: (0, 0, 0)),
        ],
        out_specs=pl.BlockSpec((None, nc * CHUNK, ATT_W), lambda bi, j: (bi, j, 0)),
        out_shape=jax.ShapeDtypeStruct((b, tp, ATT_W), bf16),
        scratch_shapes=[pltpu.VMEM(((nc + 2) * CHUNK, 2 * KV_W), bf16)],
        compiler_params=_cparams(("parallel", "parallel")),
    )(q4, kvfull, kvfull, kvfull, bias, sinks3)


def _rwkv_kernel(p_ref, shift0_ref, s0_ref, mu_ref, w0_ref, wd_ref, a0_ref, wa_ref, wg_ref,
                 kk_ref, ka_ref, rk_ref, lng_ref, lnb_ref, o_ref, sfin_ref, s_scr, last_scr,
                 *, t_valid, n_steps, nch):
    c = pl.program_id(1)
    L = CHUNK
    R = nch * L

    @pl.when(c == 0)
    def _():
        s_scr[...] = s0_ref[...]
        last_scr[...] = shift0_ref[...]

    p = p_ref[...]
    row = lax.broadcasted_iota(jnp.int32, (R, 1), 0)
    shifted = jnp.where(row == 0, last_scr[...], pltpu.roll(p, 1, axis=0))
    last_scr[...] = p[R - 1:R, :]
    xm = p + (shifted - p) * mu_ref[...]
    r = xm[:, 0:RW_W]
    k = xm[:, RW_W:2 * RW_W]
    v = xm[:, 2 * RW_W:3 * RW_W]
    tail = xm[:, 3 * RW_W:RW_PAD]

    def sigmoid(z):
        return 1.0 / (1.0 + jnp.exp(-z))

    w_log = w0_ref[...] + jnp.dot(jnp.tanh(tail).astype(bf16), wd_ref[...],
                                  preferred_element_type=f32)
    z = -w_log
    softplus = jnp.maximum(z, 0.0) + jnp.log(1.0 + jnp.exp(-jnp.abs(z)))
    ld = -jnp.exp(-softplus - 0.5)
    a = sigmoid(a0_ref[...] + jnp.dot(tail.astype(bf16), wa_ref[...], preferred_element_type=f32))
    g = jnp.dot(sigmoid(tail).astype(bf16), wg_ref[...], preferred_element_type=f32)

    lane = lax.broadcasted_iota(jnp.int32, (1, LANES), 1)
    lo_half = lane < RW_HEAD
    rr = lax.broadcasted_iota(jnp.int32, (LANES, LANES), 0)
    cc = lax.broadcasted_iota(jnp.int32, (LANES, LANES), 1)
    same_head = (rr // RW_HEAD) == (cc // RW_HEAD)
    ones_bd = jnp.where(same_head, 1.0, 0.0).astype(bf16)

    def seg_sums(xs):
        n = len(xs)
        x = jnp.concatenate(xs, axis=0) if n > 1 else xs[0]
        hi = x.astype(bf16)
        lo = (x - hi.astype(f32)).astype(bf16)
        both = jnp.concatenate([hi, lo], axis=0)
        m = 2 * n * R
        tiles = jnp.concatenate([both[:, LANES * t:LANES * (t + 1)] for t in range(N_PAIRS)], axis=0)
        res = jnp.dot(tiles, ones_bd, preferred_element_type=f32)
        y = jnp.concatenate([res[m * t:m * (t + 1)] for t in range(N_PAIRS)], axis=1)
        y = y[:n * R] + y[n * R:]
        return [y[i * R:(i + 1) * R] for i in range(n)]

    kk = k * kk_ref[...]
    k_mod = k * (1.0 + (a - 1.0) * ka_ref[...])
    nrm2, bonus_s = seg_sums([kk * kk, r * k_mod * rk_ref[...]])
    kk = kk / jnp.maximum(jnp.sqrt(nrm2), 1e-12)
    b = kk * a

    if t_valid % R != 0:
        live = (c * R + row) < t_valid
        ld = jnp.where(live, ld, 0.0)
        b = jnp.where(live, b, 0.0)
        k_mod = jnp.where(live, k_mod, 0.0)

    h1 = ld.astype(bf16)
    r1 = ld - h1.astype(f32)
    h2 = r1.astype(bf16)
    h3 = (r1 - h2.astype(f32)).astype(bf16)
    ti = lax.broadcasted_iota(jnp.int32, (L, 3 * L), 0)
    si = lax.broadcasted_iota(jnp.int32, (L, 3 * L), 1) % L
    tri3 = jnp.where(si <= ti, 1.0, 0.0).astype(bf16)
    rs = [slice(L * ch, L * (ch + 1)) for ch in range(nch)]
    cums = [jnp.dot(tri3, jnp.concatenate([h1[rs[ch]], h2[rs[ch]], h3[rs[ch]]], axis=0),
                    preferred_element_type=f32) for ch in range(nch)]
    cum = jnp.concatenate(cums, axis=0) if nch > 1 else cums[0]
    cum_ls = [cums[ch][L - 1:L, :] for ch in range(nch)]
    cum_l = (jnp.concatenate([jnp.broadcast_to(x, (L, RW_W)) for x in cum_ls], axis=0)
             if nch > 1 else cum_ls[0])
    g_l = [jnp.exp(x) for x in cum_ls]
    g_inv = jnp.exp(-cum)
    g_rest = jnp.exp(cum_l - cum)
    kq = (kk * jnp.exp(cum - ld)).astype(bf16)
    rq_f = r * jnp.exp(cum)
    rq = rq_f.astype(bf16)
    bt = (b * g_inv).astype(bf16)
    kt = (k_mod * g_inv).astype(bf16)
    bh = (b * g_rest).astype(bf16)
    kh = (k_mod * g_rest).astype(bf16)
    vb = v.astype(bf16)

    def bd(x):
        zero = jnp.zeros_like(x)
        return jnp.concatenate([jnp.where(lo_half, x, zero), jnp.where(lo_half, zero, x)], axis=0)

    def mm(x, y):
        return jnp.dot(x, y, preferred_element_type=f32)

    def mm_nt(x, y):
        return lax.dot_general(x, y, (((1,), (1,)), ((), ())), preferred_element_type=f32)

    def mm_tn(x, y):
        return lax.dot_general(x, y, (((0,), (0,)), ((), ())), preferred_element_type=f32)

    tt = lax.broadcasted_iota(jnp.int32, (L, LANES), 0)
    ss = lax.broadcasted_iota(jnp.int32, (L, LANES), 1) % RW_HEAD
    strict = ss < tt
    incl = ss <= tt
    eye_pair = jnp.where(ss == tt, 1.0, 0.0).astype(f32)

    items = [(ch, t) for ch in range(nch) for t in range(N_PAIRS)]
    I = range(len(items))
    sl = [slice(LANES * t, LANES * (t + 1)) for t in range(N_PAIRS)]

    def tile(x, i):
        ch, t = items[i]
        return x[rs[ch], sl[t]]

    a_all = [mm_nt(jnp.concatenate([tile(kq, i), tile(rq, i)], axis=0),
                   jnp.concatenate([bd(tile(bt, i)), bd(tile(kt, i))], axis=0)) for i in I]
    a_bk = [jnp.where(strict, a_all[i][:L, :LANES], 0.0) for i in I]
    a_kk = [jnp.where(strict, a_all[i][:L, LANES:], 0.0).astype(bf16) for i in I]
    a_rb = [jnp.where(incl, a_all[i][L:, :LANES], 0.0).astype(bf16) for i in I]
    a_rk = [jnp.where(incl, a_all[i][L:, LANES:], 0.0).astype(bf16) for i in I]
    bd_v = [bd(tile(vb, i)) for i in I]
    akv = [mm(a_kk[i], bd_v[i]).astype(bf16) for i in I]
    w_inv = [eye_pair - a_bk[i] for i in I]
    pw = [a_bk[i].astype(bf16) for i in I]
    pw_bd = [bd(pw[i]) for i in I]
    for _ in range(5):
        pw = [mm(pw[i], pw_bd[i]).astype(bf16) for i in I]
        pw_bd = [bd(pw[i]) for i in I]
        w_inv = [w_inv[i] + mm(w_inv[i].astype(bf16), pw_bd[i]) for i in I]
    qu = [mm(w_inv[i].astype(bf16), jnp.concatenate([bd(tile(kq, i)), bd(akv[i])], axis=1))
          for i in I]
    q_m = [qu[i][:, :LANES].astype(bf16) for i in I]
    u_m = [qu[i][:, LANES:].astype(bf16) for i in I]
    m_full = [mm_tn(q_m[i], tile(bh, i)) for i in I]
    neg_m = [jnp.where(same_head, -m_full[i], 0.0).astype(bf16) for i in I]
    c_full = [mm_tn(jnp.concatenate([tile(vb, i), u_m[i]], axis=0),
                    jnp.concatenate([tile(kh, i), -tile(bh, i)], axis=0)) for i in I]
    go = [mm(a_rb[i], jnp.concatenate([bd(q_m[i]), bd(u_m[i])], axis=1)) for i in I]
    o_rk = [mm(a_rk[i], bd_v[i]) for i in I]
    g_m = [(tile(rq_f, i) - go[i][:, :LANES]).astype(bf16) for i in I]
    o_in = [o_rk[i] - go[i][:, LANES:] for i in I]
    c_pair = [jnp.where(lo_half, c_full[i][:RW_HEAD], c_full[i][RW_HEAD:]) for i in I]
    s_cur = [s_scr[t] for t in range(N_PAIRS)]
    o_rows = []
    for ch in range(nch):
        ii = [ch * N_PAIRS + t for t in range(N_PAIRS)]
        s_b = [s.astype(bf16) for s in s_cur]
        o_rows.append(jnp.concatenate(
            [mm_nt(g_m[i], bd(s_b[t])) + o_in[i] for t, i in enumerate(ii)], axis=1))
        s_upd = [mm(s_b[t], neg_m[i]) for t, i in enumerate(ii)]
        s_cur = [s_cur[t] * g_l[ch][:, sl[t]] + s_upd[t] + c_pair[i] for t, i in enumerate(ii)]
    for t in range(N_PAIRS):
        s_scr[t] = s_cur[t]

    o = jnp.concatenate(o_rows, axis=0) if nch > 1 else o_rows[0]
    (o_sum,) = seg_sums([o])
    d = o - o_sum * (1.0 / RW_HEAD)
    (d2,) = seg_sums([d * d])
    on = d * lax.rsqrt(d2 * (1.0 / RW_HEAD) + GN_EPS) * lng_ref[...] + lnb_ref[...]
    o_ref[...] = ((on + bonus_s * v) * g).astype(bf16)

    @pl.when(c == n_steps - 1)
    def _():
        sfin_ref[...] = s_scr[...]


RWKV_CHUNKS_PER_STEP = 2


def _rwkv(prw, shift0, s0_pair, wts, *, t_valid):
    b, tp, _ = prw.shape
    nch = RWKV_CHUNKS_PER_STEP if tp % (RWKV_CHUNKS_PER_STEP * CHUNK) == 0 else 1
    rows = nch * CHUNK
    n_steps = tp // rows
    kern = functools.partial(_rwkv_kernel, t_valid=t_valid, n_steps=n_steps, nch=nch)
    const2 = lambda bi, c: (0, 0)
    row_spec = pl.BlockSpec((1, RW_W), const2)
    return pl.pallas_call(
        kern,
        grid=(b, n_steps),
        in_specs=[
            pl.BlockSpec((None, rows, RW_PAD), lambda bi, c: (bi, c, 0)),
            pl.BlockSpec((None, 1, RW_PAD), lambda bi, c: (bi, 0, 0)),
            pl.BlockSpec((None, N_PAIRS, RW_HEAD, LANES), lambda bi, c: (bi, 0, 0, 0)),
            pl.BlockSpec((1, RW_PAD), const2),
            row_spec,
            pl.BlockSpec((LORA_PAD, RW_W), const2),
            row_spec,
            pl.BlockSpec((LORA_PAD, RW_W), const2),
            pl.BlockSpec((LORA_PAD, RW_W), const2),
            row_spec, row_spec, row_spec, row_spec, row_spec,
        ],
        out_specs=[
            pl.BlockSpec((None, rows, RW_W), lambda bi, c: (bi, c, 0)),
            pl.BlockSpec((None, N_PAIRS, RW_HEAD, LANES), lambda bi, c: (bi, 0, 0, 0)),
        ],
        out_shape=[
            jax.ShapeDtypeStruct((b, tp, RW_W), bf16),
            jax.ShapeDtypeStruct((b, N_PAIRS, RW_HEAD, LANES), f32),
        ],
        scratch_shapes=[pltpu.VMEM((N_PAIRS, RW_HEAD, LANES), f32), pltpu.VMEM((1, RW_PAD), f32)],
        compiler_params=_cparams(("parallel", "arbitrary")),
    )(prw, shift0, s0_pair, wts["mu"], wts["w0"], wts["wd"], wts["a0"], wts["wa"], wts["wg"],
      wts["k_k"], wts["k_a"], wts["r_k"], wts["lnx_g"], wts["lnx_b"])


def _layer_norm(z, g, b):
    mu = jnp.mean(z, axis=-1, keepdims=True)
    d = z - mu
    var = jnp.mean(d * d, axis=-1, keepdims=True)
    return d * lax.rsqrt(var + LN_EPS) * g + b


N_SHARED_OUT = 4


def _outproj_kernel(*refs, aliased, n_steps):
    att_ref, rw_ref, x_ref, wo_ref, g_ref, b_ref, wrh_ref, wrl_ref, br_ref, cnt0_ref = refs[:10]
    h_ref, idx_ref, gate_ref, rank_ref, cnt_ref, carry = refs[10 + N_SHARED_OUT * aliased:]
    step = pl.program_id(0)

    @pl.when(step == 0)
    def _():
        carry[...] = cnt0_ref[...]

    mix = (jnp.dot(att_ref[...], wo_ref[0:ATT_W], preferred_element_type=f32)
           + jnp.dot(rw_ref[...], wo_ref[ATT_W:ATT_W + RW_W], preferred_element_type=f32))
    h = _layer_norm(DN_ALPHA * x_ref[...] + mix, g_ref[...], b_ref[...])
    h_ref[...] = h
    hh = h.astype(bf16)
    hl = (h - hh.astype(f32)).astype(bf16)
    logits = (jnp.dot(hh, wrh_ref[...], preferred_element_type=f32)
              + jnp.dot(hl, wrh_ref[...], preferred_element_type=f32)
              + jnp.dot(hh, wrl_ref[...], preferred_element_type=f32)) + br_ref[...]
    lane = lax.broadcasted_iota(jnp.int32, logits.shape, 1).astype(f32)
    vals, idxs = [], []
    cur = logits
    for _ in range(TOP_K):
        m = jnp.max(cur, axis=-1, keepdims=True)
        i = jnp.min(jnp.where(cur == m, lane, float(LANES)), axis=-1, keepdims=True)
        vals.append(m)
        idxs.append(i)
        cur = jnp.where(lane == i, -jnp.inf, cur)
    es = [jnp.exp(vv - vals[0]) for vv in vals]
    tot = es[0] + es[1] + es[2] + es[3]
    idx_ref[...] = jnp.concatenate(idxs, axis=-1).astype(jnp.int32)
    gate_ref[...] = jnp.concatenate([e / tot for e in es], axis=-1)
    tm = logits.shape[0]
    hits = [jnp.where(lane == i, 1.0, 0.0) for i in idxs]
    multi = hits[0] + hits[1] + hits[2] + hits[3]
    ti = lax.broadcasted_iota(jnp.int32, (tm, tm), 0)
    si = lax.broadcasted_iota(jnp.int32, (tm, tm), 1)
    before = jnp.where(si < ti, 1.0, 0.0).astype(bf16)
    base = carry[...] + jnp.dot(before, multi.astype(bf16), preferred_element_type=f32)
    ranks = [jnp.sum(hh_ * base, axis=-1, keepdims=True) for hh_ in hits]
    rank_ref[...] = jnp.concatenate(ranks, axis=-1).astype(jnp.int32)
    carry[...] = carry[...] + jnp.sum(multi, axis=0, keepdims=True)

    @pl.when(step == n_steps - 1)
    def _():
        cnt_ref[...] = carry[...]


def _outproj(att2, rw2, x2, wts, *, tm, row0, total_rows, prev=None, counts0=None):
    rows = x2.shape[0]
    assert rows % tm == 0 and row0 % tm == 0
    blk0 = row0 // tm
    aliased = prev is not None
    if counts0 is None:
        counts0 = jnp.zeros((1, LANES), f32)
    const = lambda i: (0, 0)
    in_specs = [
        pl.BlockSpec((tm, ATT_W), lambda i: (i, 0)),
        pl.BlockSpec((tm, RW_W), lambda i: (i, 0)),
        pl.BlockSpec((tm, D_MODEL), lambda i: (i, 0)),
        pl.BlockSpec((D_MODEL, D_MODEL), const, pipeline_mode=pl.Buffered(1)),
        pl.BlockSpec((1, D_MODEL), const),
        pl.BlockSpec((1, D_MODEL), const),
        pl.BlockSpec((D_MODEL, LANES), const),
        pl.BlockSpec((D_MODEL, LANES), const),
        pl.BlockSpec((1, LANES), const),
        pl.BlockSpec((1, LANES), const),
    ]
    args = [att2, rw2, x2, wts["w_out"], wts["ln1_g"], wts["ln1_b"], wts["wr_hi"], wts["wr_lo"],
            wts["b_router"], counts0]
    aliases = {}
    if aliased:
        in_specs += [pl.BlockSpec(memory_space=pl.ANY)] * N_SHARED_OUT
        args += list(prev)
        aliases = {len(args) - N_SHARED_OUT + k: k for k in range(N_SHARED_OUT)}
    n_steps = rows // tm
    return pl.pallas_call(
        functools.partial(_outproj_kernel, aliased=int(aliased), n_steps=n_steps),
        grid=(n_steps,),
        in_specs=in_specs,
        out_specs=[
            pl.BlockSpec((tm, D_MODEL), lambda i: (blk0 + i, 0)),
            pl.BlockSpec((tm, TOP_K), lambda i: (blk0 + i, 0)),
            pl.BlockSpec((tm, TOP_K), lambda i: (blk0 + i, 0)),
            pl.BlockSpec((tm, TOP_K), lambda i: (blk0 + i, 0)),
            pl.BlockSpec((1, LANES), const),
        ],
        out_shape=[
            jax.ShapeDtypeStruct((total_rows, D_MODEL), f32),
            jax.ShapeDtypeStruct((total_rows, TOP_K), jnp.int32),
            jax.ShapeDtypeStruct((total_rows, TOP_K), f32),
            jax.ShapeDtypeStruct((total_rows, TOP_K), jnp.int32),
            jax.ShapeDtypeStruct((1, LANES), f32),
        ],
        scratch_shapes=[pltpu.VMEM((1, LANES), f32)],
        input_output_aliases=aliases,
        compiler_params=_cparams(("arbitrary",)),
    )(*args)


MOE_SUB = 256
MOE_NSUB = 4
MOE_SUPER = MOE_SUB * MOE_NSUB
MOE_FF_TILE = 256
MOE_DOWN_N = 512
MOE_MAX_PAD = N_EXPERTS * (MOE_SUB - 1)
X_SUBL = D_MODEL // (2 * LANES)
O_SUBL = D_MODEL // LANES


def _routing(top_idx, rank, counts_f):
    n = top_idx.shape[0]
    n_assign = n * TOP_K
    counts = counts_f[0, :N_EXPERTS].astype(jnp.int32)
    padded = (counts + MOE_SUPER - 1) // MOE_SUPER * MOE_SUPER
    pad_end = jnp.cumsum(padded)
    start = pad_end - padded
    experts = jnp.arange(N_EXPERTS, dtype=jnp.int32)

    def lookup(table, idx):
        return jnp.sum(jnp.where(idx[..., None] == experts, table, 0), axis=-1)

    def bucket(edges, x):
        return jnp.minimum(jnp.sum((edges <= x[..., None]).astype(jnp.int32), axis=-1),
                           N_EXPERTS - 1)

    dest = (lookup(start, top_idx) + rank).astype(jnp.int32)
    n_super = (n_assign + N_EXPERTS * (MOE_SUPER - 1) + MOE_SUPER - 1) // MOE_SUPER
    s_row0 = jnp.arange(n_super, dtype=jnp.int32) * MOE_SUPER
    super_e = bucket(pad_end, s_row0)
    rows_here = jnp.clip(lookup(counts, super_e) - (s_row0 - lookup(start, super_e)), 0, MOE_SUPER)
    rows_here = jnp.where(s_row0 < pad_end[-1], rows_here, 0)
    n_sub = ((rows_here + MOE_SUB - 1) // MOE_SUB).astype(jnp.int32)
    n_used = (pad_end[-1] // MOE_SUPER).astype(jnp.int32).reshape(1)
    n_pad_e = (counts + MOE_SUB - 1) // MOE_SUB * MOE_SUB - counts
    pad_cum = jnp.cumsum(n_pad_e)
    kk = jnp.arange(MOE_MAX_PAD, dtype=jnp.int32)
    pe = bucket(pad_cum, kk)
    pad_dest = (lookup(start + counts - (pad_cum - n_pad_e), pe) + kk).astype(jnp.int32)
    pad_dest = jnp.where(kk < pad_cum[-1], pad_dest, 0)
    n_pad = pad_cum[-1].astype(jnp.int32).reshape(1)
    return dest, super_e, n_sub, n_used, pad_dest, n_pad, n_super


def _tile_copy(src_ref, src_tok, dst_ref, dst_tok, subl, sem):
    s0 = pl.multiple_of(src_tok * subl, subl)
    d0 = pl.multiple_of(dst_tok * subl, subl)
    return pltpu.make_async_copy(src_ref.at[pl.ds(s0, subl)], dst_ref.at[pl.ds(d0, subl)], sem)


def _scatter_kernel(npad_ref, dest_ref, pad_ref, h_ref, xs_hbm, stage, zero, sems, pad_sem,
                    *, tm, n_steps):
    i = pl.program_id(0)
    slot = i % 2

    def drain(sl):
        for _ in range(TOP_K):
            pltpu.make_async_copy(stage.at[sl], stage.at[sl], sems.at[sl]).wait()

    @pl.when(i >= 2)
    def _():
        drain(slot)

    h = h_ref[...]
    half = D_MODEL // 2
    hi = lax.bitcast_convert_type(h[:, :half].astype(bf16).astype(f32), jnp.uint32)
    lo = lax.bitcast_convert_type(h[:, half:].astype(bf16).astype(f32), jnp.uint32)
    packed = hi | (lo >> 16)
    for l in range(X_SUBL):
        stage[slot, pl.ds(l, tm, stride=X_SUBL), :] = packed[:, LANES * l:LANES * (l + 1)]

    def issue(t, carry):
        for jx in range(TOP_K):
            _tile_copy(stage.at[slot], t, xs_hbm, dest_ref[0, 0, jx * tm + t], X_SUBL,
                       sems.at[slot]).start()
        return carry

    lax.fori_loop(0, tm, issue, 0)

    @pl.when(i == 0)
    def _():
        zero[...] = jnp.zeros_like(zero)

        def fill(k, carry):
            _tile_copy(zero, 0, xs_hbm, pad_ref[k], X_SUBL, pad_sem).start()
            return carry

        lax.fori_loop(0, npad_ref[0], fill, 0)

        def fill_wait(k, carry):
            _tile_copy(zero, 0, xs_hbm, 0, X_SUBL, pad_sem).wait()
            return carry

        lax.fori_loop(0, npad_ref[0], fill_wait, 0)

    @pl.when(i == n_steps - 1)
    def _():
        drain(slot)
        if n_steps > 1:
            drain(1 - slot)


def _scatter_rows(h_all, dest, pad_dest, n_pad, n_rows, tm):
    n = h_all.shape[0]
    assert n % tm == 0
    n_steps = n // tm
    dest_blk = dest.reshape(n_steps, tm, TOP_K).transpose(0, 2, 1).reshape(n_steps, 1, TOP_K * tm)
    kern = functools.partial(_scatter_kernel, tm=tm, n_steps=n_steps)
    return pl.pallas_call(
        kern,
        grid_spec=pltpu.PrefetchScalarGridSpec(
            num_scalar_prefetch=1,
            grid=(n_steps,),
            in_specs=[
                pl.BlockSpec((1, 1, TOP_K * tm), lambda i, npad: (i, 0, 0), memory_space=pltpu.SMEM),
                pl.BlockSpec(memory_space=pltpu.SMEM),
                pl.BlockSpec((tm, D_MODEL), lambda i, npad: (i, 0)),
            ],
            out_specs=pl.BlockSpec(memory_space=pl.ANY),
            scratch_shapes=[
                pltpu.VMEM((2, tm * X_SUBL, LANES), jnp.uint32),
                pltpu.VMEM((X_SUBL, LANES), jnp.uint32),
                pltpu.SemaphoreType.DMA((2,)),
                pltpu.SemaphoreType.DMA(()),
            ],
        ),
        out_shape=jax.ShapeDtypeStruct((n_rows * X_SUBL, LANES), jnp.uint32),
        compiler_params=_cparams(("arbitrary",)),
    )(n_pad, dest_blk, pad_dest, h_all)


def _expert_kernel(se_ref, nsub_ref, nused_ref, x_ref, wg_ref, wl_ref, bg_ref, bl_ref, wd_ref,
                   bd_ref, o_ref, xb_scr, acc_scr, *, nf):
    s = pl.program_id(0)
    j = pl.program_id(1)
    n_sub = nsub_ref[s]
    half = D_MODEL // 2

    def step(m):
        rows = slice(0, m)

        @pl.when(j == 0)
        def _():
            for l in range(X_SUBL):
                u = x_ref[pl.ds(l, m, stride=X_SUBL), :]
                hi = lax.bitcast_convert_type(u & jnp.uint32(0xFFFF0000), f32)
                lo = lax.bitcast_convert_type(u << 16, f32)
                xb_scr[rows, LANES * l:LANES * (l + 1)] = hi.astype(bf16)
                xb_scr[rows, half + LANES * l:half + LANES * (l + 1)] = lo.astype(bf16)
            acc_scr[rows, :] = jnp.broadcast_to(bd_ref[...], (m, D_MODEL))

        xb = xb_scr[rows, :]
        hg = jnp.dot(xb, wg_ref[...].astype(bf16), preferred_element_type=f32) + bg_ref[...]
        hl = jnp.dot(xb, wl_ref[...].astype(bf16), preferred_element_type=f32) + bl_ref[...]
        glu = jnp.minimum(hg, SWIGLU_LIMIT)
        lin = jnp.clip(hl, -SWIGLU_LIMIT, SWIGLU_LIMIT)
        act = (glu * (1.0 / (1.0 + jnp.exp(-SWIGLU_ALPHA * glu))) * (lin + 1.0)).astype(bf16)
        wd = wd_ref[...].astype(bf16)
        for n0 in range(0, D_MODEL, MOE_DOWN_N):
            acc_scr[rows, n0:n0 + MOE_DOWN_N] += jnp.dot(
                act, wd[:, n0:n0 + MOE_DOWN_N], preferred_element_type=f32)

        @pl.when(j == nf - 1)
        def _():
            for l in range(O_SUBL):
                o_ref[pl.ds(l, m, stride=O_SUBL), :] = acc_scr[rows, LANES * l:LANES * (l + 1)]

    for k in range(1, MOE_NSUB + 1):
        pl.when(n_sub == k)(functools.partial(step, k * MOE_SUB))


def _experts(xs, super_e, n_sub, n_used, w_up, b_up, w_down, b_down, n_super):
    tf = MOE_FF_TILE
    nf = D_FF // tf
    last = lambda s, nu: jnp.minimum(s, nu[0] - 1)
    b_up3 = b_up.reshape(N_EXPERTS, 1, 2 * D_FF)
    b_down3 = b_down.reshape(N_EXPERTS, 1, D_MODEL)
    e_of = lambda s, se, nu: se[last(s, nu)]
    return pl.pallas_call(
        functools.partial(_expert_kernel, nf=nf),
        grid_spec=pltpu.PrefetchScalarGridSpec(
            num_scalar_prefetch=3,
            grid=(n_super, nf),
            in_specs=[
                pl.BlockSpec((MOE_SUPER * X_SUBL, LANES), lambda s, j, se, ns, nu: (last(s, nu), 0)),
                pl.BlockSpec((None, D_MODEL, tf), lambda s, j, se, ns, nu: (e_of(s, se, nu), 0, j)),
                pl.BlockSpec((None, D_MODEL, tf),
                             lambda s, j, se, ns, nu: (e_of(s, se, nu), 0, nf + j)),
                pl.BlockSpec((None, 1, tf), lambda s, j, se, ns, nu: (e_of(s, se, nu), 0, j)),
                pl.BlockSpec((None, 1, tf), lambda s, j, se, ns, nu: (e_of(s, se, nu), 0, nf + j)),
                pl.BlockSpec((None, tf, D_MODEL), lambda s, j, se, ns, nu: (e_of(s, se, nu), j, 0)),
                pl.BlockSpec((None, 1, D_MODEL), lambda s, j, se, ns, nu: (e_of(s, se, nu), 0, 0)),
            ],
            out_specs=pl.BlockSpec((MOE_SUPER * O_SUBL, LANES),
                                   lambda s, j, se, ns, nu: (last(s, nu), 0)),
            scratch_shapes=[pltpu.VMEM((MOE_SUPER, D_MODEL), bf16),
                            pltpu.VMEM((MOE_SUPER, D_MODEL), f32)],
        ),
        out_shape=jax.ShapeDtypeStruct((n_super * MOE_SUPER * O_SUBL, LANES), f32),
        compiler_params=_cparams(("arbitrary", "arbitrary")),
    )(super_e, n_sub, n_used, xs, w_up, w_up, b_up3, b_up3, w_down, b_down3)


def _combine_kernel(dest_ref, nxt_ref, gate_ref, h_ref, rows_hbm, g_ref, b_ref, yp_ref, ys_ref,
                    buf, sems, *, tm, n_first, n_steps):
    i = pl.program_id(0)
    slot = i % 2

    def fetch(idx_ref, sl):
        def body(t, carry):
            for jx in range(TOP_K):
                _tile_copy(rows_hbm, idx_ref[0, 0, jx * tm + t], buf.at[sl, jx], t, O_SUBL,
                           sems.at[sl]).start()
            return carry
        lax.fori_loop(0, tm, body, 0)

    @pl.when(i == 0)
    def _():
        fetch(dest_ref, slot)

    @pl.when(i + 1 < n_steps)
    def _():
        fetch(nxt_ref, 1 - slot)

    for jx in range(TOP_K):
        pltpu.make_async_copy(buf.at[slot, jx], buf.at[slot, jx], sems.at[slot]).wait()
    gate = gate_ref[...]
    cols = []
    for l in range(O_SUBL):
        acc = gate[:, 0:1] * buf[slot, 0, pl.ds(l, tm, stride=O_SUBL), :]
        for jx in range(1, TOP_K):
            acc = acc + gate[:, jx:jx + 1] * buf[slot, jx, pl.ds(l, tm, stride=O_SUBL), :]
        cols.append(acc)
    y = jnp.concatenate(cols, axis=-1)
    out = _layer_norm(DN_ALPHA * h_ref[...] + y, g_ref[...], b_ref[...])

    @pl.when(i < n_first)
    def _():
        yp_ref[...] = out

    @pl.when(i >= n_first)
    def _():
        ys_ref[...] = out


def _combine(rows_out, dest, gate, h_all, ln_g, ln_b, *, tm, n_first_rows):
    n = h_all.shape[0]
    assert n % tm == 0 and n_first_rows % tm == 0
    nblk = n // tm
    n_first = n_first_rows // tm
    dest_blk = dest.reshape(nblk, tm, TOP_K).transpose(0, 2, 1).reshape(nblk, 1, TOP_K * tm)
    kern = functools.partial(_combine_kernel, tm=tm, n_first=n_first, n_steps=nblk)
    const = lambda i: (0, 0)
    return pl.pallas_call(
        kern,
        grid=(nblk,),
        in_specs=[
            pl.BlockSpec((1, 1, TOP_K * tm), lambda i: (i, 0, 0), memory_space=pltpu.SMEM),
            pl.BlockSpec((1, 1, TOP_K * tm), lambda i: (jnp.minimum(i + 1, nblk - 1), 0, 0),
                         memory_space=pltpu.SMEM),
            pl.BlockSpec((tm, TOP_K), lambda i: (i, 0)),
            pl.BlockSpec((tm, D_MODEL), lambda i: (i, 0)),
            pl.BlockSpec(memory_space=pl.ANY),
            pl.BlockSpec((1, D_MODEL), const),
            pl.BlockSpec((1, D_MODEL), const),
        ],
        out_specs=[
            pl.BlockSpec((tm, D_MODEL), lambda i: (jnp.minimum(i, n_first - 1), 0)),
            pl.BlockSpec((tm, D_MODEL), lambda i: (jnp.maximum(i - n_first, 0), 0)),
        ],
        out_shape=[
            jax.ShapeDtypeStruct((n_first_rows, D_MODEL), f32),
            jax.ShapeDtypeStruct((n - n_first_rows, D_MODEL), f32),
        ],
        scratch_shapes=[pltpu.VMEM((2, TOP_K, tm * O_SUBL, LANES), f32),
                        pltpu.SemaphoreType.DMA((2,))],
        compiler_params=_cparams(("arbitrary",)),
    )(dest_blk, dest_blk, gate, h_all, rows_out, ln_g, ln_b)


def _t5_bucket(rel):
    half = NUM_BUCKETS // 2
    exact = half // 2
    n = jnp.abs(rel)
    log_part = exact + (jnp.log(jnp.maximum(n, 1).astype(jnp.float32) / exact)
                        / math.log(MAX_DISTANCE / exact) * (half - exact)).astype(jnp.int32)
    log_part = jnp.minimum(log_part, half - 1)
    return jnp.where(rel > 0, half, 0) + jnp.where(n < exact, n, log_part)


def _band_bias(rel_bias):
    qi = jnp.arange(CHUNK)[:, None]
    km = jnp.arange(BAND)[None, :]
    bucket = _t5_bucket(km - WINDOW - qi)
    return jnp.transpose(rel_bias[bucket], (2, 0, 1)).astype(jnp.float32)


def _pad_cols(a, width):
    return jnp.pad(a, ((0, 0), (0, width - a.shape[-1])))


def _pair_state(s):
    b = s.shape[0]
    return s.reshape(b, N_PAIRS, 2, RW_HEAD, RW_HEAD).transpose(0, 1, 3, 2, 4).reshape(
        b, N_PAIRS, RW_HEAD, LANES)


def _unpair_state(s):
    b = s.shape[0]
    return s.reshape(b, N_PAIRS, RW_HEAD, 2, RW_HEAD).transpose(0, 1, 3, 2, 4).reshape(
        b, RW_HEADS, RW_HEAD, RW_HEAD)


def _mix_group(x, k_hist, v_hist, hist_valid, wkv0, shift0, bias, sinks3, w_in_pad, rw_wts,
               *, in_tm, attn_nc):
    b, t, _ = x.shape
    q, kv, prw = _inproj(x.reshape(b * t, D_MODEL), w_in_pad, in_tm)
    tp = -(-t // (attn_nc * CHUNK)) * (attn_nc * CHUNK)
    q4 = q.reshape(ATT_HEADS, b, t, HEAD_DIM)
    kv3 = kv.reshape(b, t, 2 * KV_W)
    prw3 = prw.reshape(b, t, RW_PAD)
    if tp != t:
        q4 = jnp.pad(q4, ((0, 0), (0, 0), (0, tp - t), (0, 0)))
        prw3 = jnp.pad(prw3, ((0, 0), (0, tp - t), (0, 0)))
    hist = jnp.concatenate([k_hist.reshape(b, WINDOW, KV_W), v_hist.reshape(b, WINDOW, KV_W)], axis=-1)
    kvfull = jnp.concatenate([hist, kv3, jnp.zeros((b, tp - t, 2 * KV_W), f32)], axis=1)
    att = _attention(q4, kvfull, bias, sinks3, nc=attn_nc, t_valid=t, hist_valid=hist_valid)
    rw, s_fin = _rwkv(prw3, _pad_cols(shift0.reshape(b, RW_PROJ), RW_PAD).reshape(b, 1, RW_PAD),
                      _pair_state(wkv0.astype(f32)), rw_wts, t_valid=t)
    new_kv = kvfull[:, t:t + WINDOW]
    new_k = new_kv[..., :KV_W].reshape(b, WINDOW, ATT_KV_HEADS, HEAD_DIM)
    new_v = new_kv[..., KV_W:].reshape(b, WINDOW, ATT_KV_HEADS, HEAD_DIM)
    shift = prw3[:, t - 1:t, :RW_PROJ]
    return (att[:, :t].reshape(b * t, ATT_W), rw[:, :t].reshape(b * t, RW_W),
            new_k, new_v, _unpair_state(s_fin), shift)


SCATTER_TM = 128
COMBINE_TM = 128


def kernel(x_prompt, x_sample, cache_k, cache_v, state_wkv, state_shift, rel_bias, w_in, attn_sinks, rw_mu, rw_w0, rw_decay_up, rw_a0, rw_iclr_up, rw_gate_up, rw_k_k, rw_k_a, rw_r_k, rw_lnx_g, rw_lnx_b, w_out, ln1_g, ln1_b, w_router, b_router, w_up, b_up, w_down, b_down, ln2_g, ln2_b):
    assert w_in.shape[0] == DEPTH == 1
    l = 0
    bp, tp_, _ = x_prompt.shape
    bs, ts, _ = x_sample.shape
    bias = _band_bias(rel_bias)
    sinks3 = attn_sinks[l].astype(f32).reshape(ATT_HEADS, 1, 1)

    w_in_pad = _pad_cols(w_in[l], IN_PAD).astype(bf16)

    def lora_rows(w, row0):
        return jnp.zeros((LORA_PAD, RW_W), f32).at[row0:row0 + w.shape[0]].set(w).astype(bf16)

    rw_wts = {
        "mu": _pad_cols(rw_mu[l].reshape(1, RW_PROJ), RW_PAD),
        "w0": rw_w0[l].reshape(1, RW_W),
        "wd": lora_rows(rw_decay_up[l], 0),
        "a0": rw_a0[l].reshape(1, RW_W),
        "wa": lora_rows(rw_iclr_up[l], DECAY_LORA),
        "wg": lora_rows(rw_gate_up[l], DECAY_LORA + ICLR_LORA),
        "k_k": rw_k_k[l].reshape(1, RW_W),
        "k_a": rw_k_a[l].reshape(1, RW_W),
        "r_k": rw_r_k[l].reshape(1, RW_W),
        "lnx_g": rw_lnx_g[l].reshape(1, RW_W),
        "lnx_b": rw_lnx_b[l].reshape(1, RW_W),
    }
    wr = _pad_cols(w_router[l], LANES)
    wr_hi = wr.astype(bf16)
    op_wts = {
        "w_out": w_out[l].astype(bf16),
        "ln1_g": ln1_g[l].reshape(1, D_MODEL),
        "ln1_b": ln1_b[l].reshape(1, D_MODEL),
        "wr_hi": wr_hi,
        "wr_lo": (wr - wr_hi.astype(f32)).astype(bf16),
        "b_router": jnp.concatenate([b_router[l].astype(f32),
                                     jnp.full((LANES - N_EXPERTS,), NEG_BIG, f32)]).reshape(1, LANES),
    }

    zero_kv = jnp.zeros((bp, WINDOW, ATT_KV_HEADS, HEAD_DIM), f32)
    att_p, rwo_p, k1, v1, w1, s1 = _mix_group(
        x_prompt, zero_kv, zero_kv, False, jnp.zeros((bp, RW_HEADS, RW_HEAD, RW_HEAD), f32),
        jnp.zeros((bp, 1, RW_PROJ), f32), bias, sinks3, w_in_pad, rw_wts,
        in_tm=min(256, bp * tp_), attn_nc=min(8, -(-tp_ // CHUNK)))
    att_s, rwo_s, k2, v2, w2, s2 = _mix_group(
        x_sample, cache_k[l], cache_v[l], True, state_wkv[l], state_shift[l], bias, sinks3,
        w_in_pad, rw_wts, in_tm=min(256, bs * ts), attn_nc=1)

    n_p, n_s = bp * tp_, bs * ts
    n_all = n_p + n_s
    tm_p, tm_s = min(256, n_p), min(128, n_s)
    *outs, counts_p = _outproj(att_p, rwo_p, x_prompt.reshape(n_p, D_MODEL), op_wts,
                               tm=tm_p, row0=0, total_rows=n_all)
    h_all, top_idx, gate, rank, counts = _outproj(
        att_s, rwo_s, x_sample.reshape(n_s, D_MODEL), op_wts,
        tm=tm_s, row0=n_p, total_rows=n_all, prev=outs, counts0=counts_p)

    dest, super_e, n_sub, n_used, pad_dest, n_pad, n_super = _routing(top_idx, rank, counts)
    xs = _scatter_rows(h_all, dest, pad_dest, n_pad, n_super * MOE_SUPER, min(SCATTER_TM, n_s))
    rows_out = _experts(xs, super_e, n_sub, n_used, w_up[l], b_up[l], w_down[l], b_down[l],
                        n_super)
    y_p, y_s = _combine(rows_out, dest, gate, h_all, ln2_g[l].reshape(1, D_MODEL),
                        ln2_b[l].reshape(1, D_MODEL), tm=min(COMBINE_TM, n_s), n_first_rows=n_p)

    return (y_p.reshape(bp, tp_, D_MODEL), y_s.reshape(bs, ts, D_MODEL),
            k1[None], v1[None], w1[None], s1[None], k2[None], v2[None], w2[None], s2[None])
```

```python
import functools
import math

import jax
import jax.numpy as jnp
from jax import lax
from jax.experimental import pallas as pl
from jax.experimental.pallas import tpu as pltpu

f32 = jnp.float32
bf16 = jnp.bfloat16

D_MODEL = 2048
CHUNK = 64
ATT_HEADS = 16
ATT_KV_HEADS = 2
HEAD_DIM = 64
ATT_GROUP = ATT_HEADS // ATT_KV_HEADS
ATT_W = ATT_HEADS * HEAD_DIM
KV_W = ATT_KV_HEADS * HEAD_DIM
ATT_PROJ = ATT_W + 2 * KV_W
WINDOW = 128
BAND = WINDOW + CHUNK
NUM_BUCKETS = 32
MAX_DISTANCE = 128
RW_HEAD = 64
RW_W = 1024
RW_HEADS = RW_W // RW_HEAD
DECAY_LORA = 96
ICLR_LORA = 96
GATE_LORA = 128
RW_PROJ = 3 * RW_W + DECAY_LORA + ICLR_LORA + GATE_LORA
GN_EPS = 64e-5
LN_EPS = 1e-5
N_EXPERTS = 32
TOP_K = 4
D_FF = D_MODEL
SWIGLU_LIMIT = 7.0
SWIGLU_ALPHA = 1.702
DEPTH = 1
DN_ALPHA = (2 * DEPTH) ** 0.25

LANES = 128
VMEM_LIMIT = 56 * 1024 * 1024

LORA_W = DECAY_LORA + ICLR_LORA + GATE_LORA
LORA_PAD = -(-LORA_W // LANES) * LANES
RW_PAD = 3 * RW_W + LORA_PAD
IN_PAD = ATT_PROJ + RW_PAD
N_PAIRS = RW_HEADS // 2
NEG_BIG = -1e30


def _cparams(sem):
    return pltpu.CompilerParams(dimension_semantics=sem, vmem_limit_bytes=VMEM_LIMIT)


def _inproj_kernel(x_ref, w_ref, q_ref, kv_ref, rw_ref):
    acc = jnp.dot(x_ref[...].astype(bf16), w_ref[...], preferred_element_type=f32)
    scale = HEAD_DIM ** -0.5
    for h in range(ATT_HEADS):
        q_ref[h] = (acc[:, h * HEAD_DIM:(h + 1) * HEAD_DIM] * scale).astype(bf16)
    kv_ref[...] = acc[:, ATT_W:ATT_PROJ]
    rw_ref[...] = acc[:, ATT_PROJ:IN_PAD]


def _inproj(x2, w_pad, tm):
    rows = x2.shape[0]
    assert rows % tm == 0
    return pl.pallas_call(
        _inproj_kernel,
        grid=(rows // tm,),
        in_specs=[
            pl.BlockSpec((tm, D_MODEL), lambda i: (i, 0)),
            pl.BlockSpec((D_MODEL, IN_PAD), lambda i: (0, 0), pipeline_mode=pl.Buffered(1)),
        ],
        out_specs=[
            pl.BlockSpec((ATT_HEADS, tm, HEAD_DIM), lambda i: (0, i, 0)),
            pl.BlockSpec((tm, 2 * KV_W), lambda i: (i, 0)),
            pl.BlockSpec((tm, RW_PAD), lambda i: (i, 0)),
        ],
        out_shape=[
            jax.ShapeDtypeStruct((ATT_HEADS, rows, HEAD_DIM), bf16),
            jax.ShapeDtypeStruct((rows, 2 * KV_W), f32),
            jax.ShapeDtypeStruct((rows, RW_PAD), f32),
        ],
        compiler_params=_cparams(("parallel",)),
    )(x2, w_pad)


ATT_UNROLL = 4


def _attn_kernel(q_ref, kvm_ref, kva_ref, kvb_ref, bias_ref, sink_ref, o_ref, kvbuf,
                 *, nc, t_valid, hist_valid):
    j = pl.program_id(1)
    kvbuf[0:nc * CHUNK] = kvm_ref[...].astype(bf16)
    kvbuf[nc * CHUNK:(nc + 1) * CHUNK] = kva_ref[...].astype(bf16)
    kvbuf[(nc + 1) * CHUNK:(nc + 2) * CHUNK] = kvb_ref[...].astype(bf16)
    m_idx = lax.broadcasted_iota(jnp.int32, (1, 1, BAND), 2)
    for c0 in range(0, nc, ATT_UNROLL):
        items = [(c, g) for c in range(c0, min(c0 + ATT_UNROLL, nc)) for g in range(ATT_KV_HEADS)]
        bands = {c: kvbuf[c * CHUNK:c * CHUNK + BAND, :] for c, _ in items}
        valid = {}
        for c in bands:
            idx = (j * nc + c) * CHUNK + m_idx
            v = idx - WINDOW < t_valid
            valid[c] = v if hist_valid else jnp.logical_and(v, idx >= WINDOW)
        s = [lax.dot_general(
                q_ref[g * ATT_GROUP:(g + 1) * ATT_GROUP, c * CHUNK:(c + 1) * CHUNK, :].reshape(
                    ATT_GROUP * CHUNK, HEAD_DIM),
                bands[c][:, g * HEAD_DIM:(g + 1) * HEAD_DIM],
                (((1,), (1,)), ((), ())), preferred_element_type=f32) for c, g in items]
        s = [jnp.where(valid[c], s[i].reshape(ATT_GROUP, CHUNK, BAND)
                       + bias_ref[g * ATT_GROUP:(g + 1) * ATT_GROUP], NEG_BIG)
             for i, (c, g) in enumerate(items)]
        sk = [sink_ref[g * ATT_GROUP:(g + 1) * ATT_GROUP] for _, g in items]
        m = [jnp.maximum(jnp.max(s[i], axis=-1, keepdims=True), sk[i]) for i in range(len(items))]
        p = [jnp.exp(s[i] - m[i]) for i in range(len(items))]
        den = [jnp.sum(p[i], axis=-1, keepdims=True) + jnp.exp(sk[i] - m[i])
               for i in range(len(items))]
        o = [jnp.dot(p[i].reshape(ATT_GROUP * CHUNK, BAND).astype(bf16),
                     bands[c][:, KV_W + g * HEAD_DIM:KV_W + (g + 1) * HEAD_DIM],
                     preferred_element_type=f32).reshape(ATT_GROUP, CHUNK, HEAD_DIM) / den[i]
             for i, (c, g) in enumerate(items)]
        for c in bands:
            heads = [o[i][h] for i, (ci, _) in enumerate(items) if ci == c for h in range(ATT_GROUP)]
            o_ref[c * CHUNK:(c + 1) * CHUNK, :] = jnp.concatenate(heads, axis=-1).astype(bf16)


def _attention(q4, kvfull, bias, sinks3, *, nc, t_valid, hist_valid):
    _, b, tp, _ = q4.shape
    assert tp % (nc * CHUNK) == 0 and kvfull.shape[1] == tp + WINDOW
    nblk = tp // (nc * CHUNK)
    kern = functools.partial(_attn_kernel, nc=nc, t_valid=t_valid, hist_valid=hist_valid)
    return pl.pallas_call(
        kern,
        grid=(b, nblk),
        in_specs=[
            pl.BlockSpec((ATT_HEADS, None, nc * CHUNK, HEAD_DIM), lambda bi, j: (0, bi, j, 0)),
            pl.BlockSpec((None, nc * CHUNK, 2 * KV_W), lambda bi, j: (bi, j, 0)),
            pl.BlockSpec((None, CHUNK, 2 * KV_W), lambda bi, j: (bi, (j + 1) * nc, 0)),
            pl.BlockSpec((None, CHUNK, 2 * KV_W), lambda bi, j: (bi, (j + 1) * nc + 1, 0)),
            pl.BlockSpec((ATT_HEADS, CHUNK, BAND), lambda bi, j: (0, 0, 0)),
            pl.BlockSpec((ATT_HEADS, 1, 1), lambda bi, j: (0, 0, 0)),
        ],
        out_specs=pl.BlockSpec((None, nc * CHUNK, ATT_W), lambda bi, j: (bi, j, 0)),
        out_shape=jax.ShapeDtypeStruct((b, tp, ATT_W), bf16),
        scratch_shapes=[pltpu.VMEM(((nc + 2) * CHUNK, 2 * KV_W), bf16)],
        compiler_params=_cparams(("parallel", "parallel")),
    )(q4, kvfull, kvfull, kvfull, bias, sinks3)


def _rwkv_kernel(p_ref, shift0_ref, s0_ref, mu_ref, w0_ref, wd_ref, a0_ref, wa_ref, wg_ref,
                 kk_ref, ka_ref, rk_ref, lng_ref, lnb_ref, o_ref, sfin_ref, s_scr, last_scr,
                 *, t_valid, n_steps, nch):
    c = pl.program_id(1)
    L = CHUNK
    R = nch * L

    @pl.when(c == 0)
    def _():
        s_scr[...] = s0_ref[...]
        last_scr[...] = shift0_ref[...]

    p = p_ref[...]
    row = lax.broadcasted_iota(jnp.int32, (R, 1), 0)
    shifted = jnp.where(row == 0, last_scr[...], pltpu.roll(p, 1, axis=0))
    last_scr[...] = p[R - 1:R, :]
    xm = p + (shifted - p) * mu_ref[...]
    r = xm[:, 0:RW_W]
    k = xm[:, RW_W:2 * RW_W]
    v = xm[:, 2 * RW_W:3 * RW_W]
    tail = xm[:, 3 * RW_W:RW_PAD]

    def sigmoid(z):
        return 1.0 / (1.0 + jnp.exp(-z))

    w_log = w0_ref[...] + jnp.dot(jnp.tanh(tail).astype(bf16), wd_ref[...],
                                  preferred_element_type=f32)
    z = -w_log
    softplus = jnp.maximum(z, 0.0) + jnp.log(1.0 + jnp.exp(-jnp.abs(z)))
    ld = -jnp.exp(-softplus - 0.5)
    a = sigmoid(a0_ref[...] + jnp.dot(tail.astype(bf16), wa_ref[...], preferred_element_type=f32))
    g = jnp.dot(sigmoid(tail).astype(bf16), wg_ref[...], preferred_element_type=f32)

    lane = lax.broadcasted_iota(jnp.int32, (1, LANES), 1)
    lo_half = lane < RW_HEAD
    rr = lax.broadcasted_iota(jnp.int32, (LANES, LANES), 0)
    cc = lax.broadcasted_iota(jnp.int32, (LANES, LANES), 1)
    same_head = (rr // RW_HEAD) == (cc // RW_HEAD)
    ones_bd = jnp.where(same_head, 1.0, 0.0).astype(bf16)

    def seg_sums(xs):
        n = len(xs)
        x = jnp.concatenate(xs, axis=0) if n > 1 else xs[0]
        hi = x.astype(bf16)
        lo = (x - hi.astype(f32)).astype(bf16)
        both = jnp.concatenate([hi, lo], axis=0)
        m = 2 * n * R
        tiles = jnp.concatenate([both[:, LANES * t:LANES * (t + 1)] for t in range(N_PAIRS)], axis=0)
        res = jnp.dot(tiles, ones_bd, preferred_element_type=f32)
        y = jnp.concatenate([res[m * t:m * (t + 1)] for t in range(N_PAIRS)], axis=1)
        y = y[:n * R] + y[n * R:]
        return [y[i * R:(i + 1) * R] for i in range(n)]

    kk = k * kk_ref[...]
    k_mod = k * (1.0 + (a - 1.0) * ka_ref[...])
    nrm2, bonus_s = seg_sums([kk * kk, r * k_mod * rk_ref[...]])
    kk = kk / jnp.maximum(jnp.sqrt(nrm2), 1e-12)
    b = kk * a

    if t_valid % R != 0:
        live = (c * R + row) < t_valid
        ld = jnp.where(live, ld, 0.0)
        b = jnp.where(live, b, 0.0)
        k_mod = jnp.where(live, k_mod, 0.0)

    h1 = ld.astype(bf16)
    r1 = ld - h1.astype(f32)
    h2 = r1.astype(bf16)
    h3 = (r1 - h2.astype(f32)).astype(bf16)
    ti = lax.broadcasted_iota(jnp.int32, (L, 3 * L), 0)
    si = lax.broadcasted_iota(jnp.int32, (L, 3 * L), 1) % L
    tri3 = jnp.where(si <= ti, 1.0, 0.0).astype(bf16)
    rs = [slice(L * ch, L * (ch + 1)) for ch in range(nch)]
    cums = [jnp.dot(tri3, jnp.concatenate([h1[rs[ch]], h2[rs[ch]], h3[rs[ch]]], axis=0),
                    preferred_element_type=f32) for ch in range(nch)]
    cum = jnp.concatenate(cums, axis=0) if nch > 1 else cums[0]
    cum_ls = [cums[ch][L - 1:L, :] for ch in range(nch)]
    cum_l = (jnp.concatenate([jnp.broadcast_to(x, (L, RW_W)) for x in cum_ls], axis=0)
             if nch > 1 else cum_ls[0])
    g_l = [jnp.exp(x) for x in cum_ls]
    g_inv = jnp.exp(-cum)
    g_rest = jnp.exp(cum_l - cum)
    kq = (kk * jnp.exp(cum - ld)).astype(bf16)
    rq_f = r * jnp.exp(cum)
    rq = rq_f.astype(bf16)
    bt = (b * g_inv).astype(bf16)
    kt = (k_mod * g_inv).astype(bf16)
    bh = (b * g_rest).astype(bf16)
    kh = (k_mod * g_rest).astype(bf16)
    vb = v.astype(bf16)

    def bd(x):
        zero = jnp.zeros_like(x)
        return jnp.concatenate([jnp.where(lo_half, x, zero), jnp.where(lo_half, zero, x)], axis=0)

    def mm(x, y):
        return jnp.dot(x, y, preferred_element_type=f32)

    def mm_nt(x, y):
        return lax.dot_general(x, y, (((1,), (1,)), ((), ())), preferred_element_type=f32)

    def mm_tn(x, y):
        return lax.dot_general(x, y, (((0,), (0,)), ((), ())), preferred_element_type=f32)

    tt = lax.broadcasted_iota(jnp.int32, (L, LANES), 0)
    ss = lax.broadcasted_iota(jnp.int32, (L, LANES), 1) % RW_HEAD
    strict = ss < tt
    incl = ss <= tt
    eye_pair = jnp.where(ss == tt, 1.0, 0.0).astype(f32)

    items = [(ch, t) for ch in range(nch) for t in range(N_PAIRS)]
    I = range(len(items))
    sl = [slice(LANES * t, LANES * (t + 1)) for t in range(N_PAIRS)]

    def tile(x, i):
        ch, t = items[i]
        return x[rs[ch], sl[t]]

    a_all = [mm_nt(jnp.concatenate([tile(kq, i), tile(rq, i)], axis=0),
                   jnp.concatenate([bd(tile(bt, i)), bd(tile(kt, i))], axis=0)) for i in I]
    a_bk = [jnp.where(strict, a_all[i][:L, :LANES], 0.0) for i in I]
    a_kk = [jnp.where(strict, a_all[i][:L, LANES:], 0.0).astype(bf16) for i in I]
    a_rb = [jnp.where(incl, a_all[i][L:, :LANES], 0.0).astype(bf16) for i in I]
    a_rk = [jnp.where(incl, a_all[i][L:, LANES:], 0.0).astype(bf16) for i in I]
    bd_v = [bd(tile(vb, i)) for i in I]
    akv = [mm(a_kk[i], bd_v[i]).astype(bf16) for i in I]
    w_inv = [eye_pair - a_bk[i] for i in I]
    pw = [a_bk[i].astype(bf16) for i in I]
    pw_bd = [bd(pw[i]) for i in I]
    for _ in range(5):
        pw = [mm(pw[i], pw_bd[i]).astype(bf16) for i in I]
        pw_bd = [bd(pw[i]) for i in I]
        w_inv = [w_inv[i] + mm(w_inv[i].astype(bf16), pw_bd[i]) for i in I]
    qu = [mm(w_inv[i].astype(bf16), jnp.concatenate([bd(tile(kq, i)), bd(akv[i])], axis=1))
          for i in I]
    q_m = [qu[i][:, :LANES].astype(bf16) for i in I]
    u_m = [qu[i][:, LANES:].astype(bf16) for i in I]
    m_full = [mm_tn(q_m[i], tile(bh, i)) for i in I]
    neg_m = [jnp.where(same_head, -m_full[i], 0.0).astype(bf16) for i in I]
    c_full = [mm_tn(jnp.concatenate([tile(vb, i), u_m[i]], axis=0),
                    jnp.concatenate([tile(kh, i), -tile(bh, i)], axis=0)) for i in I]
    go = [mm(a_rb[i], jnp.concatenate([bd(q_m[i]), bd(u_m[i])], axis=1)) for i in I]
    o_rk = [mm(a_rk[i], bd_v[i]) for i in I]
    g_m = [(tile(rq_f, i) - go[i][:, :LANES]).astype(bf16) for i in I]
    o_in = [o_rk[i] - go[i][:, LANES:] for i in I]
    c_pair = [jnp.where(lo_half, c_full[i][:RW_HEAD], c_full[i][RW_HEAD:]) for i in I]
    s_cur = [s_scr[t] for t in range(N_PAIRS)]
    o_rows = []
    for ch in range(nch):
        ii = [ch * N_PAIRS + t for t in range(N_PAIRS)]
        s_b = [s.astype(bf16) for s in s_cur]
        o_rows.append(jnp.concatenate(
            [mm_nt(g_m[i], bd(s_b[t])) + o_in[i] for t, i in enumerate(ii)], axis=1))
        s_upd = [mm(s_b[t], neg_m[i]) for t, i in enumerate(ii)]
        s_cur = [s_cur[t] * g_l[ch][:, sl[t]] + s_upd[t] + c_pair[i] for t, i in enumerate(ii)]
    for t in range(N_PAIRS):
        s_scr[t] = s_cur[t]

    o = jnp.concatenate(o_rows, axis=0) if nch > 1 else o_rows[0]
    (o_sum,) = seg_sums([o])
    d = o - o_sum * (1.0 / RW_HEAD)
    (d2,) = seg_sums([d * d])
    on = d * lax.rsqrt(d2 * (1.0 / RW_HEAD) + GN_EPS) * lng_ref[...] + lnb_ref[...]
    o_ref[...] = ((on + bonus_s * v) * g).astype(bf16)

    @pl.when(c == n_steps - 1)
    def _():
        sfin_ref[...] = s_scr[...]


RWKV_CHUNKS_PER_STEP = 2


def _rwkv(prw, shift0, s0_pair, wts, *, t_valid):
    b, tp, _ = prw.shape
    nch = RWKV_CHUNKS_PER_STEP if tp % (RWKV_CHUNKS_PER_STEP * CHUNK) == 0 else 1
    rows = nch * CHUNK
    n_steps = tp // rows
    kern = functools.partial(_rwkv_kernel, t_valid=t_valid, n_steps=n_steps, nch=nch)
    const2 = lambda bi, c: (0, 0)
    row_spec = pl.BlockSpec((1, RW_W), const2)
    return pl.pallas_call(
        kern,
        grid=(b, n_steps),
        in_specs=[
            pl.BlockSpec((None, rows, RW_PAD), lambda bi, c: (bi, c, 0)),
            pl.BlockSpec((None, 1, RW_PAD), lambda bi, c: (bi, 0, 0)),
            pl.BlockSpec((None, N_PAIRS, RW_HEAD, LANES), lambda bi, c: (bi, 0, 0, 0)),
            pl.BlockSpec((1, RW_PAD), const2),
            row_spec,
            pl.BlockSpec((LORA_PAD, RW_W), const2),
            row_spec,
            pl.BlockSpec((LORA_PAD, RW_W), const2),
            pl.BlockSpec((LORA_PAD, RW_W), const2),
            row_spec, row_spec, row_spec, row_spec, row_spec,
        ],
        out_specs=[
            pl.BlockSpec((None, rows, RW_W), lambda bi, c: (bi, c, 0)),
            pl.BlockSpec((None, N_PAIRS, RW_HEAD, LANES), lambda bi, c: (bi, 0, 0, 0)),
        ],
        out_shape=[
            jax.ShapeDtypeStruct((b, tp, RW_W), bf16),
            jax.ShapeDtypeStruct((b, N_PAIRS, RW_HEAD, LANES), f32),
        ],
        scratch_shapes=[pltpu.VMEM((N_PAIRS, RW_HEAD, LANES), f32), pltpu.VMEM((1, RW_PAD), f32)],
        compiler_params=_cparams(("parallel", "arbitrary")),
    )(prw, shift0, s0_pair, wts["mu"], wts["w0"], wts["wd"], wts["a0"], wts["wa"], wts["wg"],
      wts["k_k"], wts["k_a"], wts["r_k"], wts["lnx_g"], wts["lnx_b"])


def _layer_norm(z, g, b):
    mu = jnp.mean(z, axis=-1, keepdims=True)
    d = z - mu
    var = jnp.mean(d * d, axis=-1, keepdims=True)
    return d * lax.rsqrt(var + LN_EPS) * g + b


N_SHARED_OUT = 4


def _outproj_kernel(*refs, aliased, n_steps):
    att_ref, rw_ref, x_ref, wo_ref, g_ref, b_ref, wrh_ref, wrl_ref, br_ref, cnt0_ref = refs[:10]
    h_ref, idx_ref, gate_ref, rank_ref, cnt_ref, carry = refs[10 + N_SHARED_OUT * aliased:]
    step = pl.program_id(0)

    @pl.when(step == 0)
    def _():
        carry[...] = cnt0_ref[...]

    mix = (jnp.dot(att_ref[...], wo_ref[0:ATT_W], preferred_element_type=f32)
           + jnp.dot(rw_ref[...], wo_ref[ATT_W:ATT_W + RW_W], preferred_element_type=f32))
    h = _layer_norm(DN_ALPHA * x_ref[...] + mix, g_ref[...], b_ref[...])
    h_ref[...] = h
    hh = h.astype(bf16)
    hl = (h - hh.astype(f32)).astype(bf16)
    logits = (jnp.dot(hh, wrh_ref[...], preferred_element_type=f32)
              + jnp.dot(hl, wrh_ref[...], preferred_element_type=f32)
              + jnp.dot(hh, wrl_ref[...], preferred_element_type=f32)) + br_ref[...]
    lane = lax.broadcasted_iota(jnp.int32, logits.shape, 1).astype(f32)
    vals, idxs = [], []
    cur = logits
    for _ in range(TOP_K):
        m = jnp.max(cur, axis=-1, keepdims=True)
        i = jnp.min(jnp.where(cur == m, lane, float(LANES)), axis=-1, keepdims=True)
        vals.append(m)
        idxs.append(i)
        cur = jnp.where(lane == i, -jnp.inf, cur)
    es = [jnp.exp(vv - vals[0]) for vv in vals]
    tot = es[0] + es[1] + es[2] + es[3]
    idx_ref[...] = jnp.concatenate(idxs, axis=-1).astype(jnp.int32)
    gate_ref[...] = jnp.concatenate([e / tot for e in es], axis=-1)
    tm = logits.shape[0]
    hits = [jnp.where(lane == i, 1.0, 0.0) for i in idxs]
    multi = hits[0] + hits[1] + hits[2] + hits[3]
    ti = lax.broadcasted_iota(jnp.int32, (tm, tm), 0)
    si = lax.broadcasted_iota(jnp.int32, (tm, tm), 1)
    before = jnp.where(si < ti, 1.0, 0.0).astype(bf16)
    base = carry[...] + jnp.dot(before, multi.astype(bf16), preferred_element_type=f32)
    ranks = [jnp.sum(hh_ * base, axis=-1, keepdims=True) for hh_ in hits]
    rank_ref[...] = jnp.concatenate(ranks, axis=-1).astype(jnp.int32)
    carry[...] = carry[...] + jnp.sum(multi, axis=0, keepdims=True)

    @pl.when(step == n_steps - 1)
    def _():
        cnt_ref[...] = carry[...]


def _outproj(att2, rw2, x2, wts, *, tm, row0, total_rows, prev=None, counts0=None):
    rows = x2.shape[0]
    assert rows % tm == 0 and row0 % tm == 0
    blk0 = row0 // tm
    aliased = prev is not None
    if counts0 is None:
        counts0 = jnp.zeros((1, LANES), f32)
    const = lambda i: (0, 0)
    in_specs = [
        pl.BlockSpec((tm, ATT_W), lambda i: (i, 0)),
        pl.BlockSpec((tm, RW_W), lambda i: (i, 0)),
        pl.BlockSpec((tm, D_MODEL), lambda i: (i, 0)),
        pl.BlockSpec((D_MODEL, D_MODEL), const, pipeline_mode=pl.Buffered(1)),
        pl.BlockSpec((1, D_MODEL), const),
        pl.BlockSpec((1, D_MODEL), const),
        pl.BlockSpec((D_MODEL, LANES), const),
        pl.BlockSpec((D_MODEL, LANES), const),
        pl.BlockSpec((1, LANES), const),
        pl.BlockSpec((1, LANES), const),
    ]
    args = [att2, rw2, x2, wts["w_out"], wts["ln1_g"], wts["ln1_b"], wts["wr_hi"], wts["wr_lo"],
            wts["b_router"], counts0]
    aliases = {}
    if aliased:
        in_specs += [pl.BlockSpec(memory_space=pl.ANY)] * N_SHARED_OUT
        args += list(prev)
        aliases = {len(args) - N_SHARED_OUT + k: k for k in range(N_SHARED_OUT)}
    n_steps = rows // tm
    return pl.pallas_call(
        functools.partial(_outproj_kernel, aliased=int(aliased), n_steps=n_steps),
        grid=(n_steps,),
        in_specs=in_specs,
        out_specs=[
            pl.BlockSpec((tm, D_MODEL), lambda i: (blk0 + i, 0)),
            pl.BlockSpec((tm, TOP_K), lambda i: (blk0 + i, 0)),
            pl.BlockSpec((tm, TOP_K), lambda i: (blk0 + i, 0)),
            pl.BlockSpec((tm, TOP_K), lambda i: (blk0 + i, 0)),
            pl.BlockSpec((1, LANES), const),
        ],
        out_shape=[
            jax.ShapeDtypeStruct((total_rows, D_MODEL), f32),
            jax.ShapeDtypeStruct((total_rows, TOP_K), jnp.int32),
            jax.ShapeDtypeStruct((total_rows, TOP_K), f32),
            jax.ShapeDtypeStruct((total_rows, TOP_K), jnp.int32),
            jax.ShapeDtypeStruct((1, LANES), f32),
        ],
        scratch_shapes=[pltpu.VMEM((1, LANES), f32)],
        input_output_aliases=aliases,
        compiler_params=_cparams(("arbitrary",)),
    )(*args)


MOE_SUB = 256
MOE_NSUB = 5
MOE_SUPER = MOE_SUB * MOE_NSUB
MOE_FF_TILE = 256
MOE_DOWN_N = 512
MOE_MAX_PAD = N_EXPERTS * (MOE_SUB - 1)
X_SUBL = D_MODEL // (2 * LANES)
HALF_D = D_MODEL // 2


def _pack_bf16_pairs(x):
    hi = lax.bitcast_convert_type(x[:, :HALF_D].astype(bf16).astype(f32), jnp.uint32)
    lo = lax.bitcast_convert_type(x[:, HALF_D:].astype(bf16).astype(f32), jnp.uint32)
    return hi | (lo >> 16)


def _unpack_bf16_pairs(u):
    hi = lax.bitcast_convert_type(u & jnp.uint32(0xFFFF0000), f32)
    lo = lax.bitcast_convert_type(u << 16, f32)
    return hi, lo


def _routing(top_idx, rank, counts_f):
    n = top_idx.shape[0]
    n_assign = n * TOP_K
    counts = counts_f[0, :N_EXPERTS].astype(jnp.int32)
    padded = (counts + MOE_SUPER - 1) // MOE_SUPER * MOE_SUPER
    pad_end = jnp.cumsum(padded)
    start = pad_end - padded
    experts = jnp.arange(N_EXPERTS, dtype=jnp.int32)

    def lookup(table, idx):
        return jnp.sum(jnp.where(idx[..., None] == experts, table, 0), axis=-1)

    def bucket(edges, x):
        return jnp.minimum(jnp.sum((edges <= x[..., None]).astype(jnp.int32), axis=-1),
                           N_EXPERTS - 1)

    dest = (lookup(start, top_idx) + rank).astype(jnp.int32)
    n_super = (n_assign + N_EXPERTS * (MOE_SUPER - 1) + MOE_SUPER - 1) // MOE_SUPER
    s_row0 = jnp.arange(n_super, dtype=jnp.int32) * MOE_SUPER
    super_e = bucket(pad_end, s_row0)
    rows_here = jnp.clip(lookup(counts, super_e) - (s_row0 - lookup(start, super_e)), 0, MOE_SUPER)
    rows_here = jnp.where(s_row0 < pad_end[-1], rows_here, 0)
    n_sub = ((rows_here + MOE_SUB - 1) // MOE_SUB).astype(jnp.int32)
    n_used = (pad_end[-1] // MOE_SUPER).astype(jnp.int32).reshape(1)
    n_pad_e = (counts + MOE_SUB - 1) // MOE_SUB * MOE_SUB - counts
    pad_cum = jnp.cumsum(n_pad_e)
    kk = jnp.arange(MOE_MAX_PAD, dtype=jnp.int32)
    pe = bucket(pad_cum, kk)
    pad_dest = (lookup(start + counts - (pad_cum - n_pad_e), pe) + kk).astype(jnp.int32)
    pad_dest = jnp.where(kk < pad_cum[-1], pad_dest, 0)
    n_pad = pad_cum[-1].astype(jnp.int32).reshape(1)
    return dest, super_e, n_sub, n_used, pad_dest, n_pad, n_super


def _tile_copy(src_ref, src_tok, dst_ref, dst_tok, subl, sem):
    s0 = pl.multiple_of(src_tok * subl, subl)
    d0 = pl.multiple_of(dst_tok * subl, subl)
    return pltpu.make_async_copy(src_ref.at[pl.ds(s0, subl)], dst_ref.at[pl.ds(d0, subl)], sem)


def _scatter_kernel(npad_ref, dest_ref, pad_ref, h_ref, xs_hbm, stage, zero, sems, pad_sem,
                    *, tm, n_steps):
    i = pl.program_id(0)
    slot = i % 2

    def drain(sl):
        for _ in range(TOP_K):
            pltpu.make_async_copy(stage.at[sl], stage.at[sl], sems.at[sl]).wait()

    @pl.when(i >= 2)
    def _():
        drain(slot)

    packed = _pack_bf16_pairs(h_ref[...])
    for l in range(X_SUBL):
        stage[slot, pl.ds(l, tm, stride=X_SUBL), :] = packed[:, LANES * l:LANES * (l + 1)]

    def issue(t, carry):
        for jx in range(TOP_K):
            _tile_copy(stage.at[slot], t, xs_hbm, dest_ref[0, 0, jx * tm + t], X_SUBL,
                       sems.at[slot]).start()
        return carry

    lax.fori_loop(0, tm, issue, 0)

    @pl.when(i == 0)
    def _():
        zero[...] = jnp.zeros_like(zero)

        def fill(k, carry):
            _tile_copy(zero, 0, xs_hbm, pad_ref[k], X_SUBL, pad_sem).start()
            return carry

        lax.fori_loop(0, npad_ref[0], fill, 0)

        def fill_wait(k, carry):
            _tile_copy(zero, 0, xs_hbm, 0, X_SUBL, pad_sem).wait()
            return carry

        lax.fori_loop(0, npad_ref[0], fill_wait, 0)

    @pl.when(i == n_steps - 1)
    def _():
        drain(slot)
        if n_steps > 1:
            drain(1 - slot)


def _scatter_rows(h_all, dest, pad_dest, n_pad, n_rows, tm):
    n = h_all.shape[0]
    assert n % tm == 0
    n_steps = n // tm
    dest_blk = dest.reshape(n_steps, tm, TOP_K).transpose(0, 2, 1).reshape(n_steps, 1, TOP_K * tm)
    kern = functools.partial(_scatter_kernel, tm=tm, n_steps=n_steps)
    return pl.pallas_call(
        kern,
        grid_spec=pltpu.PrefetchScalarGridSpec(
            num_scalar_prefetch=1,
            grid=(n_steps,),
            in_specs=[
                pl.BlockSpec((1, 1, TOP_K * tm), lambda i, npad: (i, 0, 0), memory_space=pltpu.SMEM),
                pl.BlockSpec(memory_space=pltpu.SMEM),
                pl.BlockSpec((tm, D_MODEL), lambda i, npad: (i, 0)),
            ],
            out_specs=pl.BlockSpec(memory_space=pl.ANY),
            scratch_shapes=[
                pltpu.VMEM((2, tm * X_SUBL, LANES), jnp.uint32),
                pltpu.VMEM((X_SUBL, LANES), jnp.uint32),
                pltpu.SemaphoreType.DMA((2,)),
                pltpu.SemaphoreType.DMA(()),
            ],
        ),
        out_shape=jax.ShapeDtypeStruct((n_rows * X_SUBL, LANES), jnp.uint32),
        compiler_params=_cparams(("arbitrary",)),
    )(n_pad, dest_blk, pad_dest, h_all)


def _expert_kernel(se_ref, nsub_ref, nused_ref, x_ref, wg_ref, wl_ref, bg_ref, bl_ref, wd_ref,
                   bd_ref, o_ref, xb_scr, acc_scr, *, nf):
    s = pl.program_id(0)
    j = pl.program_id(1)
    n_sub = nsub_ref[s]

    def step(m):
        rows = slice(0, m)

        @pl.when(j == 0)
        def _():
            for l in range(X_SUBL):
                hi, lo = _unpack_bf16_pairs(x_ref[pl.ds(l, m, stride=X_SUBL), :])
                xb_scr[rows, LANES * l:LANES * (l + 1)] = hi.astype(bf16)
                xb_scr[rows, HALF_D + LANES * l:HALF_D + LANES * (l + 1)] = lo.astype(bf16)
            acc_scr[rows, :] = jnp.broadcast_to(bd_ref[...], (m, D_MODEL))

        xb = xb_scr[rows, :]
        hg = jnp.dot(xb, wg_ref[...].astype(bf16), preferred_element_type=f32) + bg_ref[...]
        hl = jnp.dot(xb, wl_ref[...].astype(bf16), preferred_element_type=f32) + bl_ref[...]
        glu = jnp.minimum(hg, SWIGLU_LIMIT)
        lin = jnp.clip(hl, -SWIGLU_LIMIT, SWIGLU_LIMIT)
        act = (glu * (1.0 / (1.0 + jnp.exp(-SWIGLU_ALPHA * glu))) * (lin + 1.0)).astype(bf16)
        wd = wd_ref[...].astype(bf16)
        for n0 in range(0, D_MODEL, MOE_DOWN_N):
            acc_scr[rows, n0:n0 + MOE_DOWN_N] += jnp.dot(
                act, wd[:, n0:n0 + MOE_DOWN_N], preferred_element_type=f32)

        @pl.when(j == nf - 1)
        def _():
            packed = _pack_bf16_pairs(acc_scr[rows, :])
            for l in range(X_SUBL):
                o_ref[pl.ds(l, m, stride=X_SUBL), :] = packed[:, LANES * l:LANES * (l + 1)]

    for k in range(1, MOE_NSUB + 1):
        pl.when(n_sub == k)(functools.partial(step, k * MOE_SUB))


def _experts(xs, super_e, n_sub, n_used, w_up, b_up, w_down, b_down, n_super):
    tf = MOE_FF_TILE
    nf = D_FF // tf
    last = lambda s, nu: jnp.minimum(s, nu[0] - 1)
    b_up3 = b_up.reshape(N_EXPERTS, 1, 2 * D_FF)
    b_down3 = b_down.reshape(N_EXPERTS, 1, D_MODEL)
    e_of = lambda s, se, nu: se[last(s, nu)]
    return pl.pallas_call(
        functools.partial(_expert_kernel, nf=nf),
        grid_spec=pltpu.PrefetchScalarGridSpec(
            num_scalar_prefetch=3,
            grid=(n_super, nf),
            in_specs=[
                pl.BlockSpec((MOE_SUPER * X_SUBL, LANES), lambda s, j, se, ns, nu: (last(s, nu), 0)),
                pl.BlockSpec((None, D_MODEL, tf), lambda s, j, se, ns, nu: (e_of(s, se, nu), 0, j)),
                pl.BlockSpec((None, D_MODEL, tf),
                             lambda s, j, se, ns, nu: (e_of(s, se, nu), 0, nf + j)),
                pl.BlockSpec((None, 1, tf), lambda s, j, se, ns, nu: (e_of(s, se, nu), 0, j)),
                pl.BlockSpec((None, 1, tf), lambda s, j, se, ns, nu: (e_of(s, se, nu), 0, nf + j)),
                pl.BlockSpec((None, tf, D_MODEL), lambda s, j, se, ns, nu: (e_of(s, se, nu), j, 0)),
                pl.BlockSpec((None, 1, D_MODEL), lambda s, j, se, ns, nu: (e_of(s, se, nu), 0, 0)),
            ],
            out_specs=pl.BlockSpec((MOE_SUPER * X_SUBL, LANES),
                                   lambda s, j, se, ns, nu: (last(s, nu), 0)),
            scratch_shapes=[pltpu.VMEM((MOE_SUPER, D_MODEL), bf16),
                            pltpu.VMEM((MOE_SUPER, D_MODEL), f32)],
        ),
        out_shape=jax.ShapeDtypeStruct((n_super * MOE_SUPER * X_SUBL, LANES), jnp.uint32),
        compiler_params=_cparams(("arbitrary", "arbitrary")),
    )(super_e, n_sub, n_used, xs, w_up, w_up, b_up3, b_up3, w_down, b_down3)


def _combine_kernel(dest_ref, nxt_ref, gate_ref, h_ref, rows_hbm, g_ref, b_ref, yp_ref, ys_ref,
                    buf, sems, *, tm, n_first, n_steps):
    i = pl.program_id(0)
    slot = i % 2

    def fetch(idx_ref, sl):
        def body(t, carry):
            for jx in range(TOP_K):
                _tile_copy(rows_hbm, idx_ref[0, 0, jx * tm + t], buf.at[sl, jx], t, X_SUBL,
                           sems.at[sl]).start()
            return carry
        lax.fori_loop(0, tm, body, 0)

    @pl.when(i == 0)
    def _():
        fetch(dest_ref, slot)

    @pl.when(i + 1 < n_steps)
    def _():
        fetch(nxt_ref, 1 - slot)

    for jx in range(TOP_K):
        pltpu.make_async_copy(buf.at[slot, jx], buf.at[slot, jx], sems.at[slot]).wait()
    gate = gate_ref[...]
    cols_hi, cols_lo = [], []
    for l in range(X_SUBL):
        acc_hi = acc_lo = None
        for jx in range(TOP_K):
            hi, lo = _unpack_bf16_pairs(buf[slot, jx, pl.ds(l, tm, stride=X_SUBL), :])
            gj = gate[:, jx:jx + 1]
            acc_hi = gj * hi if acc_hi is None else acc_hi + gj * hi
            acc_lo = gj * lo if acc_lo is None else acc_lo + gj * lo
        cols_hi.append(acc_hi)
        cols_lo.append(acc_lo)
    y = jnp.concatenate(cols_hi + cols_lo, axis=-1)
    out = _layer_norm(DN_ALPHA * h_ref[...] + y, g_ref[...], b_ref[...])

    @pl.when(i < n_first)
    def _():
        yp_ref[...] = out

    @pl.when(i >= n_first)
    def _():
        ys_ref[...] = out


def _combine(rows_out, dest, gate, h_all, ln_g, ln_b, *, tm, n_first_rows):
    n = h_all.shape[0]
    assert n % tm == 0 and n_first_rows % tm == 0
    nblk = n // tm
    n_first = n_first_rows // tm
    dest_blk = dest.reshape(nblk, tm, TOP_K).transpose(0, 2, 1).reshape(nblk, 1, TOP_K * tm)
    kern = functools.partial(_combine_kernel, tm=tm, n_first=n_first, n_steps=nblk)
    const = lambda i: (0, 0)
    return pl.pallas_call(
        kern,
        grid=(nblk,),
        in_specs=[
            pl.BlockSpec((1, 1, TOP_K * tm), lambda i: (i, 0, 0), memory_space=pltpu.SMEM),
            pl.BlockSpec((1, 1, TOP_K * tm), lambda i: (jnp.minimum(i + 1, nblk - 1), 0, 0),
                         memory_space=pltpu.SMEM),
            pl.BlockSpec((tm, TOP_K), lambda i: (i, 0)),
            pl.BlockSpec((tm, D_MODEL), lambda i: (i, 0)),
            pl.BlockSpec(memory_space=pl.ANY),
            pl.BlockSpec((1, D_MODEL), const),
            pl.BlockSpec((1, D_MODEL), const),
        ],
        out_specs=[
            pl.BlockSpec((tm, D_MODEL), lambda i: (jnp.minimum(i, n_first - 1), 0)),
            pl.BlockSpec((tm, D_MODEL), lambda i: (jnp.maximum(i - n_first, 0), 0)),
        ],
        out_shape=[
            jax.ShapeDtypeStruct((n_first_rows, D_MODEL), f32),
            jax.ShapeDtypeStruct((n - n_first_rows, D_MODEL), f32),
        ],
        scratch_shapes=[pltpu.VMEM((2, TOP_K, tm * X_SUBL, LANES), jnp.uint32),
                        pltpu.SemaphoreType.DMA((2,))],
        compiler_params=_cparams(("arbitrary",)),
    )(dest_blk, dest_blk, gate, h_all, rows_out, ln_g, ln_b)


def _t5_bucket(rel):
    half = NUM_BUCKETS // 2
    exact = half // 2
    n = jnp.abs(rel)
    log_part = exact + (jnp.log(jnp.maximum(n, 1).astype(jnp.float32) / exact)
                        / math.log(MAX_DISTANCE / exact) * (half - exact)).astype(jnp.int32)
    log_part = jnp.minimum(log_part, half - 1)
    return jnp.where(rel > 0, half, 0) + jnp.where(n < exact, n, log_part)


def _band_bias(rel_bias):
    qi = jnp.arange(CHUNK)[:, None]
    km = jnp.arange(BAND)[None, :]
    bucket = _t5_bucket(km - WINDOW - qi)
    return jnp.transpose(rel_bias[bucket], (2, 0, 1)).astype(jnp.float32)


def _pad_cols(a, width):
    return jnp.pad(a, ((0, 0), (0, width - a.shape[-1])))


def _pair_state(s):
    b = s.shape[0]
    return s.reshape(b, N_PAIRS, 2, RW_HEAD, RW_HEAD).transpose(0, 1, 3, 2, 4).reshape(
        b, N_PAIRS, RW_HEAD, LANES)


def _unpair_state(s):
    b = s.shape[0]
    return s.reshape(b, N_PAIRS, RW_HEAD, 2, RW_HEAD).transpose(0, 1, 3, 2, 4).reshape(
        b, RW_HEADS, RW_HEAD, RW_HEAD)


def _mix_group(x, k_hist, v_hist, hist_valid, wkv0, shift0, bias, sinks3, w_in_pad, rw_wts,
               *, in_tm, attn_nc):
    b, t, _ = x.shape
    q, kv, prw = _inproj(x.reshape(b * t, D_MODEL), w_in_pad, in_tm)
    tp = -(-t // (attn_nc * CHUNK)) * (attn_nc * CHUNK)
    q4 = q.reshape(ATT_HEADS, b, t, HEAD_DIM)
    kv3 = kv.reshape(b, t, 2 * KV_W)
    prw3 = prw.reshape(b, t, RW_PAD)
    if tp != t:
        q4 = jnp.pad(q4, ((0, 0), (0, 0), (0, tp - t), (0, 0)))
        prw3 = jnp.pad(prw3, ((0, 0), (0, tp - t), (0, 0)))
    hist = jnp.concatenate([k_hist.reshape(b, WINDOW, KV_W), v_hist.reshape(b, WINDOW, KV_W)], axis=-1)
    kvfull = jnp.concatenate([hist, kv3, jnp.zeros((b, tp - t, 2 * KV_W), f32)], axis=1)
    att = _attention(q4, kvfull, bias, sinks3, nc=attn_nc, t_valid=t, hist_valid=hist_valid)
    rw, s_fin = _rwkv(prw3, _pad_cols(shift0.reshape(b, RW_PROJ), RW_PAD).reshape(b, 1, RW_PAD),
                      _pair_state(wkv0.astype(f32)), rw_wts, t_valid=t)
    new_kv = kvfull[:, t:t + WINDOW]
    new_k = new_kv[..., :KV_W].reshape(b, WINDOW, ATT_KV_HEADS, HEAD_DIM)
    new_v = new_kv[..., KV_W:].reshape(b, WINDOW, ATT_KV_HEADS, HEAD_DIM)
    shift = prw3[:, t - 1:t, :RW_PROJ]
    return (att[:, :t].reshape(b * t, ATT_W), rw[:, :t].reshape(b * t, RW_W),
            new_k, new_v, _unpair_state(s_fin), shift)


SCATTER_TM = 128
COMBINE_TM = 128


def kernel(x_prompt, x_sample, cache_k, cache_v, state_wkv, state_shift, rel_bias, w_in, attn_sinks, rw_mu, rw_w0, rw_decay_up, rw_a0, rw_iclr_up, rw_gate_up, rw_k_k, rw_k_a, rw_r_k, rw_lnx_g, rw_lnx_b, w_out, ln1_g, ln1_b, w_router, b_router, w_up, b_up, w_down, b_down, ln2_g, ln2_b):
    assert w_in.shape[0] == DEPTH == 1
    l = 0
    bp, tp_, _ = x_prompt.shape
    bs, ts, _ = x_sample.shape
    bias = _band_bias(rel_bias)
    sinks3 = attn_sinks[l].astype(f32).reshape(ATT_HEADS, 1, 1)

    w_in_pad = _pad_cols(w_in[l], IN_PAD).astype(bf16)

    def lora_rows(w, row0):
        return jnp.zeros((LORA_PAD, RW_W), f32).at[row0:row0 + w.shape[0]].set(w).astype(bf16)

    rw_wts = {
        "mu": _pad_cols(rw_mu[l].reshape(1, RW_PROJ), RW_PAD),
        "w0": rw_w0[l].reshape(1, RW_W),
        "wd": lora_rows(rw_decay_up[l], 0),
        "a0": rw_a0[l].reshape(1, RW_W),
        "wa": lora_rows(rw_iclr_up[l], DECAY_LORA),
        "wg": lora_rows(rw_gate_up[l], DECAY_LORA + ICLR_LORA),
        "k_k": rw_k_k[l].reshape(1, RW_W),
        "k_a": rw_k_a[l].reshape(1, RW_W),
        "r_k": rw_r_k[l].reshape(1, RW_W),
        "lnx_g": rw_lnx_g[l].reshape(1, RW_W),
        "lnx_b": rw_lnx_b[l].reshape(1, RW_W),
    }
    wr = _pad_cols(w_router[l], LANES)
    wr_hi = wr.astype(bf16)
    op_wts = {
        "w_out": w_out[l].astype(bf16),
        "ln1_g": ln1_g[l].reshape(1, D_MODEL),
        "ln1_b": ln1_b[l].reshape(1, D_MODEL),
        "wr_hi": wr_hi,
        "wr_lo": (wr - wr_hi.astype(f32)).astype(bf16),
        "b_router": jnp.concatenate([b_router[l].astype(f32),
                                     jnp.full((LANES - N_EXPERTS,), NEG_BIG, f32)]).reshape(1, LANES),
    }

    zero_kv = jnp.zeros((bp, WINDOW, ATT_KV_HEADS, HEAD_DIM), f32)
    att_p, rwo_p, k1, v1, w1, s1 = _mix_group(
        x_prompt, zero_kv, zero_kv, False, jnp.zeros((bp, RW_HEADS, RW_HEAD, RW_HEAD), f32),
        jnp.zeros((bp, 1, RW_PROJ), f32), bias, sinks3, w_in_pad, rw_wts,
        in_tm=min(256, bp * tp_), attn_nc=min(8, -(-tp_ // CHUNK)))
    att_s, rwo_s, k2, v2, w2, s2 = _mix_group(
        x_sample, cache_k[l], cache_v[l], True, state_wkv[l], state_shift[l], bias, sinks3,
        w_in_pad, rw_wts, in_tm=min(256, bs * ts), attn_nc=1)

    n_p, n_s = bp * tp_, bs * ts
    n_all = n_p + n_s
    tm_p, tm_s = min(256, n_p), min(128, n_s)
    *outs, counts_p = _outproj(att_p, rwo_p, x_prompt.reshape(n_p, D_MODEL), op_wts,
                               tm=tm_p, row0=0, total_rows=n_all)
    h_all, top_idx, gate, rank, counts = _outproj(
        att_s, rwo_s, x_sample.reshape(n_s, D_MODEL), op_wts,
        tm=tm_s, row0=n_p, total_rows=n_all, prev=outs, counts0=counts_p)

    dest, super_e, n_sub, n_used, pad_dest, n_pad, n_super = _routing(top_idx, rank, counts)
    xs = _scatter_rows(h_all, dest, pad_dest, n_pad, n_super * MOE_SUPER, min(SCATTER_TM, n_s))
    rows_out = _experts(xs, super_e, n_sub, n_used, w_up[l], b_up[l], w_down[l], b_down[l],
                        n_super)
    y_p, y_s = _combine(rows_out, dest, gate, h_all, ln2_g[l].reshape(1, D_MODEL),
                        ln2_b[l].reshape(1, D_MODEL), tm=min(COMBINE_TM, n_s), n_first_rows=n_p)

    return (y_p.reshape(bp, tp_, D_MODEL), y_s.reshape(bs, ts, D_MODEL),
            k1[None], v1[None], w1[None], s1[None], k2[None], v2[None], w2[None], s2[None])
```

```python
import functools
import math

import jax
import jax.numpy as jnp
from jax import lax
from jax.experimental import pallas as pl
from jax.experimental.pallas import tpu as pltpu

f32 = jnp.float32
bf16 = jnp.bfloat16

D_MODEL = 2048
CHUNK = 64
ATT_HEADS = 16
ATT_KV_HEADS = 2
HEAD_DIM = 64
ATT_GROUP = ATT_HEADS // ATT_KV_HEADS
ATT_W = ATT_HEADS * HEAD_DIM
KV_W = ATT_KV_HEADS * HEAD_DIM
ATT_PROJ = ATT_W + 2 * KV_W
WINDOW = 128
BAND = WINDOW + CHUNK
NUM_BUCKETS = 32
MAX_DISTANCE = 128
RW_HEAD = 64
RW_W = 1024
RW_HEADS = RW_W // RW_HEAD
DECAY_LORA = 96
ICLR_LORA = 96
GATE_LORA = 128
RW_PROJ = 3 * RW_W + DECAY_LORA + ICLR_LORA + GATE_LORA
GN_EPS = 64e-5
LN_EPS = 1e-5
N_EXPERTS = 32
TOP_K = 4
D_FF = D_MODEL
SWIGLU_LIMIT = 7.0
SWIGLU_ALPHA = 1.702
DEPTH = 1
DN_ALPHA = (2 * DEPTH) ** 0.25

LANES = 128
VMEM_LIMIT = 56 * 1024 * 1024

LORA_W = DECAY_LORA + ICLR_LORA + GATE_LORA
LORA_PAD = -(-LORA_W // LANES) * LANES
RW_PAD = 3 * RW_W + LORA_PAD
IN_PAD = ATT_PROJ + RW_PAD
N_PAIRS = RW_HEADS // 2
NEG_BIG = -1e30


def _cparams(sem):
    return pltpu.CompilerParams(dimension_semantics=sem, vmem_limit_bytes=VMEM_LIMIT)


def _inproj_kernel(x_ref, w_ref, q_ref, kv_ref, rw_ref):
    acc = jnp.dot(x_ref[...].astype(bf16), w_ref[...], preferred_element_type=f32)
    scale = HEAD_DIM ** -0.5
    for h in range(ATT_HEADS):
        q_ref[h] = (acc[:, h * HEAD_DIM:(h + 1) * HEAD_DIM] * scale).astype(bf16)
    kv_ref[...] = acc[:, ATT_W:ATT_PROJ]
    rw_ref[...] = acc[:, ATT_PROJ:IN_PAD]


def _inproj(x2, w_pad, tm):
    rows = x2.shape[0]
    assert rows % tm == 0
    return pl.pallas_call(
        _inproj_kernel,
        grid=(rows // tm,),
        in_specs=[
            pl.BlockSpec((tm, D_MODEL), lambda i: (i, 0)),
            pl.BlockSpec((D_MODEL, IN_PAD), lambda i: (0, 0), pipeline_mode=pl.Buffered(1)),
        ],
        out_specs=[
            pl.BlockSpec((ATT_HEADS, tm, HEAD_DIM), lambda i: (0, i, 0)),
            pl.BlockSpec((tm, 2 * KV_W), lambda i: (i, 0)),
            pl.BlockSpec((tm, RW_PAD), lambda i: (i, 0)),
        ],
        out_shape=[
            jax.ShapeDtypeStruct((ATT_HEADS, rows, HEAD_DIM), bf16),
            jax.ShapeDtypeStruct((rows, 2 * KV_W), f32),
            jax.ShapeDtypeStruct((rows, RW_PAD), f32),
        ],
        compiler_params=_cparams(("parallel",)),
    )(x2, w_pad)


ATT_UNROLL = 4


def _attn_kernel(q_ref, kvm_ref, kva_ref, kvb_ref, bias_ref, sink_ref, o_ref, kvbuf,
                 *, nc, t_valid, hist_valid):
    j = pl.program_id(1)
    kvbuf[0:nc * CHUNK] = kvm_ref[...].astype(bf16)
    kvbuf[nc * CHUNK:(nc + 1) * CHUNK] = kva_ref[...].astype(bf16)
    kvbuf[(nc + 1) * CHUNK:(nc + 2) * CHUNK] = kvb_ref[...].astype(bf16)
    m_idx = lax.broadcasted_iota(jnp.int32, (1, 1, BAND), 2)
    for c0 in range(0, nc, ATT_UNROLL):
        items = [(c, g) for c in range(c0, min(c0 + ATT_UNROLL, nc)) for g in range(ATT_KV_HEADS)]
        bands = {c: kvbuf[c * CHUNK:c * CHUNK + BAND, :] for c, _ in items}
        valid = {}
        for c in bands:
            idx = (j * nc + c) * CHUNK + m_idx
            v = idx - WINDOW < t_valid
            valid[c] = v if hist_valid else jnp.logical_and(v, idx >= WINDOW)
        s = [lax.dot_general(
                q_ref[g * ATT_GROUP:(g + 1) * ATT_GROUP, c * CHUNK:(c + 1) * CHUNK, :].reshape(
                    ATT_GROUP * CHUNK, HEAD_DIM),
                bands[c][:, g * HEAD_DIM:(g + 1) * HEAD_DIM],
                (((1,), (1,)), ((), ())), preferred_element_type=f32) for c, g in items]
        s = [jnp.where(valid[c], s[i].reshape(ATT_GROUP, CHUNK, BAND)
                       + bias_ref[g * ATT_GROUP:(g + 1) * ATT_GROUP], NEG_BIG)
             for i, (c, g) in enumerate(items)]
        sk = [sink_ref[g * ATT_GROUP:(g + 1) * ATT_GROUP] for _, g in items]
        m = [jnp.maximum(jnp.max(s[i], axis=-1, keepdims=True), sk[i]) for i in range(len(items))]
        p = [jnp.exp(s[i] - m[i]) for i in range(len(items))]
        den = [jnp.sum(p[i], axis=-1, keepdims=True) + jnp.exp(sk[i] - m[i])
               for i in range(len(items))]
        o = [jnp.dot(p[i].reshape(ATT_GROUP * CHUNK, BAND).astype(bf16),
                     bands[c][:, KV_W + g * HEAD_DIM:KV_W + (g + 1) * HEAD_DIM],
                     preferred_element_type=f32).reshape(ATT_GROUP, CHUNK, HEAD_DIM) / den[i]
             for i, (c, g) in enumerate(items)]
        for c in bands:
            heads = [o[i][h] for i, (ci, _) in enumerate(items) if ci == c for h in range(ATT_GROUP)]
            o_ref[c * CHUNK:(c + 1) * CHUNK, :] = jnp.concatenate(heads, axis=-1).astype(bf16)


def _attention(q4, kvfull, bias, sinks3, *, nc, t_valid, hist_valid):
    _, b, tp, _ = q4.shape
    assert tp % (nc * CHUNK) == 0 and kvfull.shape[1] == tp + WINDOW
    nblk = tp // (nc * CHUNK)
    kern = functools.partial(_attn_kernel, nc=nc, t_valid=t_valid, hist_valid=hist_valid)
    return pl.pallas_call(
        kern,
        grid=(b, nblk),
        in_specs=[
            pl.BlockSpec((ATT_HEADS, None, nc * CHUNK, HEAD_DIM), lambda bi, j: (0, bi, j, 0)),
            pl.BlockSpec((None, nc * CHUNK, 2 * KV_W), lambda bi, j: (bi, j, 0)),
            pl.BlockSpec((None, CHUNK, 2 * KV_W), lambda bi, j: (bi, (j + 1) * nc, 0)),
            pl.BlockSpec((None, CHUNK, 2 * KV_W), lambda bi, j: (bi, (j + 1) * nc + 1, 0)),
            pl.BlockSpec((ATT_HEADS, CHUNK, BAND), lambda bi, j: (0, 0, 0)),
            pl.BlockSpec((ATT_HEADS, 1, 1), lambda bi, j: (0, 0, 0)),
        ],
        out_specs=pl.BlockSpec((None, nc * CHUNK, ATT_W), lambda bi, j: (bi, j, 0)),
        out_shape=jax.ShapeDtypeStruct((b, tp, ATT_W), bf16),
        scratch_shapes=[pltpu.VMEM(((nc + 2) * CHUNK, 2 * KV_W), bf16)],
        compiler_params=_cparams(("parallel", "parallel")),
    )(q4, kvfull, kvfull, kvfull, bias, sinks3)


def _rwkv_kernel(p_ref, shift0_ref, s0_ref, mu_ref, w0_ref, wd_ref, a0_ref, wa_ref, wg_ref,
                 kk_ref, ka_ref, rk_ref, lng_ref, lnb_ref, o_ref, sfin_ref, s_scr, last_scr,
                 *, t_valid, n_steps, nch):
    c = pl.program_id(1)
    L = CHUNK
    R = nch * L

    @pl.when(c == 0)
    def _():
        s_scr[...] = s0_ref[...]
        last_scr[...] = shift0_ref[...]

    p = p_ref[...]
    row = lax.broadcasted_iota(jnp.int32, (R, 1), 0)
    shifted = jnp.where(row == 0, last_scr[...], pltpu.roll(p, 1, axis=0))
    last_scr[...] = p[R - 1:R, :]
    xm = p + (shifted - p) * mu_ref[...]
    r = xm[:, 0:RW_W]
    k = xm[:, RW_W:2 * RW_W]
    v = xm[:, 2 * RW_W:3 * RW_W]
    tail = xm[:, 3 * RW_W:RW_PAD]

    def sigmoid(z):
        return 1.0 / (1.0 + jnp.exp(-z))

    w_log = w0_ref[...] + jnp.dot(jnp.tanh(tail).astype(bf16), wd_ref[...],
                                  preferred_element_type=f32)
    z = -w_log
    softplus = jnp.maximum(z, 0.0) + jnp.log(1.0 + jnp.exp(-jnp.abs(z)))
    ld = -jnp.exp(-softplus - 0.5)
    a = sigmoid(a0_ref[...] + jnp.dot(tail.astype(bf16), wa_ref[...], preferred_element_type=f32))
    g = jnp.dot(sigmoid(tail).astype(bf16), wg_ref[...], preferred_element_type=f32)

    lane = lax.broadcasted_iota(jnp.int32, (1, LANES), 1)
    lo_half = lane < RW_HEAD
    rr = lax.broadcasted_iota(jnp.int32, (LANES, LANES), 0)
    cc = lax.broadcasted_iota(jnp.int32, (LANES, LANES), 1)
    same_head = (rr // RW_HEAD) == (cc // RW_HEAD)
    ones_bd = jnp.where(same_head, 1.0, 0.0).astype(bf16)

    def seg_sums(xs):
        n = len(xs)
        x = jnp.concatenate(xs, axis=0) if n > 1 else xs[0]
        hi = x.astype(bf16)
        lo = (x - hi.astype(f32)).astype(bf16)
        both = jnp.concatenate([hi, lo], axis=0)
        m = 2 * n * R
        tiles = jnp.concatenate([both[:, LANES * t:LANES * (t + 1)] for t in range(N_PAIRS)], axis=0)
        res = jnp.dot(tiles, ones_bd, preferred_element_type=f32)
        y = jnp.concatenate([res[m * t:m * (t + 1)] for t in range(N_PAIRS)], axis=1)
        y = y[:n * R] + y[n * R:]
        return [y[i * R:(i + 1) * R] for i in range(n)]

    kk = k * kk_ref[...]
    k_mod = k * (1.0 + (a - 1.0) * ka_ref[...])
    nrm2, bonus_s = seg_sums([kk * kk, r * k_mod * rk_ref[...]])
    kk = kk / jnp.maximum(jnp.sqrt(nrm2), 1e-12)
    b = kk * a

    if t_valid % R != 0:
        live = (c * R + row) < t_valid
        ld = jnp.where(live, ld, 0.0)
        b = jnp.where(live, b, 0.0)
        k_mod = jnp.where(live, k_mod, 0.0)

    h1 = ld.astype(bf16)
    r1 = ld - h1.astype(f32)
    h2 = r1.astype(bf16)
    h3 = (r1 - h2.astype(f32)).astype(bf16)
    ti = lax.broadcasted_iota(jnp.int32, (L, 3 * L), 0)
    si = lax.broadcasted_iota(jnp.int32, (L, 3 * L), 1) % L
    tri3 = jnp.where(si <= ti, 1.0, 0.0).astype(bf16)
    rs = [slice(L * ch, L * (ch + 1)) for ch in range(nch)]
    cums = [jnp.dot(tri3, jnp.concatenate([h1[rs[ch]], h2[rs[ch]], h3[rs[ch]]], axis=0),
                    preferred_element_type=f32) for ch in range(nch)]
    cum = jnp.concatenate(cums, axis=0) if nch > 1 else cums[0]
    cum_ls = [cums[ch][L - 1:L, :] for ch in range(nch)]
    cum_l = (jnp.concatenate([jnp.broadcast_to(x, (L, RW_W)) for x in cum_ls], axis=0)
             if nch > 1 else cum_ls[0])
    g_l = [jnp.exp(x) for x in cum_ls]
    g_inv = jnp.exp(-cum)
    g_rest = jnp.exp(cum_l - cum)
    kq = (kk * jnp.exp(cum - ld)).astype(bf16)
    rq_f = r * jnp.exp(cum)
    rq = rq_f.astype(bf16)
    bt = (b * g_inv).astype(bf16)
    kt = (k_mod * g_inv).astype(bf16)
    bh = (b * g_rest).astype(bf16)
    kh = (k_mod * g_rest).astype(bf16)
    vb = v.astype(bf16)

    def bd(x):
        zero = jnp.zeros_like(x)
        return jnp.concatenate([jnp.where(lo_half, x, zero), jnp.where(lo_half, zero, x)], axis=0)

    def mm(x, y):
        return jnp.dot(x, y, preferred_element_type=f32)

    def mm_nt(x, y):
        return lax.dot_general(x, y, (((1,), (1,)), ((), ())), preferred_element_type=f32)

    def mm_tn(x, y):
        return lax.dot_general(x, y, (((0,), (0,)), ((), ())), preferred_element_type=f32)

    tt = lax.broadcasted_iota(jnp.int32, (L, LANES), 0)
    ss = lax.broadcasted_iota(jnp.int32, (L, LANES), 1) % RW_HEAD
    strict = ss < tt
    incl = ss <= tt
    eye_pair = jnp.where(ss == tt, 1.0, 0.0).astype(f32)

    items = [(ch, t) for ch in range(nch) for t in range(N_PAIRS)]
    I = range(len(items))
    sl = [slice(LANES * t, LANES * (t + 1)) for t in range(N_PAIRS)]

    def tile(x, i):
        ch, t = items[i]
        return x[rs[ch], sl[t]]

    a_all = [mm_nt(jnp.concatenate([tile(kq, i), tile(rq, i)], axis=0),
                   jnp.concatenate([bd(tile(bt, i)), bd(tile(kt, i))], axis=0)) for i in I]
    a_bk = [jnp.where(strict, a_all[i][:L, :LANES], 0.0) for i in I]
    a_kk = [jnp.where(strict, a_all[i][:L, LANES:], 0.0).astype(bf16) for i in I]
    a_rb = [jnp.where(incl, a_all[i][L:, :LANES], 0.0).astype(bf16) for i in I]
    a_rk = [jnp.where(incl, a_all[i][L:, LANES:], 0.0).astype(bf16) for i in I]
    bd_v = [bd(tile(vb, i)) for i in I]
    akv = [mm(a_kk[i], bd_v[i]).astype(bf16) for i in I]
    w_inv = [eye_pair - a_bk[i] for i in I]
    pw = [a_bk[i].astype(bf16) for i in I]
    pw_bd = [bd(pw[i]) for i in I]
    for _ in range(5):
        pw = [mm(pw[i], pw_bd[i]).astype(bf16) for i in I]
        pw_bd = [bd(pw[i]) for i in I]
        w_inv = [w_inv[i] + mm(w_inv[i].astype(bf16), pw_bd[i]) for i in I]
    qu = [mm(w_inv[i].astype(bf16), jnp.concatenate([bd(tile(kq, i)), bd(akv[i])], axis=1))
          for i in I]
    q_m = [qu[i][:, :LANES].astype(bf16) for i in I]
    u_m = [qu[i][:, LANES:].astype(bf16) for i in I]
    m_full = [mm_tn(q_m[i], tile(bh, i)) for i in I]
    neg_m = [jnp.where(same_head, -m_full[i], 0.0).astype(bf16) for i in I]
    c_full = [mm_tn(jnp.concatenate([tile(vb, i), u_m[i]], axis=0),
                    jnp.concatenate([tile(kh, i), -tile(bh, i)], axis=0)) for i in I]
    go = [mm(a_rb[i], jnp.concatenate([bd(q_m[i]), bd(u_m[i])], axis=1)) for i in I]
    o_rk = [mm(a_rk[i], bd_v[i]) for i in I]
    g_m = [(tile(rq_f, i) - go[i][:, :LANES]).astype(bf16) for i in I]
    o_in = [o_rk[i] - go[i][:, LANES:] for i in I]
    c_pair = [jnp.where(lo_half, c_full[i][:RW_HEAD], c_full[i][RW_HEAD:]) for i in I]
    s_cur = [s_scr[t] for t in range(N_PAIRS)]
    o_rows = []
    for ch in range(nch):
        ii = [ch * N_PAIRS + t for t in range(N_PAIRS)]
        s_b = [s.astype(bf16) for s in s_cur]
        o_rows.append(jnp.concatenate(
            [mm_nt(g_m[i], bd(s_b[t])) + o_in[i] for t, i in enumerate(ii)], axis=1))
        s_upd = [mm(s_b[t], neg_m[i]) for t, i in enumerate(ii)]
        s_cur = [s_cur[t] * g_l[ch][:, sl[t]] + s_upd[t] + c_pair[i] for t, i in enumerate(ii)]
    for t in range(N_PAIRS):
        s_scr[t] = s_cur[t]

    o = jnp.concatenate(o_rows, axis=0) if nch > 1 else o_rows[0]
    (o_sum,) = seg_sums([o])
    d = o - o_sum * (1.0 / RW_HEAD)
    (d2,) = seg_sums([d * d])
    on = d * lax.rsqrt(d2 * (1.0 / RW_HEAD) + GN_EPS) * lng_ref[...] + lnb_ref[...]
    o_ref[...] = ((on + bonus_s * v) * g).astype(bf16)

    @pl.when(c == n_steps - 1)
    def _():
        sfin_ref[...] = s_scr[...]


RWKV_CHUNKS_PER_STEP = 2


def _rwkv(prw, shift0, s0_pair, wts, *, t_valid):
    b, tp, _ = prw.shape
    nch = RWKV_CHUNKS_PER_STEP if tp % (RWKV_CHUNKS_PER_STEP * CHUNK) == 0 else 1
    rows = nch * CHUNK
    n_steps = tp // rows
    kern = functools.partial(_rwkv_kernel, t_valid=t_valid, n_steps=n_steps, nch=nch)
    const2 = lambda bi, c: (0, 0)
    row_spec = pl.BlockSpec((1, RW_W), const2)
    return pl.pallas_call(
        kern,
        grid=(b, n_steps),
        in_specs=[
            pl.BlockSpec((None, rows, RW_PAD), lambda bi, c: (bi, c, 0)),
            pl.BlockSpec((None, 1, RW_PAD), lambda bi, c: (bi, 0, 0)),
            pl.BlockSpec((None, N_PAIRS, RW_HEAD, LANES), lambda bi, c: (bi, 0, 0, 0)),
            pl.BlockSpec((1, RW_PAD), const2),
            row_spec,
            pl.BlockSpec((LORA_PAD, RW_W), const2),
            row_spec,
            pl.BlockSpec((LORA_PAD, RW_W), const2),
            pl.BlockSpec((LORA_PAD, RW_W), const2),
            row_spec, row_spec, row_spec, row_spec, row_spec,
        ],
        out_specs=[
            pl.BlockSpec((None, rows, RW_W), lambda bi, c: (bi, c, 0)),
            pl.BlockSpec((None, N_PAIRS, RW_HEAD, LANES), lambda bi, c: (bi, 0, 0, 0)),
        ],
        out_shape=[
            jax.ShapeDtypeStruct((b, tp, RW_W), bf16),
            jax.ShapeDtypeStruct((b, N_PAIRS, RW_HEAD, LANES), f32),
        ],
        scratch_shapes=[pltpu.VMEM((N_PAIRS, RW_HEAD, LANES), f32), pltpu.VMEM((1, RW_PAD), f32)],
        compiler_params=_cparams(("parallel", "arbitrary")),
    )(prw, shift0, s0_pair, wts["mu"], wts["w0"], wts["wd"], wts["a0"], wts["wa"], wts["wg"],
      wts["k_k"], wts["k_a"], wts["r_k"], wts["lnx_g"], wts["lnx_b"])


def _layer_norm(z, g, b):
    mu = jnp.mean(z, axis=-1, keepdims=True)
    d = z - mu
    var = jnp.mean(d * d, axis=-1, keepdims=True)
    return d * lax.rsqrt(var + LN_EPS) * g + b


N_SHARED_OUT = 4
OUTPROJ_PARTS = 2


def _outproj_kernel(*refs, aliased, n_steps):
    att_ref, rw_ref, x_ref, wo_ref, g_ref, b_ref, wrh_ref, wrl_ref, br_ref, cnt0_ref = refs[:10]
    h_ref, idx_ref, gate_ref, rank_ref, cnt_ref, carry = refs[10 + N_SHARED_OUT * aliased:]
    step = pl.program_id(0)

    @pl.when(step == 0)
    def _():
        carry[...] = cnt0_ref[...]

    tm = x_ref.shape[0]
    n_part = OUTPROJ_PARTS if tm % (OUTPROJ_PARTS * 8) == 0 else 1
    pm = tm // n_part
    parts = [slice(q * pm, (q + 1) * pm) for q in range(n_part)]
    mix = [jnp.dot(att_ref[r, :], wo_ref[0:ATT_W], preferred_element_type=f32)
           + jnp.dot(rw_ref[r, :], wo_ref[ATT_W:ATT_W + RW_W], preferred_element_type=f32)
           for r in parts]
    h = [_layer_norm(DN_ALPHA * x_ref[r, :] + mix[q], g_ref[...], b_ref[...])
         for q, r in enumerate(parts)]
    for q, r in enumerate(parts):
        h_ref[r, :] = h[q]
    hh = [x.astype(bf16) for x in h]
    hl = [(h[q] - hh[q].astype(f32)).astype(bf16) for q in range(n_part)]
    logits = [(jnp.dot(hh[q], wrh_ref[...], preferred_element_type=f32)
               + jnp.dot(hl[q], wrh_ref[...], preferred_element_type=f32)
               + jnp.dot(hh[q], wrl_ref[...], preferred_element_type=f32)) + br_ref[...]
              for q in range(n_part)]
    lane = lax.broadcasted_iota(jnp.int32, (pm, LANES), 1).astype(f32)
    ti = lax.broadcasted_iota(jnp.int32, (pm, pm), 0)
    si = lax.broadcasted_iota(jnp.int32, (pm, pm), 1)
    before = jnp.where(si < ti, 1.0, 0.0).astype(bf16)
    counts = carry[...]
    for q, r in enumerate(parts):
        vals, idxs = [], []
        cur = logits[q]
        for _ in range(TOP_K):
            m = jnp.max(cur, axis=-1, keepdims=True)
            i = jnp.min(jnp.where(cur == m, lane, float(LANES)), axis=-1, keepdims=True)
            vals.append(m)
            idxs.append(i)
            cur = jnp.where(lane == i, -jnp.inf, cur)
        es = [jnp.exp(vv - vals[0]) for vv in vals]
        tot = es[0] + es[1] + es[2] + es[3]
        idx_ref[r, :] = jnp.concatenate(idxs, axis=-1).astype(jnp.int32)
        gate_ref[r, :] = jnp.concatenate([e / tot for e in es], axis=-1)
        hits = [jnp.where(lane == i, 1.0, 0.0) for i in idxs]
        multi = hits[0] + hits[1] + hits[2] + hits[3]
        base = counts + jnp.dot(before, multi.astype(bf16), preferred_element_type=f32)
        ranks = [jnp.sum(hh_ * base, axis=-1, keepdims=True) for hh_ in hits]
        rank_ref[r, :] = jnp.concatenate(ranks, axis=-1).astype(jnp.int32)
        counts = counts + jnp.sum(multi, axis=0, keepdims=True)
    carry[...] = counts

    @pl.when(step == n_steps - 1)
    def _():
        cnt_ref[...] = carry[...]


def _outproj(att2, rw2, x2, wts, *, tm, row0, total_rows, prev=None, counts0=None):
    rows = x2.shape[0]
    assert rows % tm == 0 and row0 % tm == 0
    blk0 = row0 // tm
    aliased = prev is not None
    if counts0 is None:
        counts0 = jnp.zeros((1, LANES), f32)
    const = lambda i: (0, 0)
    in_specs = [
        pl.BlockSpec((tm, ATT_W), lambda i: (i, 0)),
        pl.BlockSpec((tm, RW_W), lambda i: (i, 0)),
        pl.BlockSpec((tm, D_MODEL), lambda i: (i, 0)),
        pl.BlockSpec((D_MODEL, D_MODEL), const, pipeline_mode=pl.Buffered(1)),
        pl.BlockSpec((1, D_MODEL), const),
        pl.BlockSpec((1, D_MODEL), const),
        pl.BlockSpec((D_MODEL, LANES), const),
        pl.BlockSpec((D_MODEL, LANES), const),
        pl.BlockSpec((1, LANES), const),
        pl.BlockSpec((1, LANES), const),
    ]
    args = [att2, rw2, x2, wts["w_out"], wts["ln1_g"], wts["ln1_b"], wts["wr_hi"], wts["wr_lo"],
            wts["b_router"], counts0]
    aliases = {}
    if aliased:
        in_specs += [pl.BlockSpec(memory_space=pl.ANY)] * N_SHARED_OUT
        args += list(prev)
        aliases = {len(args) - N_SHARED_OUT + k: k for k in range(N_SHARED_OUT)}
    n_steps = rows // tm
    return pl.pallas_call(
        functools.partial(_outproj_kernel, aliased=int(aliased), n_steps=n_steps),
        grid=(n_steps,),
        in_specs=in_specs,
        out_specs=[
            pl.BlockSpec((tm, D_MODEL), lambda i: (blk0 + i, 0)),
            pl.BlockSpec((tm, TOP_K), lambda i: (blk0 + i, 0)),
            pl.BlockSpec((tm, TOP_K), lambda i: (blk0 + i, 0)),
            pl.BlockSpec((tm, TOP_K), lambda i: (blk0 + i, 0)),
            pl.BlockSpec((1, LANES), const),
        ],
        out_shape=[
            jax.ShapeDtypeStruct((total_rows, D_MODEL), f32),
            jax.ShapeDtypeStruct((total_rows, TOP_K), jnp.int32),
            jax.ShapeDtypeStruct((total_rows, TOP_K), f32),
            jax.ShapeDtypeStruct((total_rows, TOP_K), jnp.int32),
            jax.ShapeDtypeStruct((1, LANES), f32),
        ],
        scratch_shapes=[pltpu.VMEM((1, LANES), f32)],
        input_output_aliases=aliases,
        compiler_params=_cparams(("arbitrary",)),
    )(*args)


MOE_SUB = 256
MOE_NSUB = 5
MOE_SUPER = MOE_SUB * MOE_NSUB
MOE_FF_TILE = 256
MOE_DOWN_N = 512
MOE_MAX_PAD = N_EXPERTS * (MOE_SUB - 1)
X_SUBL = D_MODEL // (2 * LANES)
HALF_D = D_MODEL // 2
DMA_LOOP_UNROLL = 4


def _pack_bf16_pairs(x):
    hi = lax.bitcast_convert_type(x[:, :HALF_D].astype(bf16).astype(f32), jnp.uint32)
    lo = lax.bitcast_convert_type(x[:, HALF_D:].astype(bf16).astype(f32), jnp.uint32)
    return hi | (lo >> 16)


def _unpack_bf16_pairs(u):
    hi = lax.bitcast_convert_type(u & jnp.uint32(0xFFFF0000), f32)
    lo = lax.bitcast_convert_type(u << 16, f32)
    return hi, lo


def _routing(top_idx, rank, counts_f):
    n = top_idx.shape[0]
    n_assign = n * TOP_K
    counts = counts_f[0, :N_EXPERTS].astype(jnp.int32)
    padded = (counts + MOE_SUPER - 1) // MOE_SUPER * MOE_SUPER
    pad_end = jnp.cumsum(padded)
    start = pad_end - padded
    experts = jnp.arange(N_EXPERTS, dtype=jnp.int32)

    def lookup(table, idx):
        return jnp.sum(jnp.where(idx[..., None] == experts, table, 0), axis=-1)

    def bucket(edges, x):
        return jnp.minimum(jnp.sum((edges <= x[..., None]).astype(jnp.int32), axis=-1),
                           N_EXPERTS - 1)

    dest = (lookup(start, top_idx) + rank).astype(jnp.int32)
    n_super = (n_assign + N_EXPERTS * (MOE_SUPER - 1) + MOE_SUPER - 1) // MOE_SUPER
    s_row0 = jnp.arange(n_super, dtype=jnp.int32) * MOE_SUPER
    super_e = bucket(pad_end, s_row0)
    rows_here = jnp.clip(lookup(counts, super_e) - (s_row0 - lookup(start, super_e)), 0, MOE_SUPER)
    rows_here = jnp.where(s_row0 < pad_end[-1], rows_here, 0)
    n_sub = ((rows_here + MOE_SUB - 1) // MOE_SUB).astype(jnp.int32)
    n_used = (pad_end[-1] // MOE_SUPER).astype(jnp.int32).reshape(1)
    n_pad_e = (counts + MOE_SUB - 1) // MOE_SUB * MOE_SUB - counts
    pad_cum = jnp.cumsum(n_pad_e)
    kk = jnp.arange(MOE_MAX_PAD, dtype=jnp.int32)
    pe = bucket(pad_cum, kk)
    pad_dest = (lookup(start + counts - (pad_cum - n_pad_e), pe) + kk).astype(jnp.int32)
    pad_dest = jnp.where(kk < pad_cum[-1], pad_dest, 0)
    n_pad = pad_cum[-1].astype(jnp.int32).reshape(1)
    return dest, super_e, n_sub, n_used, pad_dest, n_pad, n_super


def _tile_copy(src_ref, src_tok, dst_ref, dst_tok, subl, sem):
    s0 = pl.multiple_of(src_tok * subl, subl)
    d0 = pl.multiple_of(dst_tok * subl, subl)
    return pltpu.make_async_copy(src_ref.at[pl.ds(s0, subl)], dst_ref.at[pl.ds(d0, subl)], sem)


def _scatter_kernel(npad_ref, dest_ref, pad_ref, h_ref, xs_hbm, stage, zero, sems, pad_sem,
                    *, tm, n_steps):
    i = pl.program_id(0)
    slot = i % 2

    def drain(sl):
        for _ in range(TOP_K):
            pltpu.make_async_copy(stage.at[sl], stage.at[sl], sems.at[sl]).wait()

    @pl.when(i >= 2)
    def _():
        drain(slot)

    packed = _pack_bf16_pairs(h_ref[...])
    for l in range(X_SUBL):
        stage[slot, pl.ds(l, tm, stride=X_SUBL), :] = packed[:, LANES * l:LANES * (l + 1)]

    def issue(t, carry):
        for jx in range(TOP_K):
            _tile_copy(stage.at[slot], t, xs_hbm, dest_ref[0, 0, jx * tm + t], X_SUBL,
                       sems.at[slot]).start()
        return carry

    lax.fori_loop(0, tm, issue, 0, unroll=DMA_LOOP_UNROLL)

    @pl.when(i == 0)
    def _():
        zero[...] = jnp.zeros_like(zero)

        def fill(k, carry):
            _tile_copy(zero, 0, xs_hbm, pad_ref[k], X_SUBL, pad_sem).start()
            return carry

        lax.fori_loop(0, npad_ref[0], fill, 0)

        def fill_wait(k, carry):
            _tile_copy(zero, 0, xs_hbm, 0, X_SUBL, pad_sem).wait()
            return carry

        lax.fori_loop(0, npad_ref[0], fill_wait, 0)

    @pl.when(i == n_steps - 1)
    def _():
        drain(slot)
        if n_steps > 1:
            drain(1 - slot)


def _scatter_rows(h_all, dest, pad_dest, n_pad, n_rows, tm):
    n = h_all.shape[0]
    assert n % tm == 0
    n_steps = n // tm
    dest_blk = dest.reshape(n_steps, tm, TOP_K).transpose(0, 2, 1).reshape(n_steps, 1, TOP_K * tm)
    kern = functools.partial(_scatter_kernel, tm=tm, n_steps=n_steps)
    return pl.pallas_call(
        kern,
        grid_spec=pltpu.PrefetchScalarGridSpec(
            num_scalar_prefetch=1,
            grid=(n_steps,),
            in_specs=[
                pl.BlockSpec((1, 1, TOP_K * tm), lambda i, npad: (i, 0, 0), memory_space=pltpu.SMEM),
                pl.BlockSpec(memory_space=pltpu.SMEM),
                pl.BlockSpec((tm, D_MODEL), lambda i, npad: (i, 0)),
            ],
            out_specs=pl.BlockSpec(memory_space=pl.ANY),
            scratch_shapes=[
                pltpu.VMEM((2, tm * X_SUBL, LANES), jnp.uint32),
                pltpu.VMEM((X_SUBL, LANES), jnp.uint32),
                pltpu.SemaphoreType.DMA((2,)),
                pltpu.SemaphoreType.DMA(()),
            ],
        ),
        out_shape=jax.ShapeDtypeStruct((n_rows * X_SUBL, LANES), jnp.uint32),
        compiler_params=_cparams(("arbitrary",)),
    )(n_pad, dest_blk, pad_dest, h_all)


def _expert_kernel(se_ref, nsub_ref, nused_ref, x_ref, wg_ref, wl_ref, bg_ref, bl_ref, wd_ref,
                   bd_ref, o_ref, xb_scr, acc_scr, *, nf):
    s = pl.program_id(0)
    j = pl.program_id(1)
    n_sub = nsub_ref[s]

    def step(m):
        rows = slice(0, m)

        @pl.when(j == 0)
        def _():
            for l in range(X_SUBL):
                hi, lo = _unpack_bf16_pairs(x_ref[pl.ds(l, m, stride=X_SUBL), :])
                xb_scr[rows, LANES * l:LANES * (l + 1)] = hi.astype(bf16)
                xb_scr[rows, HALF_D + LANES * l:HALF_D + LANES * (l + 1)] = lo.astype(bf16)

        xb = xb_scr[rows, :]
        hg = jnp.dot(xb, wg_ref[...].astype(bf16), preferred_element_type=f32) + bg_ref[...]
        hl = jnp.dot(xb, wl_ref[...].astype(bf16), preferred_element_type=f32) + bl_ref[...]
        glu = jnp.minimum(hg, SWIGLU_LIMIT)
        lin = jnp.clip(hl, -SWIGLU_LIMIT, SWIGLU_LIMIT)
        act = (glu * (1.0 / (1.0 + jnp.exp(-SWIGLU_ALPHA * glu))) * (lin + 1.0)).astype(bf16)
        wd = wd_ref[...].astype(bf16)

        def down(n0):
            return jnp.dot(act, wd[:, n0:n0 + MOE_DOWN_N], preferred_element_type=f32)

        @pl.when(j == 0)
        def _():
            for n0 in range(0, D_MODEL, MOE_DOWN_N):
                acc_scr[rows, n0:n0 + MOE_DOWN_N] = down(n0) + bd_ref[:, n0:n0 + MOE_DOWN_N]

        @pl.when(j != 0)
        def _():
            for n0 in range(0, D_MODEL, MOE_DOWN_N):
                acc_scr[rows, n0:n0 + MOE_DOWN_N] += down(n0)

        @pl.when(j == nf - 1)
        def _():
            packed = _pack_bf16_pairs(acc_scr[rows, :])
            for l in range(X_SUBL):
                o_ref[pl.ds(l, m, stride=X_SUBL), :] = packed[:, LANES * l:LANES * (l + 1)]

    for k in range(1, MOE_NSUB + 1):
        pl.when(n_sub == k)(functools.partial(step, k * MOE_SUB))


def _experts(xs, super_e, n_sub, n_used, w_up, b_up, w_down, b_down, n_super):
    tf = MOE_FF_TILE
    nf = D_FF // tf
    last = lambda s, nu: jnp.minimum(s, nu[0] - 1)
    b_up3 = b_up.reshape(N_EXPERTS, 1, 2 * D_FF)
    b_down3 = b_down.reshape(N_EXPERTS, 1, D_MODEL)
    e_of = lambda s, se, nu: se[last(s, nu)]
    return pl.pallas_call(
        functools.partial(_expert_kernel, nf=nf),
        grid_spec=pltpu.PrefetchScalarGridSpec(
            num_scalar_prefetch=3,
            grid=(n_super, nf),
            in_specs=[
                pl.BlockSpec((MOE_SUPER * X_SUBL, LANES), lambda s, j, se, ns, nu: (last(s, nu), 0)),
                pl.BlockSpec((None, D_MODEL, tf), lambda s, j, se, ns, nu: (e_of(s, se, nu), 0, j)),
                pl.BlockSpec((None, D_MODEL, tf),
                             lambda s, j, se, ns, nu: (e_of(s, se, nu), 0, nf + j)),
                pl.BlockSpec((None, 1, tf), lambda s, j, se, ns, nu: (e_of(s, se, nu), 0, j)),
                pl.BlockSpec((None, 1, tf), lambda s, j, se, ns, nu: (e_of(s, se, nu), 0, nf + j)),
                pl.BlockSpec((None, tf, D_MODEL), lambda s, j, se, ns, nu: (e_of(s, se, nu), j, 0)),
                pl.BlockSpec((None, 1, D_MODEL), lambda s, j, se, ns, nu: (e_of(s, se, nu), 0, 0)),
            ],
            out_specs=pl.BlockSpec((MOE_SUPER * X_SUBL, LANES),
                                   lambda s, j, se, ns, nu: (last(s, nu), 0)),
            scratch_shapes=[pltpu.VMEM((MOE_SUPER, D_MODEL), bf16),
                            pltpu.VMEM((MOE_SUPER, D_MODEL), f32)],
        ),
        out_shape=jax.ShapeDtypeStruct((n_super * MOE_SUPER * X_SUBL, LANES), jnp.uint32),
        compiler_params=_cparams(("arbitrary", "arbitrary")),
    )(super_e, n_sub, n_used, xs, w_up, w_up, b_up3, b_up3, w_down, b_down3)


def _combine_kernel(dest_ref, nxt_ref, gate_ref, h_ref, rows_hbm, g_ref, b_ref, yp_ref, ys_ref,
                    buf, sems, *, tm, n_first, n_steps):
    i = pl.program_id(0)
    slot = i % 2

    def fetch(idx_ref, sl):
        def body(t, carry):
            for jx in range(TOP_K):
                _tile_copy(rows_hbm, idx_ref[0, 0, jx * tm + t], buf.at[sl, jx], t, X_SUBL,
                           sems.at[sl]).start()
            return carry
        lax.fori_loop(0, tm, body, 0, unroll=DMA_LOOP_UNROLL)

    @pl.when(i == 0)
    def _():
        fetch(dest_ref, slot)

    @pl.when(i + 1 < n_steps)
    def _():
        fetch(nxt_ref, 1 - slot)

    for jx in range(TOP_K):
        pltpu.make_async_copy(buf.at[slot, jx], buf.at[slot, jx], sems.at[slot]).wait()
    gate = gate_ref[...]
    cols_hi, cols_lo = [], []
    for l in range(X_SUBL):
        acc_hi = acc_lo = None
        for jx in range(TOP_K):
            hi, lo = _unpack_bf16_pairs(buf[slot, jx, pl.ds(l, tm, stride=X_SUBL), :])
            gj = gate[:, jx:jx + 1]
            acc_hi = gj * hi if acc_hi is None else acc_hi + gj * hi
            acc_lo = gj * lo if acc_lo is None else acc_lo + gj * lo
        cols_hi.append(acc_hi)
        cols_lo.append(acc_lo)
    y = jnp.concatenate(cols_hi + cols_lo, axis=-1)
    out = _layer_norm(DN_ALPHA * h_ref[...] + y, g_ref[...], b_ref[...])

    @pl.when(i < n_first)
    def _():
        yp_ref[...] = out

    @pl.when(i >= n_first)
    def _():
        ys_ref[...] = out


def _combine(rows_out, dest, gate, h_all, ln_g, ln_b, *, tm, n_first_rows):
    n = h_all.shape[0]
    assert n % tm == 0 and n_first_rows % tm == 0
    nblk = n // tm
    n_first = n_first_rows // tm
    dest_blk = dest.reshape(nblk, tm, TOP_K).transpose(0, 2, 1).reshape(nblk, 1, TOP_K * tm)
    kern = functools.partial(_combine_kernel, tm=tm, n_first=n_first, n_steps=nblk)
    const = lambda i: (0, 0)
    return pl.pallas_call(
        kern,
        grid=(nblk,),
        in_specs=[
            pl.BlockSpec((1, 1, TOP_K * tm), lambda i: (i, 0, 0), memory_space=pltpu.SMEM),
            pl.BlockSpec((1, 1, TOP_K * tm), lambda i: (jnp.minimum(i + 1, nblk - 1), 0, 0),
                         memory_space=pltpu.SMEM),
            pl.BlockSpec((tm, TOP_K), lambda i: (i, 0)),
            pl.BlockSpec((tm, D_MODEL), lambda i: (i, 0)),
            pl.BlockSpec(memory_space=pl.ANY),
            pl.BlockSpec((1, D_MODEL), const),
            pl.BlockSpec((1, D_MODEL), const),
        ],
        out_specs=[
            pl.BlockSpec((tm, D_MODEL), lambda i: (jnp.minimum(i, n_first - 1), 0)),
            pl.BlockSpec((tm, D_MODEL), lambda i: (jnp.maximum(i - n_first, 0), 0)),
        ],
        out_shape=[
            jax.ShapeDtypeStruct((n_first_rows, D_MODEL), f32),
            jax.ShapeDtypeStruct((n - n_first_rows, D_MODEL), f32),
        ],
        scratch_shapes=[pltpu.VMEM((2, TOP_K, tm * X_SUBL, LANES), jnp.uint32),
                        pltpu.SemaphoreType.DMA((2,))],
        compiler_params=_cparams(("arbitrary",)),
    )(dest_blk, dest_blk, gate, h_all, rows_out, ln_g, ln_b)


def _t5_bucket(rel):
    half = NUM_BUCKETS // 2
    exact = half // 2
    n = jnp.abs(rel)
    log_part = exact + (jnp.log(jnp.maximum(n, 1).astype(jnp.float32) / exact)
                        / math.log(MAX_DISTANCE / exact) * (half - exact)).astype(jnp.int32)
    log_part = jnp.minimum(log_part, half - 1)
    return jnp.where(rel > 0, half, 0) + jnp.where(n < exact, n, log_part)


def _band_bias(rel_bias):
    qi = jnp.arange(CHUNK)[:, None]
    km = jnp.arange(BAND)[None, :]
    bucket = _t5_bucket(km - WINDOW - qi)
    return jnp.transpose(rel_bias[bucket], (2, 0, 1)).astype(jnp.float32)


def _pad_cols(a, width):
    return jnp.pad(a, ((0, 0), (0, width - a.shape[-1])))


def _pair_state(s):
    b = s.shape[0]
    return s.reshape(b, N_PAIRS, 2, RW_HEAD, RW_HEAD).transpose(0, 1, 3, 2, 4).reshape(
        b, N_PAIRS, RW_HEAD, LANES)


def _unpair_state(s):
    b = s.shape[0]
    return s.reshape(b, N_PAIRS, RW_HEAD, 2, RW_HEAD).transpose(0, 1, 3, 2, 4).reshape(
        b, RW_HEADS, RW_HEAD, RW_HEAD)


def _mix_group(x, k_hist, v_hist, hist_valid, wkv0, shift0, bias, sinks3, w_in_pad, rw_wts,
               *, in_tm, attn_nc):
    b, t, _ = x.shape
    q, kv, prw = _inproj(x.reshape(b * t, D_MODEL), w_in_pad, in_tm)
    tp = -(-t // (attn_nc * CHUNK)) * (attn_nc * CHUNK)
    q4 = q.reshape(ATT_HEADS, b, t, HEAD_DIM)
    kv3 = kv.reshape(b, t, 2 * KV_W)
    prw3 = prw.reshape(b, t, RW_PAD)
    if tp != t:
        q4 = jnp.pad(q4, ((0, 0), (0, 0), (0, tp - t), (0, 0)))
        prw3 = jnp.pad(prw3, ((0, 0), (0, tp - t), (0, 0)))
    hist = jnp.concatenate([k_hist.reshape(b, WINDOW, KV_W), v_hist.reshape(b, WINDOW, KV_W)], axis=-1)
    kvfull = jnp.concatenate([hist, kv3, jnp.zeros((b, tp - t, 2 * KV_W), f32)], axis=1)
    att = _attention(q4, kvfull, bias, sinks3, nc=attn_nc, t_valid=t, hist_valid=hist_valid)
    rw, s_fin = _rwkv(prw3, _pad_cols(shift0.reshape(b, RW_PROJ), RW_PAD).reshape(b, 1, RW_PAD),
                      _pair_state(wkv0.astype(f32)), rw_wts, t_valid=t)
    new_kv = kvfull[:, t:t + WINDOW]
    new_k = new_kv[..., :KV_W].reshape(b, WINDOW, ATT_KV_HEADS, HEAD_DIM)
    new_v = new_kv[..., KV_W:].reshape(b, WINDOW, ATT_KV_HEADS, HEAD_DIM)
    shift = prw3[:, t - 1:t, :RW_PROJ]
    return (att[:, :t].reshape(b * t, ATT_W), rw[:, :t].reshape(b * t, RW_W),
            new_k, new_v, _unpair_state(s_fin), shift)


SCATTER_TM = 128
COMBINE_TM = 128


def kernel(x_prompt, x_sample, cache_k, cache_v, state_wkv, state_shift, rel_bias, w_in, attn_sinks, rw_mu, rw_w0, rw_decay_up, rw_a0, rw_iclr_up, rw_gate_up, rw_k_k, rw_k_a, rw_r_k, rw_lnx_g, rw_lnx_b, w_out, ln1_g, ln1_b, w_router, b_router, w_up, b_up, w_down, b_down, ln2_g, ln2_b):
    assert w_in.shape[0] == DEPTH == 1
    l = 0
    bp, tp_, _ = x_prompt.shape
    bs, ts, _ = x_sample.shape
    bias = _band_bias(rel_bias)
    sinks3 = attn_sinks[l].astype(f32).reshape(ATT_HEADS, 1, 1)

    w_in_pad = _pad_cols(w_in[l], IN_PAD).astype(bf16)

    def lora_rows(w, row0):
        return jnp.zeros((LORA_PAD, RW_W), f32).at[row0:row0 + w.shape[0]].set(w).astype(bf16)

    rw_wts = {
        "mu": _pad_cols(rw_mu[l].reshape(1, RW_PROJ), RW_PAD),
        "w0": rw_w0[l].reshape(1, RW_W),
        "wd": lora_rows(rw_decay_up[l], 0),
        "a0": rw_a0[l].reshape(1, RW_W),
        "wa": lora_rows(rw_iclr_up[l], DECAY_LORA),
        "wg": lora_rows(rw_gate_up[l], DECAY_LORA + ICLR_LORA),
        "k_k": rw_k_k[l].reshape(1, RW_W),
        "k_a": rw_k_a[l].reshape(1, RW_W),
        "r_k": rw_r_k[l].reshape(1, RW_W),
        "lnx_g": rw_lnx_g[l].reshape(1, RW_W),
        "lnx_b": rw_lnx_b[l].reshape(1, RW_W),
    }
    wr = _pad_cols(w_router[l], LANES)
    wr_hi = wr.astype(bf16)
    op_wts = {
        "w_out": w_out[l].astype(bf16),
        "ln1_g": ln1_g[l].reshape(1, D_MODEL),
        "ln1_b": ln1_b[l].reshape(1, D_MODEL),
        "wr_hi": wr_hi,
        "wr_lo": (wr - wr_hi.astype(f32)).astype(bf16),
        "b_router": jnp.concatenate([b_router[l].astype(f32),
                                     jnp.full((LANES - N_EXPERTS,), NEG_BIG, f32)]).reshape(1, LANES),
    }

    zero_kv = jnp.zeros((bp, WINDOW, ATT_KV_HEADS, HEAD_DIM), f32)
    att_p, rwo_p, k1, v1, w1, s1 = _mix_group(
        x_prompt, zero_kv, zero_kv, False, jnp.zeros((bp, RW_HEADS, RW_HEAD, RW_HEAD), f32),
        jnp.zeros((bp, 1, RW_PROJ), f32), bias, sinks3, w_in_pad, rw_wts,
        in_tm=min(256, bp * tp_), attn_nc=min(8, -(-tp_ // CHUNK)))
    att_s, rwo_s, k2, v2, w2, s2 = _mix_group(
        x_sample, cache_k[l], cache_v[l], True, state_wkv[l], state_shift[l], bias, sinks3,
        w_in_pad, rw_wts, in_tm=min(256, bs * ts), attn_nc=1)

    n_p, n_s = bp * tp_, bs * ts
    n_all = n_p + n_s
    tm_p, tm_s = min(256, n_p), min(128, n_s)
    *outs, counts_p = _outproj(att_p, rwo_p, x_prompt.reshape(n_p, D_MODEL), op_wts,
                               tm=tm_p, row0=0, total_rows=n_all)
    h_all, top_idx, gate, rank, counts = _outproj(
        att_s, rwo_s, x_sample.reshape(n_s, D_MODEL), op_wts,
        tm=tm_s, row0=n_p, total_rows=n_all, prev=outs, counts0=counts_p)

    dest, super_e, n_sub, n_used, pad_dest, n_pad, n_super = _routing(top_idx, rank, counts)
    xs = _scatter_rows(h_all, dest, pad_dest, n_pad, n_super * MOE_SUPER, min(SCATTER_TM, n_s))
    rows_out = _experts(xs, super_e, n_sub, n_used, w_up[l], b_up[l], w_down[l], b_down[l],
                        n_super)
    y_p, y_s = _combine(rows_out, dest, gate, h_all, ln2_g[l].reshape(1, D_MODEL),
                        ln2_b[l].reshape(1, D_MODEL), tm=min(COMBINE_TM, n_s), n_first_rows=n_p)

    return (y_p.reshape(bp, tp_, D_MODEL), y_s.reshape(bs, ts, D_MODEL),
            k1[None], v1[None], w1[None], s1[None], k2[None], v2[None], w2[None], s2[None])
```

```python
import functools
import math

import jax
import jax.numpy as jnp
from jax import lax
from jax.experimental import pallas as pl
from jax.experimental.pallas import tpu as pltpu

f32 = jnp.float32
bf16 = jnp.bfloat16

D_MODEL = 2048
CHUNK = 64
ATT_HEADS = 16
ATT_KV_HEADS = 2
HEAD_DIM = 64
ATT_GROUP = ATT_HEADS // ATT_KV_HEADS
ATT_W = ATT_HEADS * HEAD_DIM
KV_W = ATT_KV_HEADS * HEAD_DIM
ATT_PROJ = ATT_W + 2 * KV_W
WINDOW = 128
BAND = WINDOW + CHUNK
NUM_BUCKETS = 32
MAX_DISTANCE = 128
RW_HEAD = 64
RW_W = 1024
RW_HEADS = RW_W // RW_HEAD
DECAY_LORA = 96
ICLR_LORA = 96
GATE_LORA = 128
RW_PROJ = 3 * RW_W + DECAY_LORA + ICLR_LORA + GATE_LORA
GN_EPS = 64e-5
LN_EPS = 1e-5
N_EXPERTS = 32
TOP_K = 4
D_FF = D_MODEL
SWIGLU_LIMIT = 7.0
SWIGLU_ALPHA = 1.702
DEPTH = 1
DN_ALPHA = (2 * DEPTH) ** 0.25

LANES = 128
VMEM_LIMIT = 56 * 1024 * 1024

LORA_W = DECAY_LORA + ICLR_LORA + GATE_LORA
LORA_PAD = -(-LORA_W // LANES) * LANES
RW_PAD = 3 * RW_W + LORA_PAD
IN_PAD = ATT_PROJ + RW_PAD
N_PAIRS = RW_HEADS // 2
NEG_BIG = -1e30


def _cparams(sem):
    return pltpu.CompilerParams(dimension_semantics=sem, vmem_limit_bytes=VMEM_LIMIT)


def _inproj_kernel(x_ref, w_ref, q_ref, kv_ref, rw_ref):
    acc = jnp.dot(x_ref[...].astype(bf16), w_ref[...], preferred_element_type=f32)
    scale = HEAD_DIM ** -0.5
    for h in range(ATT_HEADS):
        q_ref[h] = (acc[:, h * HEAD_DIM:(h + 1) * HEAD_DIM] * scale).astype(bf16)
    kv_ref[...] = acc[:, ATT_W:ATT_PROJ]
    rw_ref[...] = acc[:, ATT_PROJ:IN_PAD]


def _inproj(x2, w_pad, tm):
    rows = x2.shape[0]
    assert rows % tm == 0
    return pl.pallas_call(
        _inproj_kernel,
        grid=(rows // tm,),
        in_specs=[
            pl.BlockSpec((tm, D_MODEL), lambda i: (i, 0)),
            pl.BlockSpec((D_MODEL, IN_PAD), lambda i: (0, 0), pipeline_mode=pl.Buffered(1)),
        ],
        out_specs=[
            pl.BlockSpec((ATT_HEADS, tm, HEAD_DIM), lambda i: (0, i, 0)),
            pl.BlockSpec((tm, 2 * KV_W), lambda i: (i, 0)),
            pl.BlockSpec((tm, RW_PAD), lambda i: (i, 0)),
        ],
        out_shape=[
            jax.ShapeDtypeStruct((ATT_HEADS, rows, HEAD_DIM), bf16),
            jax.ShapeDtypeStruct((rows, 2 * KV_W), f32),
            jax.ShapeDtypeStruct((rows, RW_PAD), f32),
        ],
        compiler_params=_cparams(("parallel",)),
    )(x2, w_pad)


ATT_UNROLL = 4


def _attn_kernel(q_ref, kvm_ref, kva_ref, kvb_ref, bias_ref, sink_ref, o_ref, kvbuf,
                 *, nc, t_valid, hist_valid):
    j = pl.program_id(1)
    kvbuf[0:nc * CHUNK] = kvm_ref[...].astype(bf16)
    kvbuf[nc * CHUNK:(nc + 1) * CHUNK] = kva_ref[...].astype(bf16)
    kvbuf[(nc + 1) * CHUNK:(nc + 2) * CHUNK] = kvb_ref[...].astype(bf16)
    m_idx = lax.broadcasted_iota(jnp.int32, (1, 1, BAND), 2)
    for c0 in range(0, nc, ATT_UNROLL):
        items = [(c, g) for c in range(c0, min(c0 + ATT_UNROLL, nc)) for g in range(ATT_KV_HEADS)]
        bands = {c: kvbuf[c * CHUNK:c * CHUNK + BAND, :] for c, _ in items}
        valid = {}
        for c in bands:
            idx = (j * nc + c) * CHUNK + m_idx
            v = idx - WINDOW < t_valid
            valid[c] = v if hist_valid else jnp.logical_and(v, idx >= WINDOW)
        s = [lax.dot_general(
                q_ref[g * ATT_GROUP:(g + 1) * ATT_GROUP, c * CHUNK:(c + 1) * CHUNK, :].reshape(
                    ATT_GROUP * CHUNK, HEAD_DIM),
                bands[c][:, g * HEAD_DIM:(g + 1) * HEAD_DIM],
                (((1,), (1,)), ((), ())), preferred_element_type=f32) for c, g in items]
        s = [jnp.where(valid[c], s[i].reshape(ATT_GROUP, CHUNK, BAND)
                       + bias_ref[g * ATT_GROUP:(g + 1) * ATT_GROUP], NEG_BIG)
             for i, (c, g) in enumerate(items)]
        sk = [sink_ref[g * ATT_GROUP:(g + 1) * ATT_GROUP] for _, g in items]
        m = [jnp.maximum(jnp.max(s[i], axis=-1, keepdims=True), sk[i]) for i in range(len(items))]
        p = [jnp.exp(s[i] - m[i]) for i in range(len(items))]
        den = [jnp.sum(p[i], axis=-1, keepdims=True) + jnp.exp(sk[i] - m[i])
               for i in range(len(items))]
        o = [jnp.dot(p[i].reshape(ATT_GROUP * CHUNK, BAND).astype(bf16),
                     bands[c][:, KV_W + g * HEAD_DIM:KV_W + (g + 1) * HEAD_DIM],
                     preferred_element_type=f32).reshape(ATT_GROUP, CHUNK, HEAD_DIM) / den[i]
             for i, (c, g) in enumerate(items)]
        for c in bands:
            heads = [o[i][h] for i, (ci, _) in enumerate(items) if ci == c for h in range(ATT_GROUP)]
            o_ref[c * CHUNK:(c + 1) * CHUNK, :] = jnp.concatenate(heads, axis=-1).astype(bf16)


def _attention(q4, kvfull, bias, sinks3, *, nc, t_valid, hist_valid):
    _, b, tp, _ = q4.shape
    assert tp % (nc * CHUNK) == 0 and kvfull.shape[1] == tp + WINDOW
    nblk = tp // (nc * CHUNK)
    kern = functools.partial(_attn_kernel, nc=nc, t_valid=t_valid, hist_valid=hist_valid)
    return pl.pallas_call(
        kern,
        grid=(b, nblk),
        in_specs=[
            pl.BlockSpec((ATT_HEADS, None, nc * CHUNK, HEAD_DIM), lambda bi, j: (0, bi, j, 0)),
            pl.BlockSpec((None, nc * CHUNK, 2 * KV_W), lambda bi, j: (bi, j, 0)),
            pl.BlockSpec((None, CHUNK, 2 * KV_W), lambda bi, j: (bi, (j + 1) * nc, 0)),
            pl.BlockSpec((None, CHUNK, 2 * KV_W), lambda bi, j: (bi, (j + 1) * nc + 1, 0)),
            pl.BlockSpec((ATT_HEADS, CHUNK, BAND), lambda bi, j: (0, 0, 0)),
            pl.BlockSpec((ATT_HEADS, 1, 1), lambda bi, j: (0, 0, 0)),
        ],
        out_specs=pl.BlockSpec((None, nc * CHUNK, ATT_W), lambda bi, j: (bi, j, 0)),
        out_shape=jax.ShapeDtypeStruct((b, tp, ATT_W), bf16),
        scratch_shapes=[pltpu.VMEM(((nc + 2) * CHUNK, 2 * KV_W), bf16)],
        compiler_params=_cparams(("parallel", "parallel")),
    )(q4, kvfull, kvfull, kvfull, bias, sinks3)


N_OPS_BF16 = 7
N_OPS_F32 = 4


def _rwkv_step_kernel(p_ref, shift0_ref, s0_ref, mu_ref, w0_ref, wd_ref, a0_ref, wa_ref, wg_ref,
                      kk_ref, ka_ref, rk_ref, lng_ref, lnb_ref, o_ref, sfin_ref,
                      s_scr, last_scr, opb_scr, opf_scr, cl_scr, *, t_valid, n_steps, nch):
    c = pl.program_id(1)
    L = CHUNK
    R = nch * L

    @pl.when(c == 0)
    def _():
        s_scr[...] = s0_ref[...]
        last_scr[...] = shift0_ref[...]
        opb_scr[...] = jnp.zeros_like(opb_scr)
        opf_scr[...] = jnp.zeros_like(opf_scr)
        cl_scr[...] = jnp.zeros_like(cl_scr)

    kq, rq, bt, kt, bh, kh, vb = range(N_OPS_BF16)
    g_l = [jnp.exp(cl_scr[ch]) for ch in range(nch)]

    lane = lax.broadcasted_iota(jnp.int32, (1, LANES), 1)
    lo_half = lane < RW_HEAD
    rr = lax.broadcasted_iota(jnp.int32, (LANES, LANES), 0)
    cc = lax.broadcasted_iota(jnp.int32, (LANES, LANES), 1)
    same_head = (rr // RW_HEAD) == (cc // RW_HEAD)
    r4 = lax.broadcasted_iota(jnp.int32, (2 * LANES, 2 * LANES), 0)
    c4 = lax.broadcasted_iota(jnp.int32, (2 * LANES, 2 * LANES), 1)
    ones_bd = jnp.where((r4 // RW_HEAD) == (c4 // RW_HEAD), 1.0, 0.0).astype(bf16)
    rs = [slice(L * ch, L * (ch + 1)) for ch in range(nch)]
    sl = [slice(LANES * t, LANES * (t + 1)) for t in range(N_PAIRS)]
    n_quad = RW_W // (2 * LANES)

    def seg_sums(xs):
        n = len(xs)
        x = jnp.concatenate(xs, axis=0) if n > 1 else xs[0]
        hi = x.astype(bf16)
        lo = (x - hi.astype(f32)).astype(bf16)
        both = jnp.concatenate([hi, lo], axis=0)
        m = 2 * n * R
        tiles = jnp.concatenate([both[:, 2 * LANES * t:2 * LANES * (t + 1)] for t in range(n_quad)],
                                axis=0)
        res = jnp.dot(tiles, ones_bd, preferred_element_type=f32)
        y = jnp.concatenate([res[m * t:m * (t + 1)] for t in range(n_quad)], axis=1)
        y = y[:n * R] + y[n * R:]
        return [y[i * R:(i + 1) * R] for i in range(n)]

    def sigmoid(z):
        return 1.0 / (1.0 + jnp.exp(-z))

    p = p_ref[...]
    row = lax.broadcasted_iota(jnp.int32, (R, 1), 0)
    shifted = jnp.where(row == 0, last_scr[...], pltpu.roll(p, 1, axis=0))
    last_scr[...] = p[R - 1:R, :]
    xm = p + (shifted - p) * mu_ref[...]
    r = xm[:, 0:RW_W]
    k = xm[:, RW_W:2 * RW_W]
    v = xm[:, 2 * RW_W:3 * RW_W]
    tail = xm[:, 3 * RW_W:RW_PAD]
    w_log = w0_ref[...] + jnp.dot(jnp.tanh(tail).astype(bf16), wd_ref[...],
                                  preferred_element_type=f32)
    a = sigmoid(a0_ref[...] + jnp.dot(tail.astype(bf16), wa_ref[...], preferred_element_type=f32))
    g = jnp.dot(sigmoid(tail).astype(bf16), wg_ref[...], preferred_element_type=f32)

    def prep_mid():
        z = -w_log
        softplus = jnp.maximum(z, 0.0) + jnp.log(1.0 + jnp.exp(-jnp.abs(z)))
        ld = -jnp.exp(-softplus - 0.5)
        kk = k * kk_ref[...]
        k_mod = k * (1.0 + (a - 1.0) * ka_ref[...])
        nrm2, bonus_s = seg_sums([kk * kk, r * k_mod * rk_ref[...]])
        kk = kk / jnp.maximum(jnp.sqrt(nrm2), 1e-12)
        b = kk * a
        if t_valid % R != 0:
            live = (jnp.minimum(c, n_steps - 1) * R + row) < t_valid
            ld = jnp.where(live, ld, 0.0)
            b = jnp.where(live, b, 0.0)
            k_mod = jnp.where(live, k_mod, 0.0)
        h1 = ld.astype(bf16)
        r1 = ld - h1.astype(f32)
        h2 = r1.astype(bf16)
        h3 = (r1 - h2.astype(f32)).astype(bf16)
        ti = lax.broadcasted_iota(jnp.int32, (L, 3 * L), 0)
        si = lax.broadcasted_iota(jnp.int32, (L, 3 * L), 1) % L
        tri3 = jnp.where(si <= ti, 1.0, 0.0).astype(bf16)
        cums = [jnp.dot(tri3, jnp.concatenate([h1[rs[ch]], h2[rs[ch]], h3[rs[ch]]], axis=0),
                        preferred_element_type=f32) for ch in range(nch)]
        return kk, k_mod, b, ld, cums, bonus_s

    def prep_tail(kk, k_mod, b, ld, cums, bonus_s):
        cum = jnp.concatenate(cums, axis=0) if nch > 1 else cums[0]
        cum_ls = [cums[ch][L - 1:L, :] for ch in range(nch)]
        cum_l = (jnp.concatenate([jnp.broadcast_to(x, (L, RW_W)) for x in cum_ls], axis=0)
                 if nch > 1 else cum_ls[0])
        g_inv = jnp.exp(-cum)
        g_rest = jnp.exp(cum_l - cum)
        n_rq = r * jnp.exp(cum)
        ops_b = [(kk * jnp.exp(cum - ld)).astype(bf16), n_rq.astype(bf16),
                 (b * g_inv).astype(bf16), (k_mod * g_inv).astype(bf16),
                 (b * g_rest).astype(bf16), (k_mod * g_rest).astype(bf16), v.astype(bf16)]
        return ops_b, [n_rq, v, g, bonus_s], cum_ls

    def bd(x):
        zero = jnp.zeros_like(x)
        return jnp.concatenate([jnp.where(lo_half, x, zero), jnp.where(lo_half, zero, x)], axis=0)

    def mm(x, y):
        return jnp.dot(x, y, preferred_element_type=f32)

    def mm_nt(x, y):
        return lax.dot_general(x, y, (((1,), (1,)), ((), ())), preferred_element_type=f32)

    def mm_tn(x, y):
        return lax.dot_general(x, y, (((0,), (0,)), ((), ())), preferred_element_type=f32)

    tt = lax.broadcasted_iota(jnp.int32, (L, LANES), 0)
    ss = lax.broadcasted_iota(jnp.int32, (L, LANES), 1) % RW_HEAD
    strict = ss < tt
    incl = ss <= tt
    eye_pair = jnp.where(ss == tt, 1.0, 0.0).astype(f32)

    items = [(ch, t) for ch in range(nch) for t in range(N_PAIRS)]
    I = range(len(items))

    def tile(slot, i):
        ch, t = items[i]
        return opb_scr[slot, rs[ch], sl[t]]

    a_all = [mm_nt(jnp.concatenate([tile(kq, i), tile(rq, i)], axis=0),
                   jnp.concatenate([bd(tile(bt, i)), bd(tile(kt, i))], axis=0)) for i in I]
    a_bk = [jnp.where(strict, a_all[i][:L, :LANES], 0.0) for i in I]
    a_kk = [jnp.where(strict, a_all[i][:L, LANES:], 0.0).astype(bf16) for i in I]
    a_rb = [jnp.where(incl, a_all[i][L:, :LANES], 0.0).astype(bf16) for i in I]
    a_rk = [jnp.where(incl, a_all[i][L:, LANES:], 0.0).astype(bf16) for i in I]
    bd_v = [bd(tile(vb, i)) for i in I]
    akv = [mm(a_kk[i], bd_v[i]).astype(bf16) for i in I]
    w_inv = [eye_pair - a_bk[i] for i in I]
    pw = [a_bk[i].astype(bf16) for i in I]
    pw_bd = [bd(pw[i]) for i in I]
    for it in range(5):
        pw = [mm(pw[i], pw_bd[i]).astype(bf16) for i in I]
        pw_bd = [bd(pw[i]) for i in I]
        w_inv = [w_inv[i] + mm(w_inv[i].astype(bf16), pw_bd[i]) for i in I]
        if it == 2:
            mid = prep_mid()
    qu = [mm(w_inv[i].astype(bf16), jnp.concatenate([bd(tile(kq, i)), bd(akv[i])], axis=1))
          for i in I]
    q_m = [qu[i][:, :LANES].astype(bf16) for i in I]
    u_m = [qu[i][:, LANES:].astype(bf16) for i in I]
    m_full = [mm_tn(q_m[i], tile(bh, i)) for i in I]
    neg_m = [jnp.where(same_head, -m_full[i], 0.0).astype(bf16) for i in I]
    c_full = [mm_tn(jnp.concatenate([tile(vb, i), u_m[i]], axis=0),
                    jnp.concatenate([tile(kh, i), -tile(bh, i)], axis=0)) for i in I]
    go = [mm(a_rb[i], jnp.concatenate([bd(q_m[i]), bd(u_m[i])], axis=1)) for i in I]
    o_rk = [mm(a_rk[i], bd_v[i]) for i in I]
    new_b, new_f, new_cl = prep_tail(*mid)
    g_m = [(opf_scr[0, rs[items[i][0]], sl[items[i][1]]] - go[i][:, :LANES]).astype(bf16)
           for i in I]
    o_in = [o_rk[i] - go[i][:, LANES:] for i in I]
    c_pair = [jnp.where(lo_half, c_full[i][:RW_HEAD], c_full[i][RW_HEAD:]) for i in I]
    s_cur = [s_scr[t] for t in range(N_PAIRS)]
    o_rows = []
    for ch in range(nch):
        ii = [ch * N_PAIRS + t for t in range(N_PAIRS)]
        s_b = [s.astype(bf16) for s in s_cur]
        o_rows.append(jnp.concatenate(
            [mm_nt(g_m[i], bd(s_b[t])) + o_in[i] for t, i in enumerate(ii)], axis=1))
        s_upd = [mm(s_b[t], neg_m[i]) for t, i in enumerate(ii)]
        s_cur = [s_cur[t] * g_l[ch][:, sl[t]] + s_upd[t] + c_pair[i] for t, i in enumerate(ii)]
    for t in range(N_PAIRS):
        s_scr[t] = s_cur[t]

    o = jnp.concatenate(o_rows, axis=0) if nch > 1 else o_rows[0]
    (o_sum,) = seg_sums([o])
    d = o - o_sum * (1.0 / RW_HEAD)
    (d2,) = seg_sums([d * d])
    on = d * lax.rsqrt(d2 * (1.0 / RW_HEAD) + GN_EPS) * lng_ref[...] + lnb_ref[...]
    o_ref[...] = ((on + opf_scr[3] * opf_scr[1]) * opf_scr[2]).astype(bf16)

    for i in range(N_OPS_BF16):
        opb_scr[i] = new_b[i]
    for i in range(N_OPS_F32):
        opf_scr[i] = new_f[i]
    for ch in range(nch):
        cl_scr[ch] = new_cl[ch]

    @pl.when(c == n_steps)
    def _():
        sfin_ref[...] = s_scr[...]


RWKV_CHUNKS_PER_STEP = 2


def _rwkv(prw, shift0, s0_pair, wts, *, t_valid):
    b, tp, _ = prw.shape
    nch = RWKV_CHUNKS_PER_STEP if tp % (RWKV_CHUNKS_PER_STEP * CHUNK) == 0 else 1
    rows = nch * CHUNK
    n_steps = tp // rows
    kern = functools.partial(_rwkv_step_kernel, t_valid=t_valid, n_steps=n_steps, nch=nch)
    const2 = lambda bi, c: (0, 0)
    row_spec = pl.BlockSpec((1, RW_W), const2)
    return pl.pallas_call(
        kern,
        grid=(b, n_steps + 1),
        in_specs=[
            pl.BlockSpec((None, rows, RW_PAD), lambda bi, c: (bi, jnp.minimum(c, n_steps - 1), 0)),
            pl.BlockSpec((None, 1, RW_PAD), lambda bi, c: (bi, 0, 0)),
            pl.BlockSpec((None, N_PAIRS, RW_HEAD, LANES), lambda bi, c: (bi, 0, 0, 0)),
            pl.BlockSpec((1, RW_PAD), const2),
            row_spec,
            pl.BlockSpec((LORA_PAD, RW_W), const2),
            row_spec,
            pl.BlockSpec((LORA_PAD, RW_W), const2),
            pl.BlockSpec((LORA_PAD, RW_W), const2),
            row_spec, row_spec, row_spec, row_spec, row_spec,
        ],
        out_specs=[
            pl.BlockSpec((None, rows, RW_W), lambda bi, c: (bi, jnp.maximum(c - 1, 0), 0)),
            pl.BlockSpec((None, N_PAIRS, RW_HEAD, LANES), lambda bi, c: (bi, 0, 0, 0)),
        ],
        out_shape=[
            jax.ShapeDtypeStruct((b, tp, RW_W), bf16),
            jax.ShapeDtypeStruct((b, N_PAIRS, RW_HEAD, LANES), f32),
        ],
        scratch_shapes=[pltpu.VMEM((N_PAIRS, RW_HEAD, LANES), f32), pltpu.VMEM((1, RW_PAD), f32),
                        pltpu.VMEM((N_OPS_BF16, rows, RW_W), bf16),
                        pltpu.VMEM((N_OPS_F32, rows, RW_W), f32),
                        pltpu.VMEM((nch, 1, RW_W), f32)],
        compiler_params=_cparams(("parallel", "arbitrary")),
    )(prw, shift0, s0_pair, wts["mu"], wts["w0"], wts["wd"], wts["a0"], wts["wa"], wts["wg"],
      wts["k_k"], wts["k_a"], wts["r_k"], wts["lnx_g"], wts["lnx_b"])


def _layer_norm(z, g, b):
    mu = jnp.mean(z, axis=-1, keepdims=True)
    d = z - mu
    var = jnp.mean(d * d, axis=-1, keepdims=True)
    return d * lax.rsqrt(var + LN_EPS) * g + b


N_SHARED_OUT = 4
OUTPROJ_PARTS = 2


def _outproj_kernel(*refs, aliased, n_steps):
    att_ref, rw_ref, x_ref, wo_ref, g_ref, b_ref, wrh_ref, wrl_ref, br_ref, cnt0_ref = refs[:10]
    h_ref, idx_ref, gate_ref, rank_ref, cnt_ref, carry = refs[10 + N_SHARED_OUT * aliased:]
    step = pl.program_id(0)

    @pl.when(step == 0)
    def _():
        carry[...] = cnt0_ref[...]

    tm = x_ref.shape[0]
    n_part = OUTPROJ_PARTS if tm % (OUTPROJ_PARTS * 8) == 0 else 1
    pm = tm // n_part
    parts = [slice(q * pm, (q + 1) * pm) for q in range(n_part)]
    mix = [jnp.dot(att_ref[r, :], wo_ref[0:ATT_W], preferred_element_type=f32)
           + jnp.dot(rw_ref[r, :], wo_ref[ATT_W:ATT_W + RW_W], preferred_element_type=f32)
           for r in parts]
    h = [_layer_norm(DN_ALPHA * x_ref[r, :] + mix[q], g_ref[...], b_ref[...])
         for q, r in enumerate(parts)]
    for q, r in enumerate(parts):
        h_ref[r, :] = h[q]
    hh = [x.astype(bf16) for x in h]
    hl = [(h[q] - hh[q].astype(f32)).astype(bf16) for q in range(n_part)]
    logits = [(jnp.dot(hh[q], wrh_ref[...], preferred_element_type=f32)
               + jnp.dot(hl[q], wrh_ref[...], preferred_element_type=f32)
               + jnp.dot(hh[q], wrl_ref[...], preferred_element_type=f32)) + br_ref[...]
              for q in range(n_part)]
    lane = lax.broadcasted_iota(jnp.int32, (pm, LANES), 1).astype(f32)
    ti = lax.broadcasted_iota(jnp.int32, (pm, pm), 0)
    si = lax.broadcasted_iota(jnp.int32, (pm, pm), 1)
    before = jnp.where(si < ti, 1.0, 0.0).astype(bf16)
    counts = carry[...]
    for q, r in enumerate(parts):
        vals, idxs = [], []
        cur = logits[q]
        for _ in range(TOP_K):
            m = jnp.max(cur, axis=-1, keepdims=True)
            i = jnp.min(jnp.where(cur == m, lane, float(LANES)), axis=-1, keepdims=True)
            vals.append(m)
            idxs.append(i)
            cur = jnp.where(lane == i, -jnp.inf, cur)
        es = [jnp.exp(vv - vals[0]) for vv in vals]
        tot = es[0] + es[1] + es[2] + es[3]
        idx_ref[r, :] = jnp.concatenate(idxs, axis=-1).astype(jnp.int32)
        gate_ref[r, :] = jnp.concatenate([e / tot for e in es], axis=-1)
        hits = [jnp.where(lane == i, 1.0, 0.0) for i in idxs]
        multi = hits[0] + hits[1] + hits[2] + hits[3]
        base = counts + jnp.dot(before, multi.astype(bf16), preferred_element_type=f32)
        ranks = [jnp.sum(hh_ * base, axis=-1, keepdims=True) for hh_ in hits]
        rank_ref[r, :] = jnp.concatenate(ranks, axis=-1).astype(jnp.int32)
        counts = counts + jnp.sum(multi, axis=0, keepdims=True)
    carry[...] = counts

    @pl.when(step == n_steps - 1)
    def _():
        cnt_ref[...] = carry[...]


def _outproj(att2, rw2, x2, wts, *, tm, row0, total_rows, prev=None, counts0=None):
    rows = x2.shape[0]
    assert rows % tm == 0 and row0 % tm == 0
    blk0 = row0 // tm
    aliased = prev is not None
    if counts0 is None:
        counts0 = jnp.zeros((1, LANES), f32)
    const = lambda i: (0, 0)
    in_specs = [
        pl.BlockSpec((tm, ATT_W), lambda i: (i, 0)),
        pl.BlockSpec((tm, RW_W), lambda i: (i, 0)),
        pl.BlockSpec((tm, D_MODEL), lambda i: (i, 0)),
        pl.BlockSpec((D_MODEL, D_MODEL), const, pipeline_mode=pl.Buffered(1)),
        pl.BlockSpec((1, D_MODEL), const),
        pl.BlockSpec((1, D_MODEL), const),
        pl.BlockSpec((D_MODEL, LANES), const),
        pl.BlockSpec((D_MODEL, LANES), const),
        pl.BlockSpec((1, LANES), const),
        pl.BlockSpec((1, LANES), const),
    ]
    args = [att2, rw2, x2, wts["w_out"], wts["ln1_g"], wts["ln1_b"], wts["wr_hi"], wts["wr_lo"],
            wts["b_router"], counts0]
    aliases = {}
    if aliased:
        in_specs += [pl.BlockSpec(memory_space=pl.ANY)] * N_SHARED_OUT
        args += list(prev)
        aliases = {len(args) - N_SHARED_OUT + k: k for k in range(N_SHARED_OUT)}
    n_steps = rows // tm
    return pl.pallas_call(
        functools.partial(_outproj_kernel, aliased=int(aliased), n_steps=n_steps),
        grid=(n_steps,),
        in_specs=in_specs,
        out_specs=[
            pl.BlockSpec((tm, D_MODEL), lambda i: (blk0 + i, 0)),
            pl.BlockSpec((tm, TOP_K), lambda i: (blk0 + i, 0)),
            pl.BlockSpec((tm, TOP_K), lambda i: (blk0 + i, 0)),
            pl.BlockSpec((tm, TOP_K), lambda i: (blk0 + i, 0)),
            pl.BlockSpec((1, LANES), const),
        ],
        out_shape=[
            jax.ShapeDtypeStruct((total_rows, D_MODEL), f32),
            jax.ShapeDtypeStruct((total_rows, TOP_K), jnp.int32),
            jax.ShapeDtypeStruct((total_rows, TOP_K), f32),
            jax.ShapeDtypeStruct((total_rows, TOP_K), jnp.int32),
            jax.ShapeDtypeStruct((1, LANES), f32),
        ],
        scratch_shapes=[pltpu.VMEM((1, LANES), f32)],
        input_output_aliases=aliases,
        compiler_params=_cparams(("arbitrary",)),
    )(*args)


MOE_SUB = 256
MOE_NSUB = 5
MOE_SUPER = MOE_SUB * MOE_NSUB
MOE_FF_TILE = 256
MOE_DOWN_N = 512
MOE_MAX_PAD = N_EXPERTS * (MOE_SUB - 1)
X_SUBL = D_MODEL // (2 * LANES)
HALF_D = D_MODEL // 2
DMA_LOOP_UNROLL = 4


def _pack_bf16_pairs(x):
    hi = lax.bitcast_convert_type(x[:, :HALF_D].astype(bf16).astype(f32), jnp.uint32)
    lo = lax.bitcast_convert_type(x[:, HALF_D:].astype(bf16).astype(f32), jnp.uint32)
    return hi | (lo >> 16)


def _unpack_bf16_pairs(u):
    hi = lax.bitcast_convert_type(u & jnp.uint32(0xFFFF0000), f32)
    lo = lax.bitcast_convert_type(u << 16, f32)
    return hi, lo


def _routing(top_idx, rank, counts_f):
    n = top_idx.shape[0]
    n_assign = n * TOP_K
    counts = counts_f[0, :N_EXPERTS].astype(jnp.int32)
    padded = (counts + MOE_SUPER - 1) // MOE_SUPER * MOE_SUPER
    pad_end = jnp.cumsum(padded)
    start = pad_end - padded
    experts = jnp.arange(N_EXPERTS, dtype=jnp.int32)

    def lookup(table, idx):
        return jnp.sum(jnp.where(idx[..., None] == experts, table, 0), axis=-1)

    def bucket(edges, x):
        return jnp.minimum(jnp.sum((edges <= x[..., None]).astype(jnp.int32), axis=-1),
                           N_EXPERTS - 1)

    dest = (lookup(start, top_idx) + rank).astype(jnp.int32)
    n_super = (n_assign + N_EXPERTS * (MOE_SUPER - 1) + MOE_SUPER - 1) // MOE_SUPER
    s_row0 = jnp.arange(n_super, dtype=jnp.int32) * MOE_SUPER
    super_e = bucket(pad_end, s_row0)
    rows_here = jnp.clip(lookup(counts, super_e) - (s_row0 - lookup(start, super_e)), 0, MOE_SUPER)
    rows_here = jnp.where(s_row0 < pad_end[-1], rows_here, 0)
    n_sub = ((rows_here + MOE_SUB - 1) // MOE_SUB).astype(jnp.int32)
    n_used = (pad_end[-1] // MOE_SUPER).astype(jnp.int32).reshape(1)
    n_pad_e = (counts + MOE_SUB - 1) // MOE_SUB * MOE_SUB - counts
    pad_cum = jnp.cumsum(n_pad_e)
    kk = jnp.arange(MOE_MAX_PAD, dtype=jnp.int32)
    pe = bucket(pad_cum, kk)
    pad_dest = (lookup(start + counts - (pad_cum - n_pad_e), pe) + kk).astype(jnp.int32)
    pad_dest = jnp.where(kk < pad_cum[-1], pad_dest, 0)
    n_pad = pad_cum[-1].astype(jnp.int32).reshape(1)
    return dest, super_e, n_sub, n_used, pad_dest, n_pad, n_super


def _tile_copy(src_ref, src_tok, dst_ref, dst_tok, subl, sem):
    s0 = pl.multiple_of(src_tok * subl, subl)
    d0 = pl.multiple_of(dst_tok * subl, subl)
    return pltpu.make_async_copy(src_ref.at[pl.ds(s0, subl)], dst_ref.at[pl.ds(d0, subl)], sem)


def _scatter_kernel(npad_ref, dest_ref, pad_ref, h_ref, xs_hbm, stage, zero, sems, pad_sem,
                    *, tm, n_steps):
    i = pl.program_id(0)
    slot = i % 2

    def drain(sl):
        for _ in range(TOP_K):
            pltpu.make_async_copy(stage.at[sl], stage.at[sl], sems.at[sl]).wait()

    @pl.when(i >= 2)
    def _():
        drain(slot)

    packed = _pack_bf16_pairs(h_ref[...])
    for l in range(X_SUBL):
        stage[slot, pl.ds(l, tm, stride=X_SUBL), :] = packed[:, LANES * l:LANES * (l + 1)]

    def issue(t, carry):
        for jx in range(TOP_K):
            _tile_copy(stage.at[slot], t, xs_hbm, dest_ref[0, 0, jx * tm + t], X_SUBL,
                       sems.at[slot]).start()
        return carry

    lax.fori_loop(0, tm, issue, 0, unroll=DMA_LOOP_UNROLL)

    @pl.when(i == 0)
    def _():
        zero[...] = jnp.zeros_like(zero)

        def fill(k, carry):
            _tile_copy(zero, 0, xs_hbm, pad_ref[k], X_SUBL, pad_sem).start()
            return carry

        lax.fori_loop(0, npad_ref[0], fill, 0)

        def fill_wait(k, carry):
            _tile_copy(zero, 0, xs_hbm, 0, X_SUBL, pad_sem).wait()
            return carry

        lax.fori_loop(0, npad_ref[0], fill_wait, 0)

    @pl.when(i == n_steps - 1)
    def _():
        drain(slot)
        if n_steps > 1:
            drain(1 - slot)


def _scatter_rows(h_all, dest, pad_dest, n_pad, n_rows, tm):
    n = h_all.shape[0]
    assert n % tm == 0
    n_steps = n // tm
    dest_blk = dest.reshape(n_steps, tm, TOP_K).transpose(0, 2, 1).reshape(n_steps, 1, TOP_K * tm)
    kern = functools.partial(_scatter_kernel, tm=tm, n_steps=n_steps)
    return pl.pallas_call(
        kern,
        grid_spec=pltpu.PrefetchScalarGridSpec(
            num_scalar_prefetch=1,
            grid=(n_steps,),
            in_specs=[
                pl.BlockSpec((1, 1, TOP_K * tm), lambda i, npad: (i, 0, 0), memory_space=pltpu.SMEM),
                pl.BlockSpec(memory_space=pltpu.SMEM),
                pl.BlockSpec((tm, D_MODEL), lambda i, npad: (i, 0)),
            ],
            out_specs=pl.BlockSpec(memory_space=pl.ANY),
            scratch_shapes=[
                pltpu.VMEM((2, tm * X_SUBL, LANES), jnp.uint32),
                pltpu.VMEM((X_SUBL, LANES), jnp.uint32),
                pltpu.SemaphoreType.DMA((2,)),
                pltpu.SemaphoreType.DMA(()),
            ],
        ),
        out_shape=jax.ShapeDtypeStruct((n_rows * X_SUBL, LANES), jnp.uint32),
        compiler_params=_cparams(("arbitrary",)),
    )(n_pad, dest_blk, pad_dest, h_all)


def _expert_kernel(se_ref, nsub_ref, nused_ref, x_ref, wg_ref, wl_ref, bg_ref, bl_ref, wd_ref,
                   bd_ref, o_ref, xb_scr, acc_scr, *, nf):
    s = pl.program_id(0)
    j = pl.program_id(1)
    n_sub = nsub_ref[s]

    def step(m):
        rows = slice(0, m)

        @pl.when(j == 0)
        def _():
            for l in range(X_SUBL):
                hi, lo = _unpack_bf16_pairs(x_ref[pl.ds(l, m, stride=X_SUBL), :])
                xb_scr[rows, LANES * l:LANES * (l + 1)] = hi.astype(bf16)
                xb_scr[rows, HALF_D + LANES * l:HALF_D + LANES * (l + 1)] = lo.astype(bf16)
            acc_scr[rows, :] = jnp.broadcast_to(bd_ref[...], (m, D_MODEL))

        xb = xb_scr[rows, :]
        hg = jnp.dot(xb, wg_ref[...].astype(bf16), preferred_element_type=f32) + bg_ref[...]
        hl = jnp.dot(xb, wl_ref[...].astype(bf16), preferred_element_type=f32) + bl_ref[...]
        glu = jnp.minimum(hg, SWIGLU_LIMIT)
        lin = jnp.clip(hl, -SWIGLU_LIMIT, SWIGLU_LIMIT)
        act = (glu * (1.0 / (1.0 + jnp.exp(-SWIGLU_ALPHA * glu))) * (lin + 1.0)).astype(bf16)
        wd = wd_ref[...].astype(bf16)
        for n0 in range(0, D_MODEL, MOE_DOWN_N):
            acc_scr[rows, n0:n0 + MOE_DOWN_N] += jnp.dot(
                act, wd[:, n0:n0 + MOE_DOWN_N], preferred_element_type=f32)

        @pl.when(j == nf - 1)
        def _():
            packed = _pack_bf16_pairs(acc_scr[rows, :])
            for l in range(X_SUBL):
                o_ref[pl.ds(l, m, stride=X_SUBL), :] = packed[:, LANES * l:LANES * (l + 1)]

    for k in range(1, MOE_NSUB + 1):
        pl.when(n_sub == k)(functools.partial(step, k * MOE_SUB))


def _experts(xs, super_e, n_sub, n_used, w_up, b_up, w_down, b_down, n_super):
    tf = MOE_FF_TILE
    nf = D_FF // tf
    last = lambda s, nu: jnp.minimum(s, nu[0] - 1)
    b_up3 = b_up.reshape(N_EXPERTS, 1, 2 * D_FF)
    b_down3 = b_down.reshape(N_EXPERTS, 1, D_MODEL)
    e_of = lambda s, se, nu: se[last(s, nu)]
    return pl.pallas_call(
        functools.partial(_expert_kernel, nf=nf),
        grid_spec=pltpu.PrefetchScalarGridSpec(
            num_scalar_prefetch=3,
            grid=(n_super, nf),
            in_specs=[
                pl.BlockSpec((MOE_SUPER * X_SUBL, LANES), lambda s, j, se, ns, nu: (last(s, nu), 0)),
                pl.BlockSpec((None, D_MODEL, tf), lambda s, j, se, ns, nu: (e_of(s, se, nu), 0, j)),
                pl.BlockSpec((None, D_MODEL, tf),
                             lambda s, j, se, ns, nu: (e_of(s, se, nu), 0, nf + j)),
                pl.BlockSpec((None, 1, tf), lambda s, j, se, ns, nu: (e_of(s, se, nu), 0, j)),
                pl.BlockSpec((None, 1, tf), lambda s, j, se, ns, nu: (e_of(s, se, nu), 0, nf + j)),
                pl.BlockSpec((None, tf, D_MODEL), lambda s, j, se, ns, nu: (e_of(s, se, nu), j, 0)),
                pl.BlockSpec((None, 1, D_MODEL), lambda s, j, se, ns, nu: (e_of(s, se, nu), 0, 0)),
            ],
            out_specs=pl.BlockSpec((MOE_SUPER * X_SUBL, LANES),
                                   lambda s, j, se, ns, nu: (last(s, nu), 0)),
            scratch_shapes=[pltpu.VMEM((MOE_SUPER, D_MODEL), bf16),
                            pltpu.VMEM((MOE_SUPER, D_MODEL), f32)],
        ),
        out_shape=jax.ShapeDtypeStruct((n_super * MOE_SUPER * X_SUBL, LANES), jnp.uint32),
        compiler_params=_cparams(("arbitrary", "arbitrary")),
    )(super_e, n_sub, n_used, xs, w_up, w_up, b_up3, b_up3, w_down, b_down3)


def _combine_kernel(dest_ref, nxt_ref, gate_ref, h_ref, rows_hbm, g_ref, b_ref, yp_ref, ys_ref,
                    buf, sems, *, tm, n_first, n_steps):
    i = pl.program_id(0)
    slot = i % 2

    def fetch(idx_ref, sl):
        def body(t, carry):
            for jx in range(TOP_K):
                _tile_copy(rows_hbm, idx_ref[0, 0, jx * tm + t], buf.at[sl, jx], t, X_SUBL,
                           sems.at[sl]).start()
            return carry
        lax.fori_loop(0, tm, body, 0, unroll=DMA_LOOP_UNROLL)

    @pl.when(i == 0)
    def _():
        fetch(dest_ref, slot)

    @pl.when(i + 1 < n_steps)
    def _():
        fetch(nxt_ref, 1 - slot)

    for jx in range(TOP_K):
        pltpu.make_async_copy(buf.at[slot, jx], buf.at[slot, jx], sems.at[slot]).wait()
    gate = gate_ref[...]
    cols_hi, cols_lo = [], []
    for l in range(X_SUBL):
        acc_hi = acc_lo = None
        for jx in range(TOP_K):
            hi, lo = _unpack_bf16_pairs(buf[slot, jx, pl.ds(l, tm, stride=X_SUBL), :])
            gj = gate[:, jx:jx + 1]
            acc_hi = gj * hi if acc_hi is None else acc_hi + gj * hi
            acc_lo = gj * lo if acc_lo is None else acc_lo + gj * lo
        cols_hi.append(acc_hi)
        cols_lo.append(acc_lo)
    y = jnp.concatenate(cols_hi + cols_lo, axis=-1)
    out = _layer_norm(DN_ALPHA * h_ref[...] + y, g_ref[...], b_ref[...])

    @pl.when(i < n_first)
    def _():
        yp_ref[...] = out

    @pl.when(i >= n_first)
    def _():
        ys_ref[...] = out


def _combine(rows_out, dest, gate, h_all, ln_g, ln_b, *, tm, n_first_rows):
    n = h_all.shape[0]
    assert n % tm == 0 and n_first_rows % tm == 0
    nblk = n // tm
    n_first = n_first_rows // tm
    dest_blk = dest.reshape(nblk, tm, TOP_K).transpose(0, 2, 1).reshape(nblk, 1, TOP_K * tm)
    kern = functools.partial(_combine_kernel, tm=tm, n_first=n_first, n_steps=nblk)
    const = lambda i: (0, 0)
    return pl.pallas_call(
        kern,
        grid=(nblk,),
        in_specs=[
            pl.BlockSpec((1, 1, TOP_K * tm), lambda i: (i, 0, 0), memory_space=pltpu.SMEM),
            pl.BlockSpec((1, 1, TOP_K * tm), lambda i: (jnp.minimum(i + 1, nblk - 1), 0, 0),
                         memory_space=pltpu.SMEM),
            pl.BlockSpec((tm, TOP_K), lambda i: (i, 0)),
            pl.BlockSpec((tm, D_MODEL), lambda i: (i, 0)),
            pl.BlockSpec(memory_space=pl.ANY),
            pl.BlockSpec((1, D_MODEL), const),
            pl.BlockSpec((1, D_MODEL), const),
        ],
        out_specs=[
            pl.BlockSpec((tm, D_MODEL), lambda i: (jnp.minimum(i, n_first - 1), 0)),
            pl.BlockSpec((tm, D_MODEL), lambda i: (jnp.maximum(i - n_first, 0), 0)),
        ],
        out_shape=[
            jax.ShapeDtypeStruct((n_first_rows, D_MODEL), f32),
            jax.ShapeDtypeStruct((n - n_first_rows, D_MODEL), f32),
        ],
        scratch_shapes=[pltpu.VMEM((2, TOP_K, tm * X_SUBL, LANES), jnp.uint32),
                        pltpu.SemaphoreType.DMA((2,))],
        compiler_params=_cparams(("arbitrary",)),
    )(dest_blk, dest_blk, gate, h_all, rows_out, ln_g, ln_b)


def _t5_bucket(rel):
    half = NUM_BUCKETS // 2
    exact = half // 2
    n = jnp.abs(rel)
    log_part = exact + (jnp.log(jnp.maximum(n, 1).astype(jnp.float32) / exact)
                        / math.log(MAX_DISTANCE / exact) * (half - exact)).astype(jnp.int32)
    log_part = jnp.minimum(log_part, half - 1)
    return jnp.where(rel > 0, half, 0) + jnp.where(n < exact, n, log_part)


def _band_bias(rel_bias):
    qi = jnp.arange(CHUNK)[:, None]
    km = jnp.arange(BAND)[None, :]
    bucket = _t5_bucket(km - WINDOW - qi)
    return jnp.transpose(rel_bias[bucket], (2, 0, 1)).astype(jnp.float32)


def _pad_cols(a, width):
    return jnp.pad(a, ((0, 0), (0, width - a.shape[-1])))


def _pair_state(s):
    b = s.shape[0]
    return s.reshape(b, N_PAIRS, 2, RW_HEAD, RW_HEAD).transpose(0, 1, 3, 2, 4).reshape(
        b, N_PAIRS, RW_HEAD, LANES)


def _unpair_state(s):
    b = s.shape[0]
    return s.reshape(b, N_PAIRS, RW_HEAD, 2, RW_HEAD).transpose(0, 1, 3, 2, 4).reshape(
        b, RW_HEADS, RW_HEAD, RW_HEAD)


def _mix_group(x, k_hist, v_hist, hist_valid, wkv0, shift0, bias, sinks3, w_in_pad, rw_wts,
               *, in_tm, attn_nc):
    b, t, _ = x.shape
    q, kv, prw = _inproj(x.reshape(b * t, D_MODEL), w_in_pad, in_tm)
    tp = -(-t // (attn_nc * CHUNK)) * (attn_nc * CHUNK)
    q4 = q.reshape(ATT_HEADS, b, t, HEAD_DIM)
    kv3 = kv.reshape(b, t, 2 * KV_W)
    prw3 = prw.reshape(b, t, RW_PAD)
    if tp != t:
        q4 = jnp.pad(q4, ((0, 0), (0, 0), (0, tp - t), (0, 0)))
        prw3 = jnp.pad(prw3, ((0, 0), (0, tp - t), (0, 0)))
    hist = jnp.concatenate([k_hist.reshape(b, WINDOW, KV_W), v_hist.reshape(b, WINDOW, KV_W)], axis=-1)
    kvfull = jnp.concatenate([hist, kv3, jnp.zeros((b, tp - t, 2 * KV_W), f32)], axis=1)
    att = _attention(q4, kvfull, bias, sinks3, nc=attn_nc, t_valid=t, hist_valid=hist_valid)
    rw, s_fin = _rwkv(prw3, _pad_cols(shift0.reshape(b, RW_PROJ), RW_PAD).reshape(b, 1, RW_PAD),
                      _pair_state(wkv0.astype(f32)), rw_wts, t_valid=t)
    new_kv = kvfull[:, t:t + WINDOW]
    new_k = new_kv[..., :KV_W].reshape(b, WINDOW, ATT_KV_HEADS, HEAD_DIM)
    new_v = new_kv[..., KV_W:].reshape(b, WINDOW, ATT_KV_HEADS, HEAD_DIM)
    shift = prw3[:, t - 1:t, :RW_PROJ]
    return (att[:, :t].reshape(b * t, ATT_W), rw[:, :t].reshape(b * t, RW_W),
            new_k, new_v, _unpair_state(s_fin), shift)


SCATTER_TM = 128
COMBINE_TM = 128


def kernel(x_prompt, x_sample, cache_k, cache_v, state_wkv, state_shift, rel_bias, w_in, attn_sinks, rw_mu, rw_w0, rw_decay_up, rw_a0, rw_iclr_up, rw_gate_up, rw_k_k, rw_k_a, rw_r_k, rw_lnx_g, rw_lnx_b, w_out, ln1_g, ln1_b, w_router, b_router, w_up, b_up, w_down, b_down, ln2_g, ln2_b):
    assert w_in.shape[0] == DEPTH == 1
    l = 0
    bp, tp_, _ = x_prompt.shape
    bs, ts, _ = x_sample.shape
    bias = _band_bias(rel_bias)
    sinks3 = attn_sinks[l].astype(f32).reshape(ATT_HEADS, 1, 1)

    w_in_pad = _pad_cols(w_in[l], IN_PAD).astype(bf16)

    def lora_rows(w, row0):
        return jnp.zeros((LORA_PAD, RW_W), f32).at[row0:row0 + w.shape[0]].set(w).astype(bf16)

    rw_wts = {
        "mu": _pad_cols(rw_mu[l].reshape(1, RW_PROJ), RW_PAD),
        "w0": rw_w0[l].reshape(1, RW_W),
        "wd": lora_rows(rw_decay_up[l], 0),
        "a0": rw_a0[l].reshape(1, RW_W),
        "wa": lora_rows(rw_iclr_up[l], DECAY_LORA),
        "wg": lora_rows(rw_gate_up[l], DECAY_LORA + ICLR_LORA),
        "k_k": rw_k_k[l].reshape(1, RW_W),
        "k_a": rw_k_a[l].reshape(1, RW_W),
        "r_k": rw_r_k[l].reshape(1, RW_W),
        "lnx_g": rw_lnx_g[l].reshape(1, RW_W),
        "lnx_b": rw_lnx_b[l].reshape(1, RW_W),
    }
    wr = _pad_cols(w_router[l], LANES)
    wr_hi = wr.astype(bf16)
    op_wts = {
        "w_out": w_out[l].astype(bf16),
        "ln1_g": ln1_g[l].reshape(1, D_MODEL),
        "ln1_b": ln1_b[l].reshape(1, D_MODEL),
        "wr_hi": wr_hi,
        "wr_lo": (wr - wr_hi.astype(f32)).astype(bf16),
        "b_router": jnp.concatenate([b_router[l].astype(f32),
                                     jnp.full((LANES - N_EXPERTS,), NEG_BIG, f32)]).reshape(1, LANES),
    }

    zero_kv = jnp.zeros((bp, WINDOW, ATT_KV_HEADS, HEAD_DIM), f32)
    att_p, rwo_p, k1, v1, w1, s1 = _mix_group(
        x_prompt, zero_kv, zero_kv, False, jnp.zeros((bp, RW_HEADS, RW_HEAD, RW_HEAD), f32),
        jnp.zeros((bp, 1, RW_PROJ), f32), bias, sinks3, w_in_pad, rw_wts,
        in_tm=min(256, bp * tp_), attn_nc=min(8, -(-tp_ // CHUNK)))
    att_s, rwo_s, k2, v2, w2, s2 = _mix_group(
        x_sample, cache_k[l], cache_v[l], True, state_wkv[l], state_shift[l], bias, sinks3,
        w_in_pad, rw_wts, in_tm=min(256, bs * ts), attn_nc=1)

    n_p, n_s = bp * tp_, bs * ts
    n_all = n_p + n_s
    tm_p, tm_s = min(256, n_p), min(128, n_s)
    *outs, counts_p = _outproj(att_p, rwo_p, x_prompt.reshape(n_p, D_MODEL), op_wts,
                               tm=tm_p, row0=0, total_rows=n_all)
    h_all, top_idx, gate, rank, counts = _outproj(
        att_s, rwo_s, x_sample.reshape(n_s, D_MODEL), op_wts,
        tm=tm_s, row0=n_p, total_rows=n_all, prev=outs, counts0=counts_p)

    dest, super_e, n_sub, n_used, pad_dest, n_pad, n_super = _routing(top_idx, rank, counts)
    xs = _scatter_rows(h_all, dest, pad_dest, n_pad, n_super * MOE_SUPER, min(SCATTER_TM, n_s))
    rows_out = _experts(xs, super_e, n_sub, n_used, w_up[l], b_up[l], w_down[l], b_down[l],
                        n_super)
    y_p, y_s = _combine(rows_out, dest, gate, h_all, ln2_g[l].reshape(1, D_MODEL),
                        ln2_b[l].reshape(1, D_MODEL), tm=min(COMBINE_TM, n_s), n_first_rows=n_p)

    return (y_p.reshape(bp, tp_, D_MODEL), y_s.reshape(bs, ts, D_MODEL),
            k1[None], v1[None], w1[None], s1[None], k2[None], v2[None], w2[None], s2[None])
```

```python
import functools
import math

import jax
import jax.numpy as jnp
from jax import lax
from jax.experimental import pallas as pl
from jax.experimental.pallas import tpu as pltpu

f32 = jnp.float32
bf16 = jnp.bfloat16

D_MODEL = 2048
CHUNK = 64
ATT_HEADS = 16
ATT_KV_HEADS = 2
HEAD_DIM = 64
ATT_GROUP = ATT_HEADS // ATT_KV_HEADS
ATT_W = ATT_HEADS * HEAD_DIM
KV_W = ATT_KV_HEADS * HEAD_DIM
ATT_PROJ = ATT_W + 2 * KV_W
WINDOW = 128
BAND = WINDOW + CHUNK
NUM_BUCKETS = 32
MAX_DISTANCE = 128
RW_HEAD = 64
RW_W = 1024
RW_HEADS = RW_W // RW_HEAD
DECAY_LORA = 96
ICLR_LORA = 96
GATE_LORA = 128
RW_PROJ = 3 * RW_W + DECAY_LORA + ICLR_LORA + GATE_LORA
GN_EPS = 64e-5
LN_EPS = 1e-5
N_EXPERTS = 32
TOP_K = 4
D_FF = D_MODEL
SWIGLU_LIMIT = 7.0
SWIGLU_ALPHA = 1.702
DEPTH = 1
DN_ALPHA = (2 * DEPTH) ** 0.25

LANES = 128
VMEM_LIMIT = 56 * 1024 * 1024

LORA_W = DECAY_LORA + ICLR_LORA + GATE_LORA
LORA_PAD = -(-LORA_W // LANES) * LANES
RW_PAD = 3 * RW_W + LORA_PAD
IN_PAD = ATT_PROJ + RW_PAD
N_PAIRS = RW_HEADS // 2
NEG_BIG = -1e30


def _cparams(sem):
    return pltpu.CompilerParams(dimension_semantics=sem, vmem_limit_bytes=VMEM_LIMIT)


def _inproj_kernel(x_ref, w_ref, q_ref, kv_ref, rw_ref):
    acc = jnp.dot(x_ref[...].astype(bf16), w_ref[...], preferred_element_type=f32)
    scale = HEAD_DIM ** -0.5
    for h in range(ATT_HEADS):
        q_ref[h] = (acc[:, h * HEAD_DIM:(h + 1) * HEAD_DIM] * scale).astype(bf16)
    kv_ref[...] = acc[:, ATT_W:ATT_PROJ]
    rw_ref[...] = acc[:, ATT_PROJ:IN_PAD]


def _inproj(x2, w_pad, tm):
    rows = x2.shape[0]
    assert rows % tm == 0
    return pl.pallas_call(
        _inproj_kernel,
        grid=(rows // tm,),
        in_specs=[
            pl.BlockSpec((tm, D_MODEL), lambda i: (i, 0)),
            pl.BlockSpec((D_MODEL, IN_PAD), lambda i: (0, 0), pipeline_mode=pl.Buffered(1)),
        ],
        out_specs=[
            pl.BlockSpec((ATT_HEADS, tm, HEAD_DIM), lambda i: (0, i, 0)),
            pl.BlockSpec((tm, 2 * KV_W), lambda i: (i, 0)),
            pl.BlockSpec((tm, RW_PAD), lambda i: (i, 0)),
        ],
        out_shape=[
            jax.ShapeDtypeStruct((ATT_HEADS, rows, HEAD_DIM), bf16),
            jax.ShapeDtypeStruct((rows, 2 * KV_W), f32),
            jax.ShapeDtypeStruct((rows, RW_PAD), f32),
        ],
        compiler_params=_cparams(("parallel",)),
    )(x2, w_pad)


ATT_UNROLL = 4


def _attn_kernel(q_ref, hist_ref, kva_ref, kvb_ref, kvm_ref, bias_ref, sink_ref, o_ref, kvbuf,
                 *, nc, t_valid, hist_valid):
    j = pl.program_id(1)
    first = j == 0
    kvbuf[0:CHUNK] = jnp.where(first, hist_ref[0:CHUNK], kva_ref[...]).astype(bf16)
    kvbuf[CHUNK:WINDOW] = jnp.where(first, hist_ref[CHUNK:WINDOW], kvb_ref[...]).astype(bf16)
    kvbuf[WINDOW:WINDOW + nc * CHUNK] = kvm_ref[...].astype(bf16)
    m_idx = lax.broadcasted_iota(jnp.int32, (1, 1, BAND), 2)
    for c0 in range(0, nc, ATT_UNROLL):
        items = [(c, g) for c in range(c0, min(c0 + ATT_UNROLL, nc)) for g in range(ATT_KV_HEADS)]
        bands = {c: kvbuf[c * CHUNK:c * CHUNK + BAND, :] for c, _ in items}
        valid = {}
        for c in bands:
            idx = (j * nc + c) * CHUNK + m_idx
            v = idx - WINDOW < t_valid
            valid[c] = v if hist_valid else jnp.logical_and(v, idx >= WINDOW)
        s = [lax.dot_general(
                q_ref[g * ATT_GROUP:(g + 1) * ATT_GROUP, c * CHUNK:(c + 1) * CHUNK, :].reshape(
                    ATT_GROUP * CHUNK, HEAD_DIM),
                bands[c][:, g * HEAD_DIM:(g + 1) * HEAD_DIM],
                (((1,), (1,)), ((), ())), preferred_element_type=f32) for c, g in items]
        s = [jnp.where(valid[c], s[i].reshape(ATT_GROUP, CHUNK, BAND)
                       + bias_ref[g * ATT_GROUP:(g + 1) * ATT_GROUP], NEG_BIG)
             for i, (c, g) in enumerate(items)]
        sk = [sink_ref[g * ATT_GROUP:(g + 1) * ATT_GROUP] for _, g in items]
        m = [jnp.maximum(jnp.max(s[i], axis=-1, keepdims=True), sk[i]) for i in range(len(items))]
        p = [jnp.exp(s[i] - m[i]) for i in range(len(items))]
        den = [jnp.sum(p[i], axis=-1, keepdims=True) + jnp.exp(sk[i] - m[i])
               for i in range(len(items))]
        o = [jnp.dot(p[i].reshape(ATT_GROUP * CHUNK, BAND).astype(bf16),
                     bands[c][:, KV_W + g * HEAD_DIM:KV_W + (g + 1) * HEAD_DIM],
                     preferred_element_type=f32).reshape(ATT_GROUP, CHUNK, HEAD_DIM) / den[i]
             for i, (c, g) in enumerate(items)]
        for c in bands:
            heads = [o[i][h] for i, (ci, _) in enumerate(items) if ci == c for h in range(ATT_GROUP)]
            o_ref[c * CHUNK:(c + 1) * CHUNK, :] = jnp.concatenate(heads, axis=-1).astype(bf16)


def _attention(q4, hist, kv, bias, sinks3, *, nc, t_valid, hist_valid):
    _, b, tp, _ = q4.shape
    assert tp % (nc * CHUNK) == 0 and kv.shape[1] == tp
    nblk = tp // (nc * CHUNK)
    kern = functools.partial(_attn_kernel, nc=nc, t_valid=t_valid, hist_valid=hist_valid)
    return pl.pallas_call(
        kern,
        grid=(b, nblk),
        in_specs=[
            pl.BlockSpec((ATT_HEADS, None, nc * CHUNK, HEAD_DIM), lambda bi, j: (0, bi, j, 0)),
            pl.BlockSpec((None, WINDOW, 2 * KV_W), lambda bi, j: (bi, 0, 0)),
            pl.BlockSpec((None, CHUNK, 2 * KV_W), lambda bi, j: (bi, jnp.maximum(j * nc - 2, 0), 0)),
            pl.BlockSpec((None, CHUNK, 2 * KV_W), lambda bi, j: (bi, jnp.maximum(j * nc - 1, 0), 0)),
            pl.BlockSpec((None, nc * CHUNK, 2 * KV_W), lambda bi, j: (bi, j, 0)),
            pl.BlockSpec((ATT_HEADS, CHUNK, BAND), lambda bi, j: (0, 0, 0)),
            pl.BlockSpec((ATT_HEADS, 1, 1), lambda bi, j: (0, 0, 0)),
        ],
        out_specs=pl.BlockSpec((None, nc * CHUNK, ATT_W), lambda bi, j: (bi, j, 0)),
        out_shape=jax.ShapeDtypeStruct((b, tp, ATT_W), bf16),
        scratch_shapes=[pltpu.VMEM(((nc + 2) * CHUNK, 2 * KV_W), bf16)],
        compiler_params=_cparams(("parallel", "parallel")),
    )(q4, hist, kv, kv, kv, bias, sinks3)


N_OPS_BF16 = 7
N_OPS_F32 = 4


def _rwkv_step_kernel(p_ref, shift0_ref, s0_ref, mu_ref, w0_ref, wd_ref, a0_ref, wa_ref, wg_ref,
                      kk_ref, ka_ref, rk_ref, lng_ref, lnb_ref, o_ref, sfin_ref,
                      s_scr, last_scr, opb_scr, opf_scr, cl_scr, *, t_valid, n_steps, nch):
    c = pl.program_id(1)
    L = CHUNK
    R = nch * L

    @pl.when(c == 0)
    def _():
        s_scr[...] = s0_ref[...]
        last_scr[...] = shift0_ref[...]
        opb_scr[...] = jnp.zeros_like(opb_scr)
        opf_scr[...] = jnp.zeros_like(opf_scr)
        cl_scr[...] = jnp.zeros_like(cl_scr)

    kq, rq, bt, kt, bh, kh, vb = range(N_OPS_BF16)
    g_l = [jnp.exp(cl_scr[ch]) for ch in range(nch)]

    lane = lax.broadcasted_iota(jnp.int32, (1, LANES), 1)
    lo_half = lane < RW_HEAD
    rr = lax.broadcasted_iota(jnp.int32, (LANES, LANES), 0)
    cc = lax.broadcasted_iota(jnp.int32, (LANES, LANES), 1)
    same_head = (rr // RW_HEAD) == (cc // RW_HEAD)
    r4 = lax.broadcasted_iota(jnp.int32, (2 * LANES, 2 * LANES), 0)
    c4 = lax.broadcasted_iota(jnp.int32, (2 * LANES, 2 * LANES), 1)
    ones_bd = jnp.where((r4 // RW_HEAD) == (c4 // RW_HEAD), 1.0, 0.0).astype(bf16)
    rs = [slice(L * ch, L * (ch + 1)) for ch in range(nch)]
    sl = [slice(LANES * t, LANES * (t + 1)) for t in range(N_PAIRS)]
    n_quad = RW_W // (2 * LANES)

    def seg_sums(xs):
        n = len(xs)
        x = jnp.concatenate(xs, axis=0) if n > 1 else xs[0]
        hi = x.astype(bf16)
        lo = (x - hi.astype(f32)).astype(bf16)
        both = jnp.concatenate([hi, lo], axis=0)
        m = 2 * n * R
        tiles = jnp.concatenate([both[:, 2 * LANES * t:2 * LANES * (t + 1)] for t in range(n_quad)],
                                axis=0)
        res = jnp.dot(tiles, ones_bd, preferred_element_type=f32)
        y = jnp.concatenate([res[m * t:m * (t + 1)] for t in range(n_quad)], axis=1)
        y = y[:n * R] + y[n * R:]
        return [y[i * R:(i + 1) * R] for i in range(n)]

    def sigmoid(z):
        return 1.0 / (1.0 + jnp.exp(-z))

    p = p_ref[...]
    row = lax.broadcasted_iota(jnp.int32, (R, 1), 0)
    shifted = jnp.where(row == 0, last_scr[...], pltpu.roll(p, 1, axis=0))
    last_scr[...] = p[R - 1:R, :]
    xm = p + (shifted - p) * mu_ref[...]
    r = xm[:, 0:RW_W]
    k = xm[:, RW_W:2 * RW_W]
    v = xm[:, 2 * RW_W:3 * RW_W]
    tail = xm[:, 3 * RW_W:RW_PAD]
    w_log = w0_ref[...] + jnp.dot(jnp.tanh(tail).astype(bf16), wd_ref[...],
                                  preferred_element_type=f32)
    a = sigmoid(a0_ref[...] + jnp.dot(tail.astype(bf16), wa_ref[...], preferred_element_type=f32))
    g = jnp.dot(sigmoid(tail).astype(bf16), wg_ref[...], preferred_element_type=f32)

    def prep_mid():
        z = -w_log
        softplus = jnp.maximum(z, 0.0) + jnp.log(1.0 + jnp.exp(-jnp.abs(z)))
        ld = -jnp.exp(-softplus - 0.5)
        kk = k * kk_ref[...]
        k_mod = k * (1.0 + (a - 1.0) * ka_ref[...])
        nrm2, bonus_s = seg_sums([kk * kk, r * k_mod * rk_ref[...]])
        kk = kk / jnp.maximum(jnp.sqrt(nrm2), 1e-12)
        b = kk * a
        if t_valid % R != 0:
            live = (jnp.minimum(c, n_steps - 1) * R + row) < t_valid
            ld = jnp.where(live, ld, 0.0)
            b = jnp.where(live, b, 0.0)
            k_mod = jnp.where(live, k_mod, 0.0)
        h1 = ld.astype(bf16)
        r1 = ld - h1.astype(f32)
        h2 = r1.astype(bf16)
        h3 = (r1 - h2.astype(f32)).astype(bf16)
        ti = lax.broadcasted_iota(jnp.int32, (L, 3 * L), 0)
        si = lax.broadcasted_iota(jnp.int32, (L, 3 * L), 1) % L
        tri3 = jnp.where(si <= ti, 1.0, 0.0).astype(bf16)
        cums = [jnp.dot(tri3, jnp.concatenate([h1[rs[ch]], h2[rs[ch]], h3[rs[ch]]], axis=0),
                        preferred_element_type=f32) for ch in range(nch)]
        return kk, k_mod, b, ld, cums, bonus_s

    def prep_tail(kk, k_mod, b, ld, cums, bonus_s):
        cum = jnp.concatenate(cums, axis=0) if nch > 1 else cums[0]
        cum_ls = [cums[ch][L - 1:L, :] for ch in range(nch)]
        cum_l = (jnp.concatenate([jnp.broadcast_to(x, (L, RW_W)) for x in cum_ls], axis=0)
                 if nch > 1 else cum_ls[0])
        g_inv = jnp.exp(-cum)
        g_rest = jnp.exp(cum_l - cum)
        n_rq = r * jnp.exp(cum)
        ops_b = [(kk * jnp.exp(cum - ld)).astype(bf16), n_rq.astype(bf16),
                 (b * g_inv).astype(bf16), (k_mod * g_inv).astype(bf16),
                 (b * g_rest).astype(bf16), (k_mod * g_rest).astype(bf16), v.astype(bf16)]
        return ops_b, [n_rq, v, g, bonus_s], cum_ls

    def bd(x):
        zero = jnp.zeros_like(x)
        return jnp.concatenate([jnp.where(lo_half, x, zero), jnp.where(lo_half, zero, x)], axis=0)

    def mm(x, y):
        return jnp.dot(x, y, preferred_element_type=f32)

    def mm_nt(x, y):
        return lax.dot_general(x, y, (((1,), (1,)), ((), ())), preferred_element_type=f32)

    def mm_tn(x, y):
        return lax.dot_general(x, y, (((0,), (0,)), ((), ())), preferred_element_type=f32)

    tt = lax.broadcasted_iota(jnp.int32, (L, LANES), 0)
    ss = lax.broadcasted_iota(jnp.int32, (L, LANES), 1) % RW_HEAD
    strict = ss < tt
    incl = ss <= tt
    eye_pair = jnp.where(ss == tt, 1.0, 0.0).astype(f32)

    items = [(ch, t) for ch in range(nch) for t in range(N_PAIRS)]
    I = range(len(items))

    def tile(slot, i):
        ch, t = items[i]
        return opb_scr[slot, rs[ch], sl[t]]

    a_all = [mm_nt(jnp.concatenate([tile(kq, i), tile(rq, i)], axis=0),
                   jnp.concatenate([bd(tile(bt, i)), bd(tile(kt, i))], axis=0)) for i in I]
    a_bk = [jnp.where(strict, a_all[i][:L, :LANES], 0.0) for i in I]
    a_kk = [jnp.where(strict, a_all[i][:L, LANES:], 0.0).astype(bf16) for i in I]
    a_rb = [jnp.where(incl, a_all[i][L:, :LANES], 0.0).astype(bf16) for i in I]
    a_rk = [jnp.where(incl, a_all[i][L:, LANES:], 0.0).astype(bf16) for i in I]
    bd_v = [bd(tile(vb, i)) for i in I]
    akv = [mm(a_kk[i], bd_v[i]).astype(bf16) for i in I]
    w_inv = [eye_pair - a_bk[i] for i in I]
    pw = [a_bk[i].astype(bf16) for i in I]
    pw_bd = [bd(pw[i]) for i in I]
    for it in range(5):
        pw = [mm(pw[i], pw_bd[i]).astype(bf16) for i in I]
        pw_bd = [bd(pw[i]) for i in I]
        w_inv = [w_inv[i] + mm(w_inv[i].astype(bf16), pw_bd[i]) for i in I]
        if it == 2:
            mid = prep_mid()
    qu = [mm(w_inv[i].astype(bf16), jnp.concatenate([bd(tile(kq, i)), bd(akv[i])], axis=1))
          for i in I]
    q_m = [qu[i][:, :LANES].astype(bf16) for i in I]
    u_m = [qu[i][:, LANES:].astype(bf16) for i in I]
    m_full = [mm_tn(q_m[i], tile(bh, i)) for i in I]
    neg_m = [jnp.where(same_head, -m_full[i], 0.0).astype(bf16) for i in I]
    c_full = [mm_tn(jnp.concatenate([tile(vb, i), u_m[i]], axis=0),
                    jnp.concatenate([tile(kh, i), -tile(bh, i)], axis=0)) for i in I]
    go = [mm(a_rb[i], jnp.concatenate([bd(q_m[i]), bd(u_m[i])], axis=1)) for i in I]
    o_rk = [mm(a_rk[i], bd_v[i]) for i in I]
    new_b, new_f, new_cl = prep_tail(*mid)
    g_m = [(opf_scr[0, rs[items[i][0]], sl[items[i][1]]] - go[i][:, :LANES]).astype(bf16)
           for i in I]
    o_in = [o_rk[i] - go[i][:, LANES:] for i in I]
    c_pair = [jnp.where(lo_half, c_full[i][:RW_HEAD], c_full[i][RW_HEAD:]) for i in I]
    s_cur = [s_scr[t] for t in range(N_PAIRS)]
    o_rows = []
    for ch in range(nch):
        ii = [ch * N_PAIRS + t for t in range(N_PAIRS)]
        s_b = [s.astype(bf16) for s in s_cur]
        o_rows.append(jnp.concatenate(
            [mm_nt(g_m[i], bd(s_b[t])) + o_in[i] for t, i in enumerate(ii)], axis=1))
        s_upd = [mm(s_b[t], neg_m[i]) for t, i in enumerate(ii)]
        s_cur = [s_cur[t] * g_l[ch][:, sl[t]] + s_upd[t] + c_pair[i] for t, i in enumerate(ii)]
    for t in range(N_PAIRS):
        s_scr[t] = s_cur[t]

    o = jnp.concatenate(o_rows, axis=0) if nch > 1 else o_rows[0]
    (o_sum,) = seg_sums([o])
    d = o - o_sum * (1.0 / RW_HEAD)
    (d2,) = seg_sums([d * d])
    on = d * lax.rsqrt(d2 * (1.0 / RW_HEAD) + GN_EPS) * lng_ref[...] + lnb_ref[...]
    o_ref[...] = ((on + opf_scr[3] * opf_scr[1]) * opf_scr[2]).astype(bf16)

    for i in range(N_OPS_BF16):
        opb_scr[i] = new_b[i]
    for i in range(N_OPS_F32):
        opf_scr[i] = new_f[i]
    for ch in range(nch):
        cl_scr[ch] = new_cl[ch]

    @pl.when(c == n_steps)
    def _():
        sfin_ref[...] = s_scr[...]


RWKV_CHUNKS_PER_STEP = 2


def _rwkv(prw, shift0, s0_pair, wts, *, t_valid):
    b, tp, _ = prw.shape
    nch = RWKV_CHUNKS_PER_STEP if tp % (RWKV_CHUNKS_PER_STEP * CHUNK) == 0 else 1
    rows = nch * CHUNK
    n_steps = tp // rows
    kern = functools.partial(_rwkv_step_kernel, t_valid=t_valid, n_steps=n_steps, nch=nch)
    const2 = lambda bi, c: (0, 0)
    row_spec = pl.BlockSpec((1, RW_W), const2)
    return pl.pallas_call(
        kern,
        grid=(b, n_steps + 1),
        in_specs=[
            pl.BlockSpec((None, rows, RW_PAD), lambda bi, c: (bi, jnp.minimum(c, n_steps - 1), 0)),
            pl.BlockSpec((None, 1, RW_PAD), lambda bi, c: (bi, 0, 0)),
            pl.BlockSpec((None, N_PAIRS, RW_HEAD, LANES), lambda bi, c: (bi, 0, 0, 0)),
            pl.BlockSpec((1, RW_PAD), const2),
            row_spec,
            pl.BlockSpec((LORA_PAD, RW_W), const2),
            row_spec,
            pl.BlockSpec((LORA_PAD, RW_W), const2),
            pl.BlockSpec((LORA_PAD, RW_W), const2),
            row_spec, row_spec, row_spec, row_spec, row_spec,
        ],
        out_specs=[
            pl.BlockSpec((None, rows, RW_W), lambda bi, c: (bi, jnp.maximum(c - 1, 0), 0)),
            pl.BlockSpec((None, N_PAIRS, RW_HEAD, LANES), lambda bi, c: (bi, 0, 0, 0)),
        ],
        out_shape=[
            jax.ShapeDtypeStruct((b, tp, RW_W), bf16),
            jax.ShapeDtypeStruct((b, N_PAIRS, RW_HEAD, LANES), f32),
        ],
        scratch_shapes=[pltpu.VMEM((N_PAIRS, RW_HEAD, LANES), f32), pltpu.VMEM((1, RW_PAD), f32),
                        pltpu.VMEM((N_OPS_BF16, rows, RW_W), bf16),
                        pltpu.VMEM((N_OPS_F32, rows, RW_W), f32),
                        pltpu.VMEM((nch, 1, RW_W), f32)],
        compiler_params=_cparams(("parallel", "arbitrary")),
    )(prw, shift0, s0_pair, wts["mu"], wts["w0"], wts["wd"], wts["a0"], wts["wa"], wts["wg"],
      wts["k_k"], wts["k_a"], wts["r_k"], wts["lnx_g"], wts["lnx_b"])


def _layer_norm(z, g, b):
    mu = jnp.mean(z, axis=-1, keepdims=True)
    d = z - mu
    var = jnp.mean(d * d, axis=-1, keepdims=True)
    return d * lax.rsqrt(var + LN_EPS) * g + b


N_SHARED_OUT = 4
OUTPROJ_PARTS = 2


def _outproj_kernel(*refs, aliased, n_steps):
    att_ref, rw_ref, x_ref, wo_ref, g_ref, b_ref, wrh_ref, wrl_ref, br_ref, cnt0_ref = refs[:10]
    h_ref, idx_ref, gate_ref, rank_ref, cnt_ref, carry = refs[10 + N_SHARED_OUT * aliased:]
    step = pl.program_id(0)

    @pl.when(step == 0)
    def _():
        carry[...] = cnt0_ref[...]

    tm = x_ref.shape[0]
    n_part = OUTPROJ_PARTS if tm % (OUTPROJ_PARTS * 8) == 0 else 1
    pm = tm // n_part
    parts = [slice(q * pm, (q + 1) * pm) for q in range(n_part)]
    mix = [jnp.dot(att_ref[r, :], wo_ref[0:ATT_W], preferred_element_type=f32)
           + jnp.dot(rw_ref[r, :], wo_ref[ATT_W:ATT_W + RW_W], preferred_element_type=f32)
           for r in parts]
    h = [_layer_norm(DN_ALPHA * x_ref[r, :] + mix[q], g_ref[...], b_ref[...])
         for q, r in enumerate(parts)]
    for q, r in enumerate(parts):
        h_ref[r, :] = h[q]
    hh = [x.astype(bf16) for x in h]
    hl = [(h[q] - hh[q].astype(f32)).astype(bf16) for q in range(n_part)]
    logits = [(jnp.dot(hh[q], wrh_ref[...], preferred_element_type=f32)
               + jnp.dot(hl[q], wrh_ref[...], preferred_element_type=f32)
               + jnp.dot(hh[q], wrl_ref[...], preferred_element_type=f32)) + br_ref[...]
              for q in range(n_part)]
    lane = lax.broadcasted_iota(jnp.int32, (pm, LANES), 1).astype(f32)
    ti = lax.broadcasted_iota(jnp.int32, (pm, pm), 0)
    si = lax.broadcasted_iota(jnp.int32, (pm, pm), 1)
    before = jnp.where(si < ti, 1.0, 0.0).astype(bf16)
    counts = carry[...]
    for q, r in enumerate(parts):
        vals, idxs = [], []
        cur = logits[q]
        for _ in range(TOP_K):
            m = jnp.max(cur, axis=-1, keepdims=True)
            i = jnp.min(jnp.where(cur == m, lane, float(LANES)), axis=-1, keepdims=True)
            vals.append(m)
            idxs.append(i)
            cur = jnp.where(lane == i, -jnp.inf, cur)
        es = [jnp.exp(vv - vals[0]) for vv in vals]
        tot = es[0] + es[1] + es[2] + es[3]
        idx_ref[r, :] = jnp.concatenate(idxs, axis=-1).astype(jnp.int32)
        gate_ref[r, :] = jnp.concatenate([e / tot for e in es], axis=-1)
        hits = [jnp.where(lane == i, 1.0, 0.0) for i in idxs]
        multi = hits[0] + hits[1] + hits[2] + hits[3]
        base = counts + jnp.dot(before, multi.astype(bf16), preferred_element_type=f32)
        ranks = [jnp.sum(hh_ * base, axis=-1, keepdims=True) for hh_ in hits]
        rank_ref[r, :] = jnp.concatenate(ranks, axis=-1).astype(jnp.int32)
        counts = counts + jnp.sum(multi, axis=0, keepdims=True)
    carry[...] = counts

    @pl.when(step == n_steps - 1)
    def _():
        cnt_ref[...] = carry[...]


def _outproj(att2, rw2, x2, wts, *, tm, row0, total_rows, prev=None, counts0=None):
    rows = x2.shape[0]
    assert rows % tm == 0 and row0 % tm == 0
    blk0 = row0 // tm
    aliased = prev is not None
    if counts0 is None:
        counts0 = jnp.zeros((1, LANES), f32)
    const = lambda i: (0, 0)
    in_specs = [
        pl.BlockSpec((tm, ATT_W), lambda i: (i, 0)),
        pl.BlockSpec((tm, RW_W), lambda i: (i, 0)),
        pl.BlockSpec((tm, D_MODEL), lambda i: (i, 0)),
        pl.BlockSpec((D_MODEL, D_MODEL), const, pipeline_mode=pl.Buffered(1)),
        pl.BlockSpec((1, D_MODEL), const),
        pl.BlockSpec((1, D_MODEL), const),
        pl.BlockSpec((D_MODEL, LANES), const),
        pl.BlockSpec((D_MODEL, LANES), const),
        pl.BlockSpec((1, LANES), const),
        pl.BlockSpec((1, LANES), const),
    ]
    args = [att2, rw2, x2, wts["w_out"], wts["ln1_g"], wts["ln1_b"], wts["wr_hi"], wts["wr_lo"],
            wts["b_router"], counts0]
    aliases = {}
    if aliased:
        in_specs += [pl.BlockSpec(memory_space=pl.ANY)] * N_SHARED_OUT
        args += list(prev)
        aliases = {len(args) - N_SHARED_OUT + k: k for k in range(N_SHARED_OUT)}
    n_steps = rows // tm
    return pl.pallas_call(
        functools.partial(_outproj_kernel, aliased=int(aliased), n_steps=n_steps),
        grid=(n_steps,),
        in_specs=in_specs,
        out_specs=[
            pl.BlockSpec((tm, D_MODEL), lambda i: (blk0 + i, 0)),
            pl.BlockSpec((tm, TOP_K), lambda i: (blk0 + i, 0)),
            pl.BlockSpec((tm, TOP_K), lambda i: (blk0 + i, 0)),
            pl.BlockSpec((tm, TOP_K), lambda i: (blk0 + i, 0)),
            pl.BlockSpec((1, LANES), const),
        ],
        out_shape=[
            jax.ShapeDtypeStruct((total_rows, D_MODEL), f32),
            jax.ShapeDtypeStruct((total_rows, TOP_K), jnp.int32),
            jax.ShapeDtypeStruct((total_rows, TOP_K), f32),
            jax.ShapeDtypeStruct((total_rows, TOP_K), jnp.int32),
            jax.ShapeDtypeStruct((1, LANES), f32),
        ],
        scratch_shapes=[pltpu.VMEM((1, LANES), f32)],
        input_output_aliases=aliases,
        compiler_params=_cparams(("arbitrary",)),
    )(*args)


MOE_SUB = 256
MOE_NSUB = 5
MOE_SUPER = MOE_SUB * MOE_NSUB
MOE_FF_TILE = 256
MOE_DOWN_N = 512
MOE_MAX_PAD = N_EXPERTS * (MOE_SUB - 1)
X_SUBL = D_MODEL // (2 * LANES)
HALF_D = D_MODEL // 2
DMA_LOOP_UNROLL = 4


def _pack_bf16_pairs(x):
    hi = lax.bitcast_convert_type(x[:, :HALF_D].astype(bf16).astype(f32), jnp.uint32)
    lo = lax.bitcast_convert_type(x[:, HALF_D:].astype(bf16).astype(f32), jnp.uint32)
    return hi | (lo >> 16)


def _unpack_bf16_pairs(u):
    hi = lax.bitcast_convert_type(u & jnp.uint32(0xFFFF0000), f32)
    lo = lax.bitcast_convert_type(u << 16, f32)
    return hi, lo


def _routing(top_idx, rank, counts_f):
    n = top_idx.shape[0]
    n_assign = n * TOP_K
    counts = counts_f[0, :N_EXPERTS].astype(jnp.int32)
    padded = (counts + MOE_SUPER - 1) // MOE_SUPER * MOE_SUPER
    pad_end = jnp.cumsum(padded)
    start = pad_end - padded
    experts = jnp.arange(N_EXPERTS, dtype=jnp.int32)

    def lookup(table, idx):
        return jnp.sum(jnp.where(idx[..., None] == experts, table, 0), axis=-1)

    def bucket(edges, x):
        return jnp.minimum(jnp.sum((edges <= x[..., None]).astype(jnp.int32), axis=-1),
                           N_EXPERTS - 1)

    dest = (lookup(start, top_idx) + rank).astype(jnp.int32)
    n_super = (n_assign + N_EXPERTS * (MOE_SUPER - 1) + MOE_SUPER - 1) // MOE_SUPER
    s_row0 = jnp.arange(n_super, dtype=jnp.int32) * MOE_SUPER
    super_e = bucket(pad_end, s_row0)
    rows_here = jnp.clip(lookup(counts, super_e) - (s_row0 - lookup(start, super_e)), 0, MOE_SUPER)
    rows_here = jnp.where(s_row0 < pad_end[-1], rows_here, 0)
    n_sub = ((rows_here + MOE_SUB - 1) // MOE_SUB).astype(jnp.int32)
    n_used = (pad_end[-1] // MOE_SUPER).astype(jnp.int32).reshape(1)
    n_pad_e = (counts + MOE_SUB - 1) // MOE_SUB * MOE_SUB - counts
    pad_cum = jnp.cumsum(n_pad_e)
    kk = jnp.arange(MOE_MAX_PAD, dtype=jnp.int32)
    pe = bucket(pad_cum, kk)
    pad_dest = (lookup(start + counts - (pad_cum - n_pad_e), pe) + kk).astype(jnp.int32)
    pad_dest = jnp.where(kk < pad_cum[-1], pad_dest, 0)
    n_pad = pad_cum[-1].astype(jnp.int32).reshape(1)
    return dest, super_e, n_sub, n_used, pad_dest, n_pad, n_super


def _tile_copy(src_ref, src_tok, dst_ref, dst_tok, subl, sem):
    s0 = pl.multiple_of(src_tok * subl, subl)
    d0 = pl.multiple_of(dst_tok * subl, subl)
    return pltpu.make_async_copy(src_ref.at[pl.ds(s0, subl)], dst_ref.at[pl.ds(d0, subl)], sem)


def _scatter_kernel(npad_ref, dest_ref, pad_ref, h_ref, xs_hbm, stage, zero, sems, pad_sem,
                    *, tm, n_steps):
    i = pl.program_id(0)
    slot = i % 2

    def drain(sl):
        for _ in range(TOP_K):
            pltpu.make_async_copy(stage.at[sl], stage.at[sl], sems.at[sl]).wait()

    @pl.when(i >= 2)
    def _():
        drain(slot)

    packed = _pack_bf16_pairs(h_ref[...])
    for l in range(X_SUBL):
        stage[slot, pl.ds(l, tm, stride=X_SUBL), :] = packed[:, LANES * l:LANES * (l + 1)]

    def issue(t, carry):
        for jx in range(TOP_K):
            _tile_copy(stage.at[slot], t, xs_hbm, dest_ref[0, 0, jx * tm + t], X_SUBL,
                       sems.at[slot]).start()
        return carry

    lax.fori_loop(0, tm, issue, 0, unroll=DMA_LOOP_UNROLL)

    @pl.when(i == 0)
    def _():
        zero[...] = jnp.zeros_like(zero)

        def fill(k, carry):
            _tile_copy(zero, 0, xs_hbm, pad_ref[k], X_SUBL, pad_sem).start()
            return carry

        lax.fori_loop(0, npad_ref[0], fill, 0)

        def fill_wait(k, carry):
            _tile_copy(zero, 0, xs_hbm, 0, X_SUBL, pad_sem).wait()
            return carry

        lax.fori_loop(0, npad_ref[0], fill_wait, 0)

    @pl.when(i == n_steps - 1)
    def _():
        drain(slot)
        if n_steps > 1:
            drain(1 - slot)


def _scatter_rows(h_all, dest, pad_dest, n_pad, n_rows, tm):
    n = h_all.shape[0]
    assert n % tm == 0
    n_steps = n // tm
    dest_blk = dest.reshape(n_steps, tm, TOP_K).transpose(0, 2, 1).reshape(n_steps, 1, TOP_K * tm)
    kern = functools.partial(_scatter_kernel, tm=tm, n_steps=n_steps)
    return pl.pallas_call(
        kern,
        grid_spec=pltpu.PrefetchScalarGridSpec(
            num_scalar_prefetch=1,
            grid=(n_steps,),
            in_specs=[
                pl.BlockSpec((1, 1, TOP_K * tm), lambda i, npad: (i, 0, 0), memory_space=pltpu.SMEM),
                pl.BlockSpec(memory_space=pltpu.SMEM),
                pl.BlockSpec((tm, D_MODEL), lambda i, npad: (i, 0)),
            ],
            out_specs=pl.BlockSpec(memory_space=pl.ANY),
            scratch_shapes=[
                pltpu.VMEM((2, tm * X_SUBL, LANES), jnp.uint32),
                pltpu.VMEM((X_SUBL, LANES), jnp.uint32),
                pltpu.SemaphoreType.DMA((2,)),
                pltpu.SemaphoreType.DMA(()),
            ],
        ),
        out_shape=jax.ShapeDtypeStruct((n_rows * X_SUBL, LANES), jnp.uint32),
        compiler_params=_cparams(("arbitrary",)),
    )(n_pad, dest_blk, pad_dest, h_all)


def _expert_kernel(se_ref, nsub_ref, nused_ref, x_ref, wg_ref, wl_ref, bg_ref, bl_ref, wd_ref,
                   bd_ref, o_ref, xb_scr, acc_scr, *, nf):
    s = pl.program_id(0)
    j = pl.program_id(1)
    n_sub = nsub_ref[s]

    def step(m):
        rows = slice(0, m)

        @pl.when(j == 0)
        def _():
            for l in range(X_SUBL):
                hi, lo = _unpack_bf16_pairs(x_ref[pl.ds(l, m, stride=X_SUBL), :])
                xb_scr[rows, LANES * l:LANES * (l + 1)] = hi.astype(bf16)
                xb_scr[rows, HALF_D + LANES * l:HALF_D + LANES * (l + 1)] = lo.astype(bf16)
            acc_scr[rows, :] = jnp.broadcast_to(bd_ref[...], (m, D_MODEL))

        xb = xb_scr[rows, :]
        w_up = jnp.concatenate([wg_ref[...].astype(bf16), wl_ref[...].astype(bf16)], axis=1)
        h_up = jnp.dot(xb, w_up, preferred_element_type=f32)
        hg = h_up[:, :MOE_FF_TILE] + bg_ref[...]
        hl = h_up[:, MOE_FF_TILE:] + bl_ref[...]
        glu = jnp.minimum(hg, SWIGLU_LIMIT)
        lin = jnp.clip(hl, -SWIGLU_LIMIT, SWIGLU_LIMIT)
        act = (glu * (1.0 / (1.0 + jnp.exp(-SWIGLU_ALPHA * glu))) * (lin + 1.0)).astype(bf16)
        wd = wd_ref[...].astype(bf16)
        for n0 in range(0, D_MODEL, MOE_DOWN_N):
            acc_scr[rows, n0:n0 + MOE_DOWN_N] += jnp.dot(
                act, wd[:, n0:n0 + MOE_DOWN_N], preferred_element_type=f32)

        @pl.when(j == nf - 1)
        def _():
            packed = _pack_bf16_pairs(acc_scr[rows, :])
            for l in range(X_SUBL):
                o_ref[pl.ds(l, m, stride=X_SUBL), :] = packed[:, LANES * l:LANES * (l + 1)]

    for k in range(1, MOE_NSUB + 1):
        pl.when(n_sub == k)(functools.partial(step, k * MOE_SUB))


def _experts(xs, super_e, n_sub, n_used, w_up, b_up, w_down, b_down, n_super):
    tf = MOE_FF_TILE
    nf = D_FF // tf
    last = lambda s, nu: jnp.minimum(s, nu[0] - 1)
    b_up3 = b_up.reshape(N_EXPERTS, 1, 2 * D_FF)
    b_down3 = b_down.reshape(N_EXPERTS, 1, D_MODEL)
    e_of = lambda s, se, nu: se[last(s, nu)]
    return pl.pallas_call(
        functools.partial(_expert_kernel, nf=nf),
        grid_spec=pltpu.PrefetchScalarGridSpec(
            num_scalar_prefetch=3,
            grid=(n_super, nf),
            in_specs=[
                pl.BlockSpec((MOE_SUPER * X_SUBL, LANES), lambda s, j, se, ns, nu: (last(s, nu), 0)),
                pl.BlockSpec((None, D_MODEL, tf), lambda s, j, se, ns, nu: (e_of(s, se, nu), 0, j)),
                pl.BlockSpec((None, D_MODEL, tf),
                             lambda s, j, se, ns, nu: (e_of(s, se, nu), 0, nf + j)),
                pl.BlockSpec((None, 1, tf), lambda s, j, se, ns, nu: (e_of(s, se, nu), 0, j)),
                pl.BlockSpec((None, 1, tf), lambda s, j, se, ns, nu: (e_of(s, se, nu), 0, nf + j)),
                pl.BlockSpec((None, tf, D_MODEL), lambda s, j, se, ns, nu: (e_of(s, se, nu), j, 0)),
                pl.BlockSpec((None, 1, D_MODEL), lambda s, j, se, ns, nu: (e_of(s, se, nu), 0, 0)),
            ],
            out_specs=pl.BlockSpec((MOE_SUPER * X_SUBL, LANES),
                                   lambda s, j, se, ns, nu: (last(s, nu), 0)),
            scratch_shapes=[pltpu.VMEM((MOE_SUPER, D_MODEL), bf16),
                            pltpu.VMEM((MOE_SUPER, D_MODEL), f32)],
        ),
        out_shape=jax.ShapeDtypeStruct((n_super * MOE_SUPER * X_SUBL, LANES), jnp.uint32),
        compiler_params=_cparams(("arbitrary", "arbitrary")),
    )(super_e, n_sub, n_used, xs, w_up, w_up, b_up3, b_up3, w_down, b_down3)


def _combine_kernel(dest_ref, nxt_ref, gate_ref, h_ref, rows_hbm, g_ref, b_ref, yp_ref, ys_ref,
                    buf, sems, *, tm, n_first, n_steps):
    i = pl.program_id(0)
    slot = i % 2

    def fetch(idx_ref, sl):
        def body(t, carry):
            for jx in range(TOP_K):
                _tile_copy(rows_hbm, idx_ref[0, 0, jx * tm + t], buf.at[sl, jx], t, X_SUBL,
                           sems.at[sl]).start()
            return carry
        lax.fori_loop(0, tm, body, 0, unroll=DMA_LOOP_UNROLL)

    @pl.when(i == 0)
    def _():
        fetch(dest_ref, slot)

    @pl.when(i + 1 < n_steps)
    def _():
        fetch(nxt_ref, 1 - slot)

    for jx in range(TOP_K):
        pltpu.make_async_copy(buf.at[slot, jx], buf.at[slot, jx], sems.at[slot]).wait()
    gate = gate_ref[...]
    cols_hi, cols_lo = [], []
    for l in range(X_SUBL):
        acc_hi = acc_lo = None
        for jx in range(TOP_K):
            hi, lo = _unpack_bf16_pairs(buf[slot, jx, pl.ds(l, tm, stride=X_SUBL), :])
            gj = gate[:, jx:jx + 1]
            acc_hi = gj * hi if acc_hi is None else acc_hi + gj * hi
            acc_lo = gj * lo if acc_lo is None else acc_lo + gj * lo
        cols_hi.append(acc_hi)
        cols_lo.append(acc_lo)
    y = jnp.concatenate(cols_hi + cols_lo, axis=-1)
    out = _layer_norm(DN_ALPHA * h_ref[...] + y, g_ref[...], b_ref[...])

    @pl.when(i < n_first)
    def _():
        yp_ref[...] = out

    @pl.when(i >= n_first)
    def _():
        ys_ref[...] = out


def _combine(rows_out, dest, gate, h_all, ln_g, ln_b, *, tm, n_first_rows):
    n = h_all.shape[0]
    assert n % tm == 0 and n_first_rows % tm == 0
    nblk = n // tm
    n_first = n_first_rows // tm
    dest_blk = dest.reshape(nblk, tm, TOP_K).transpose(0, 2, 1).reshape(nblk, 1, TOP_K * tm)
    kern = functools.partial(_combine_kernel, tm=tm, n_first=n_first, n_steps=nblk)
    const = lambda i: (0, 0)
    return pl.pallas_call(
        kern,
        grid=(nblk,),
        in_specs=[
            pl.BlockSpec((1, 1, TOP_K * tm), lambda i: (i, 0, 0), memory_space=pltpu.SMEM),
            pl.BlockSpec((1, 1, TOP_K * tm), lambda i: (jnp.minimum(i + 1, nblk - 1), 0, 0),
                         memory_space=pltpu.SMEM),
            pl.BlockSpec((tm, TOP_K), lambda i: (i, 0)),
            pl.BlockSpec((tm, D_MODEL), lambda i: (i, 0)),
            pl.BlockSpec(memory_space=pl.ANY),
            pl.BlockSpec((1, D_MODEL), const),
            pl.BlockSpec((1, D_MODEL), const),
        ],
        out_specs=[
            pl.BlockSpec((tm, D_MODEL), lambda i: (jnp.minimum(i, n_first - 1), 0)),
            pl.BlockSpec((tm, D_MODEL), lambda i: (jnp.maximum(i - n_first, 0), 0)),
        ],
        out_shape=[
            jax.ShapeDtypeStruct((n_first_rows, D_MODEL), f32),
            jax.ShapeDtypeStruct((n - n_first_rows, D_MODEL), f32),
        ],
        scratch_shapes=[pltpu.VMEM((2, TOP_K, tm * X_SUBL, LANES), jnp.uint32),
                        pltpu.SemaphoreType.DMA((2,))],
        compiler_params=_cparams(("arbitrary",)),
    )(dest_blk, dest_blk, gate, h_all, rows_out, ln_g, ln_b)


def _t5_bucket(rel):
    half = NUM_BUCKETS // 2
    exact = half // 2
    n = jnp.abs(rel)
    log_part = exact + (jnp.log(jnp.maximum(n, 1).astype(jnp.float32) / exact)
                        / math.log(MAX_DISTANCE / exact) * (half - exact)).astype(jnp.int32)
    log_part = jnp.minimum(log_part, half - 1)
    return jnp.where(rel > 0, half, 0) + jnp.where(n < exact, n, log_part)


def _band_bias(rel_bias):
    qi = jnp.arange(CHUNK)[:, None]
    km = jnp.arange(BAND)[None, :]
    bucket = _t5_bucket(km - WINDOW - qi)
    onehot = (bucket[..., None] == jnp.arange(NUM_BUCKETS)).astype(f32)
    return jnp.einsum("imb,bh->him", onehot, rel_bias.astype(f32),
                      precision=lax.Precision.HIGHEST)


def _pad_cols(a, width):
    return jnp.pad(a, ((0, 0), (0, width - a.shape[-1])))


def _pair_state(s):
    b = s.shape[0]
    return s.reshape(b, N_PAIRS, 2, RW_HEAD, RW_HEAD).transpose(0, 1, 3, 2, 4).reshape(
        b, N_PAIRS, RW_HEAD, LANES)


def _unpair_state(s):
    b = s.shape[0]
    return s.reshape(b, N_PAIRS, RW_HEAD, 2, RW_HEAD).transpose(0, 1, 3, 2, 4).reshape(
        b, RW_HEADS, RW_HEAD, RW_HEAD)


def _mix_group(x, k_hist, v_hist, hist_valid, wkv0, shift0, bias, sinks3, w_in_pad, rw_wts,
               *, in_tm, attn_nc):
    b, t, _ = x.shape
    q, kv, prw = _inproj(x.reshape(b * t, D_MODEL), w_in_pad, in_tm)
    tp = -(-t // (attn_nc * CHUNK)) * (attn_nc * CHUNK)
    q4 = q.reshape(ATT_HEADS, b, t, HEAD_DIM)
    kv3 = kv.reshape(b, t, 2 * KV_W)
    prw3 = prw.reshape(b, t, RW_PAD)
    hist = jnp.concatenate([k_hist.reshape(b, WINDOW, KV_W), v_hist.reshape(b, WINDOW, KV_W)], axis=-1)
    if t >= WINDOW:
        new_kv = kv3[:, t - WINDOW:]
    else:
        new_kv = jnp.concatenate([hist[:, t:], kv3], axis=1)
    if tp != t:
        q4 = jnp.pad(q4, ((0, 0), (0, 0), (0, tp - t), (0, 0)))
        prw3 = jnp.pad(prw3, ((0, 0), (0, tp - t), (0, 0)))
        kv3 = jnp.pad(kv3, ((0, 0), (0, tp - t), (0, 0)))
    att = _attention(q4, hist, kv3, bias, sinks3, nc=attn_nc, t_valid=t, hist_valid=hist_valid)
    rw, s_fin = _rwkv(prw3, _pad_cols(shift0.reshape(b, RW_PROJ), RW_PAD).reshape(b, 1, RW_PAD),
                      _pair_state(wkv0.astype(f32)), rw_wts, t_valid=t)
    new_k = new_kv[..., :KV_W].reshape(b, WINDOW, ATT_KV_HEADS, HEAD_DIM)
    new_v = new_kv[..., KV_W:].reshape(b, WINDOW, ATT_KV_HEADS, HEAD_DIM)
    shift = prw3[:, t - 1:t, :RW_PROJ]
    return (att[:, :t].reshape(b * t, ATT_W), rw[:, :t].reshape(b * t, RW_W),
            new_k, new_v, _unpair_state(s_fin), shift)


SCATTER_TM = 128
COMBINE_TM = 128


def kernel(x_prompt, x_sample, cache_k, cache_v, state_wkv, state_shift, rel_bias, w_in, attn_sinks, rw_mu, rw_w0, rw_decay_up, rw_a0, rw_iclr_up, rw_gate_up, rw_k_k, rw_k_a, rw_r_k, rw_lnx_g, rw_lnx_b, w_out, ln1_g, ln1_b, w_router, b_router, w_up, b_up, w_down, b_down, ln2_g, ln2_b):
    assert w_in.shape[0] == DEPTH == 1
    l = 0
    bp, tp_, _ = x_prompt.shape
    bs, ts, _ = x_sample.shape
    bias = _band_bias(rel_bias)
    sinks3 = attn_sinks[l].astype(f32).reshape(ATT_HEADS, 1, 1)

    w_in_pad = _pad_cols(w_in[l], IN_PAD).astype(bf16)

    def lora_rows(w, row0):
        return jnp.zeros((LORA_PAD, RW_W), f32).at[row0:row0 + w.shape[0]].set(w).astype(bf16)

    rw_wts = {
        "mu": _pad_cols(rw_mu[l].reshape(1, RW_PROJ), RW_PAD),
        "w0": rw_w0[l].reshape(1, RW_W),
        "wd": lora_rows(rw_decay_up[l], 0),
        "a0": rw_a0[l].reshape(1, RW_W),
        "wa": lora_rows(rw_iclr_up[l], DECAY_LORA),
        "wg": lora_rows(rw_gate_up[l], DECAY_LORA + ICLR_LORA),
        "k_k": rw_k_k[l].reshape(1, RW_W),
        "k_a": rw_k_a[l].reshape(1, RW_W),
        "r_k": rw_r_k[l].reshape(1, RW_W),
        "lnx_g": rw_lnx_g[l].reshape(1, RW_W),
        "lnx_b": rw_lnx_b[l].reshape(1, RW_W),
    }
    wr = _pad_cols(w_router[l], LANES)
    wr_hi = wr.astype(bf16)
    op_wts = {
        "w_out": w_out[l].astype(bf16),
        "ln1_g": ln1_g[l].reshape(1, D_MODEL),
        "ln1_b": ln1_b[l].reshape(1, D_MODEL),
        "wr_hi": wr_hi,
        "wr_lo": (wr - wr_hi.astype(f32)).astype(bf16),
        "b_router": jnp.concatenate([b_router[l].astype(f32),
                                     jnp.full((LANES - N_EXPERTS,), NEG_BIG, f32)]).reshape(1, LANES),
    }

    zero_kv = jnp.zeros((bp, WINDOW, ATT_KV_HEADS, HEAD_DIM), f32)
    att_p, rwo_p, k1, v1, w1, s1 = _mix_group(
        x_prompt, zero_kv, zero_kv, False, jnp.zeros((bp, RW_HEADS, RW_HEAD, RW_HEAD), f32),
        jnp.zeros((bp, 1, RW_PROJ), f32), bias, sinks3, w_in_pad, rw_wts,
        in_tm=min(256, bp * tp_), attn_nc=min(8, -(-tp_ // CHUNK)))
    att_s, rwo_s, k2, v2, w2, s2 = _mix_group(
        x_sample, cache_k[l], cache_v[l], True, state_wkv[l], state_shift[l], bias, sinks3,
        w_in_pad, rw_wts, in_tm=min(256, bs * ts), attn_nc=1)

    n_p, n_s = bp * tp_, bs * ts
    n_all = n_p + n_s
    tm_p, tm_s = min(256, n_p), min(128, n_s)
    *outs, counts_p = _outproj(att_p, rwo_p, x_prompt.reshape(n_p, D_MODEL), op_wts,
                               tm=tm_p, row0=0, total_rows=n_all)
    h_all, top_idx, gate, rank, counts = _outproj(
        att_s, rwo_s, x_sample.reshape(n_s, D_MODEL), op_wts,
        tm=tm_s, row0=n_p, total_rows=n_all, prev=outs, counts0=counts_p)

    dest, super_e, n_sub, n_used, pad_dest, n_pad, n_super = _routing(top_idx, rank, counts)
    xs = _scatter_rows(h_all, dest, pad_dest, n_pad, n_super * MOE_SUPER, min(SCATTER_TM, n_s))
    rows_out = _experts(xs, super_e, n_sub, n_used, w_up[l], b_up[l], w_down[l], b_down[l],
                        n_super)
    y_p, y_s = _combine(rows_out, dest, gate, h_all, ln2_g[l].reshape(1, D_MODEL),
                        ln2_b[l].reshape(1, D_MODEL), tm=min(COMBINE_TM, n_s), n_first_rows=n_p)

    return (y_p.reshape(bp, tp_, D_MODEL), y_s.reshape(bs, ts, D_MODEL),
            k1[None], v1[None], w1[None], s1[None], k2[None], v2[None], w2[None], s2[None])
```

```python
import functools
import math

import jax
import jax.numpy as jnp
from jax import lax
from jax.experimental import pallas as pl
from jax.experimental.pallas import tpu as pltpu

f32 = jnp.float32
bf16 = jnp.bfloat16

D_MODEL = 2048
CHUNK = 64
ATT_HEADS = 16
ATT_KV_HEADS = 2
HEAD_DIM = 64
ATT_GROUP = ATT_HEADS // ATT_KV_HEADS
ATT_W = ATT_HEADS * HEAD_DIM
KV_W = ATT_KV_HEADS * HEAD_DIM
ATT_PROJ = ATT_W + 2 * KV_W
WINDOW = 128
BAND = WINDOW + CHUNK
NUM_BUCKETS = 32
MAX_DISTANCE = 128
RW_HEAD = 64
RW_W = 1024
RW_HEADS = RW_W // RW_HEAD
DECAY_LORA = 96
ICLR_LORA = 96
GATE_LORA = 128
RW_PROJ = 3 * RW_W + DECAY_LORA + ICLR_LORA + GATE_LORA
GN_EPS = 64e-5
LN_EPS = 1e-5
N_EXPERTS = 32
TOP_K = 4
D_FF = D_MODEL
SWIGLU_LIMIT = 7.0
SWIGLU_ALPHA = 1.702
DEPTH = 1
DN_ALPHA = (2 * DEPTH) ** 0.25

LANES = 128
VMEM_LIMIT = 56 * 1024 * 1024

LORA_W = DECAY_LORA + ICLR_LORA + GATE_LORA
LORA_PAD = -(-LORA_W // LANES) * LANES
RW_PAD = 3 * RW_W + LORA_PAD
IN_PAD = ATT_PROJ + RW_PAD
N_PAIRS = RW_HEADS // 2
NEG_BIG = -1e30


def _cparams(sem):
    return pltpu.CompilerParams(dimension_semantics=sem, vmem_limit_bytes=VMEM_LIMIT)


def _inproj_kernel(x_ref, w_ref, q_ref, kv_ref, rw_ref):
    acc = jnp.dot(x_ref[...].astype(bf16), w_ref[...], preferred_element_type=f32)
    scale = HEAD_DIM ** -0.5
    for h in range(ATT_HEADS):
        q_ref[h] = (acc[:, h * HEAD_DIM:(h + 1) * HEAD_DIM] * scale).astype(bf16)
    kv_ref[...] = acc[:, ATT_W:ATT_PROJ]
    rw_ref[...] = acc[:, ATT_PROJ:IN_PAD]


def _inproj(x2, w_pad, tm):
    rows = x2.shape[0]
    assert rows % tm == 0
    return pl.pallas_call(
        _inproj_kernel,
        grid=(rows // tm,),
        in_specs=[
            pl.BlockSpec((tm, D_MODEL), lambda i: (i, 0)),
            pl.BlockSpec((D_MODEL, IN_PAD), lambda i: (0, 0), pipeline_mode=pl.Buffered(1)),
        ],
        out_specs=[
            pl.BlockSpec((ATT_HEADS, None, tm, HEAD_DIM), lambda i: (0, 0, i, 0)),
            pl.BlockSpec((tm, 2 * KV_W), lambda i: (i, 0)),
            pl.BlockSpec((tm, RW_PAD), lambda i: (i, 0)),
        ],
        out_shape=[
            jax.ShapeDtypeStruct((ATT_HEADS, 1, rows, HEAD_DIM), bf16),
            jax.ShapeDtypeStruct((rows, 2 * KV_W), f32),
            jax.ShapeDtypeStruct((rows, RW_PAD), f32),
        ],
        compiler_params=_cparams(("parallel",)),
    )(x2, w_pad)


ATT_UNROLL = 4


def _attn_kernel(q_ref, hist_ref, kva_ref, kvb_ref, kvm_ref, bias_ref, sink_ref, o_ref, kvbuf,
                 *, nc, t_valid, hist_valid):
    j = pl.program_id(1)
    first = j == 0
    kvbuf[0:CHUNK] = jnp.where(first, hist_ref[0:CHUNK], kva_ref[...]).astype(bf16)
    kvbuf[CHUNK:WINDOW] = jnp.where(first, hist_ref[CHUNK:WINDOW], kvb_ref[...]).astype(bf16)
    kvbuf[WINDOW:WINDOW + nc * CHUNK] = kvm_ref[...].astype(bf16)
    m_idx = lax.broadcasted_iota(jnp.int32, (1, 1, BAND), 2)
    for c0 in range(0, nc, ATT_UNROLL):
        items = [(c, g) for c in range(c0, min(c0 + ATT_UNROLL, nc)) for g in range(ATT_KV_HEADS)]
        bands = {c: kvbuf[c * CHUNK:c * CHUNK + BAND, :] for c, _ in items}
        valid = {}
        for c in bands:
            idx = (j * nc + c) * CHUNK + m_idx
            v = idx - WINDOW < t_valid
            valid[c] = v if hist_valid else jnp.logical_and(v, idx >= WINDOW)
        s = [lax.dot_general(
                q_ref[g * ATT_GROUP:(g + 1) * ATT_GROUP, c * CHUNK:(c + 1) * CHUNK, :].reshape(
                    ATT_GROUP * CHUNK, HEAD_DIM),
                bands[c][:, g * HEAD_DIM:(g + 1) * HEAD_DIM],
                (((1,), (1,)), ((), ())), preferred_element_type=f32) for c, g in items]
        s = [jnp.where(valid[c], s[i].reshape(ATT_GROUP, CHUNK, BAND)
                       + bias_ref[g * ATT_GROUP:(g + 1) * ATT_GROUP], NEG_BIG)
             for i, (c, g) in enumerate(items)]
        sk = [sink_ref[g * ATT_GROUP:(g + 1) * ATT_GROUP] for _, g in items]
        m = [jnp.maximum(jnp.max(s[i], axis=-1, keepdims=True), sk[i]) for i in range(len(items))]
        p = [jnp.exp(s[i] - m[i]) for i in range(len(items))]
        den = [jnp.sum(p[i], axis=-1, keepdims=True) + jnp.exp(sk[i] - m[i])
               for i in range(len(items))]
        o = [jnp.dot(p[i].reshape(ATT_GROUP * CHUNK, BAND).astype(bf16),
                     bands[c][:, KV_W + g * HEAD_DIM:KV_W + (g + 1) * HEAD_DIM],
                     preferred_element_type=f32).reshape(ATT_GROUP, CHUNK, HEAD_DIM) / den[i]
             for i, (c, g) in enumerate(items)]
        for c in bands:
            heads = [o[i][h] for i, (ci, _) in enumerate(items) if ci == c for h in range(ATT_GROUP)]
            o_ref[c * CHUNK:(c + 1) * CHUNK, :] = jnp.concatenate(heads, axis=-1).astype(bf16)


def _attention(q4, hist, kv, bias, sinks3, *, nc, t_valid, hist_valid):
    _, b, tp, _ = q4.shape
    assert tp % (nc * CHUNK) == 0 and kv.shape[1] == tp
    nblk = tp // (nc * CHUNK)
    kern = functools.partial(_attn_kernel, nc=nc, t_valid=t_valid, hist_valid=hist_valid)
    return pl.pallas_call(
        kern,
        grid=(b, nblk),
        in_specs=[
            pl.BlockSpec((ATT_HEADS, None, nc * CHUNK, HEAD_DIM), lambda bi, j: (0, bi, j, 0)),
            pl.BlockSpec((None, WINDOW, 2 * KV_W), lambda bi, j: (bi, 0, 0)),
            pl.BlockSpec((None, CHUNK, 2 * KV_W), lambda bi, j: (bi, jnp.maximum(j * nc - 2, 0), 0)),
            pl.BlockSpec((None, CHUNK, 2 * KV_W), lambda bi, j: (bi, jnp.maximum(j * nc - 1, 0), 0)),
            pl.BlockSpec((None, nc * CHUNK, 2 * KV_W), lambda bi, j: (bi, j, 0)),
            pl.BlockSpec((ATT_HEADS, CHUNK, BAND), lambda bi, j: (0, 0, 0)),
            pl.BlockSpec((ATT_HEADS, 1, 1), lambda bi, j: (0, 0, 0)),
        ],
        out_specs=pl.BlockSpec((None, nc * CHUNK, ATT_W), lambda bi, j: (bi, j, 0)),
        out_shape=jax.ShapeDtypeStruct((b, tp, ATT_W), bf16),
        scratch_shapes=[pltpu.VMEM(((nc + 2) * CHUNK, 2 * KV_W), bf16)],
        compiler_params=_cparams(("parallel", "parallel")),
    )(q4, hist, kv, kv, kv, bias, sinks3)


N_OPS_BF16 = 7
N_OPS_F32 = 4


def _rwkv_step_kernel(p_ref, shift0_ref, s0_ref, mu_ref, w0_ref, wd_ref, a0_ref, wa_ref, wg_ref,
                      kk_ref, ka_ref, rk_ref, lng_ref, lnb_ref, o_ref, sfin_ref,
                      s_scr, last_scr, opb_scr, opf_scr, cl_scr, *, t_valid, n_steps, nch):
    c = pl.program_id(1)
    L = CHUNK
    R = nch * L

    @pl.when(c == 0)
    def _():
        s_scr[...] = s0_ref[...]
        last_scr[...] = shift0_ref[...]
        opb_scr[...] = jnp.zeros_like(opb_scr)
        opf_scr[...] = jnp.zeros_like(opf_scr)
        cl_scr[...] = jnp.zeros_like(cl_scr)

    kq, rq, bt, kt, bh, kh, vb = range(N_OPS_BF16)
    g_l = [jnp.exp(cl_scr[ch]) for ch in range(nch)]

    lane = lax.broadcasted_iota(jnp.int32, (1, LANES), 1)
    lo_half = lane < RW_HEAD
    rr = lax.broadcasted_iota(jnp.int32, (LANES, LANES), 0)
    cc = lax.broadcasted_iota(jnp.int32, (LANES, LANES), 1)
    same_head = (rr // RW_HEAD) == (cc // RW_HEAD)
    r4 = lax.broadcasted_iota(jnp.int32, (2 * LANES, 2 * LANES), 0)
    c4 = lax.broadcasted_iota(jnp.int32, (2 * LANES, 2 * LANES), 1)
    ones_bd = jnp.where((r4 // RW_HEAD) == (c4 // RW_HEAD), 1.0, 0.0).astype(bf16)
    rs = [slice(L * ch, L * (ch + 1)) for ch in range(nch)]
    sl = [slice(LANES * t, LANES * (t + 1)) for t in range(N_PAIRS)]
    n_quad = RW_W // (2 * LANES)

    def seg_sums(xs):
        n = len(xs)
        x = jnp.concatenate(xs, axis=0) if n > 1 else xs[0]
        hi = x.astype(bf16)
        lo = (x - hi.astype(f32)).astype(bf16)
        both = jnp.concatenate([hi, lo], axis=0)
        m = 2 * n * R
        tiles = jnp.concatenate([both[:, 2 * LANES * t:2 * LANES * (t + 1)] for t in range(n_quad)],
                                axis=0)
        res = jnp.dot(tiles, ones_bd, preferred_element_type=f32)
        y = jnp.concatenate([res[m * t:m * (t + 1)] for t in range(n_quad)], axis=1)
        y = y[:n * R] + y[n * R:]
        return [y[i * R:(i + 1) * R] for i in range(n)]

    def sigmoid(z):
        return 1.0 / (1.0 + jnp.exp(-z))

    p = p_ref[...]
    row = lax.broadcasted_iota(jnp.int32, (R, 1), 0)
    shifted = jnp.where(row == 0, last_scr[...], pltpu.roll(p, 1, axis=0))
    last_scr[...] = p[R - 1:R, :]
    xm = p + (shifted - p) * mu_ref[...]
    r = xm[:, 0:RW_W]
    k = xm[:, RW_W:2 * RW_W]
    v = xm[:, 2 * RW_W:3 * RW_W]
    tail = xm[:, 3 * RW_W:RW_PAD]
    w_log = w0_ref[...] + jnp.dot(jnp.tanh(tail).astype(bf16), wd_ref[...],
                                  preferred_element_type=f32)
    a = sigmoid(a0_ref[...] + jnp.dot(tail.astype(bf16), wa_ref[...], preferred_element_type=f32))
    g = jnp.dot(sigmoid(tail).astype(bf16), wg_ref[...], preferred_element_type=f32)

    def prep_mid():
        z = -w_log
        softplus = jnp.maximum(z, 0.0) + jnp.log(1.0 + jnp.exp(-jnp.abs(z)))
        ld = -jnp.exp(-softplus - 0.5)
        kk = k * kk_ref[...]
        k_mod = k * (1.0 + (a - 1.0) * ka_ref[...])
        nrm2, bonus_s = seg_sums([kk * kk, r * k_mod * rk_ref[...]])
        kk = kk / jnp.maximum(jnp.sqrt(nrm2), 1e-12)
        b = kk * a
        if t_valid % R != 0:
            live = (jnp.minimum(c, n_steps - 1) * R + row) < t_valid
            ld = jnp.where(live, ld, 0.0)
            b = jnp.where(live, b, 0.0)
            k_mod = jnp.where(live, k_mod, 0.0)
        h1 = ld.astype(bf16)
        r1 = ld - h1.astype(f32)
        h2 = r1.astype(bf16)
        h3 = (r1 - h2.astype(f32)).astype(bf16)
        ti = lax.broadcasted_iota(jnp.int32, (L, 3 * L), 0)
        si = lax.broadcasted_iota(jnp.int32, (L, 3 * L), 1) % L
        tri3 = jnp.where(si <= ti, 1.0, 0.0).astype(bf16)
        cums = [jnp.dot(tri3, jnp.concatenate([h1[rs[ch]], h2[rs[ch]], h3[rs[ch]]], axis=0),
                        preferred_element_type=f32) for ch in range(nch)]
        return kk, k_mod, b, ld, cums, bonus_s

    def prep_tail(kk, k_mod, b, ld, cums, bonus_s):
        cum = jnp.concatenate(cums, axis=0) if nch > 1 else cums[0]
        cum_ls = [cums[ch][L - 1:L, :] for ch in range(nch)]
        cum_l = (jnp.concatenate([jnp.broadcast_to(x, (L, RW_W)) for x in cum_ls], axis=0)
                 if nch > 1 else cum_ls[0])
        g_inv = jnp.exp(-cum)
        g_rest = jnp.exp(cum_l - cum)
        n_rq = r * jnp.exp(cum)
        ops_b = [(kk * jnp.exp(cum - ld)).astype(bf16), n_rq.astype(bf16),
                 (b * g_inv).astype(bf16), (k_mod * g_inv).astype(bf16),
                 (b * g_rest).astype(bf16), (k_mod * g_rest).astype(bf16), v.astype(bf16)]
        return ops_b, [n_rq, v, g, bonus_s], cum_ls

    def bd(x):
        zero = jnp.zeros_like(x)
        return jnp.concatenate([jnp.where(lo_half, x, zero), jnp.where(lo_half, zero, x)], axis=0)

    def mm(x, y):
        return jnp.dot(x, y, preferred_element_type=f32)

    def mm_nt(x, y):
        return lax.dot_general(x, y, (((1,), (1,)), ((), ())), preferred_element_type=f32)

    def mm_tn(x, y):
        return lax.dot_general(x, y, (((0,), (0,)), ((), ())), preferred_element_type=f32)

    tt = lax.broadcasted_iota(jnp.int32, (L, LANES), 0)
    ss = lax.broadcasted_iota(jnp.int32, (L, LANES), 1) % RW_HEAD
    strict = ss < tt
    incl = ss <= tt
    eye_pair = jnp.where(ss == tt, 1.0, 0.0).astype(f32)

    items = [(ch, t) for ch in range(nch) for t in range(N_PAIRS)]
    I = range(len(items))

    def tile(slot, i):
        ch, t = items[i]
        return opb_scr[slot, rs[ch], sl[t]]

    a_all = [mm_nt(jnp.concatenate([tile(kq, i), tile(rq, i)], axis=0),
                   jnp.concatenate([bd(tile(bt, i)), bd(tile(kt, i))], axis=0)) for i in I]
    a_bk = [jnp.where(strict, a_all[i][:L, :LANES], 0.0) for i in I]
    a_kk = [jnp.where(strict, a_all[i][:L, LANES:], 0.0).astype(bf16) for i in I]
    a_rb = [jnp.where(incl, a_all[i][L:, :LANES], 0.0).astype(bf16) for i in I]
    a_rk = [jnp.where(incl, a_all[i][L:, LANES:], 0.0).astype(bf16) for i in I]
    bd_v = [bd(tile(vb, i)) for i in I]
    akv = [mm(a_kk[i], bd_v[i]).astype(bf16) for i in I]
    w_inv = [eye_pair - a_bk[i] for i in I]
    pw = [a_bk[i].astype(bf16) for i in I]
    pw_bd = [bd(pw[i]) for i in I]
    for it in range(5):
        pw = [mm(pw[i], pw_bd[i]).astype(bf16) for i in I]
        pw_bd = [bd(pw[i]) for i in I]
        w_inv = [w_inv[i] + mm(w_inv[i].astype(bf16), pw_bd[i]) for i in I]
        if it == 2:
            mid = prep_mid()
    qu = [mm(w_inv[i].astype(bf16), jnp.concatenate([bd(tile(kq, i)), bd(akv[i])], axis=1))
          for i in I]
    q_m = [qu[i][:, :LANES].astype(bf16) for i in I]
    u_m = [qu[i][:, LANES:].astype(bf16) for i in I]
    m_full = [mm_tn(q_m[i], tile(bh, i)) for i in I]
    neg_m = [jnp.where(same_head, -m_full[i], 0.0).astype(bf16) for i in I]
    c_full = [mm_tn(jnp.concatenate([tile(vb, i), u_m[i]], axis=0),
                    jnp.concatenate([tile(kh, i), -tile(bh, i)], axis=0)) for i in I]
    go = [mm(a_rb[i], jnp.concatenate([bd(q_m[i]), bd(u_m[i])], axis=1)) for i in I]
    o_rk = [mm(a_rk[i], bd_v[i]) for i in I]
    new_b, new_f, new_cl = prep_tail(*mid)
    g_m = [(opf_scr[0, rs[items[i][0]], sl[items[i][1]]] - go[i][:, :LANES]).astype(bf16)
           for i in I]
    o_in = [o_rk[i] - go[i][:, LANES:] for i in I]
    c_pair = [jnp.where(lo_half, c_full[i][:RW_HEAD], c_full[i][RW_HEAD:]) for i in I]
    s_cur = [s_scr[t] for t in range(N_PAIRS)]
    o_rows = []
    for ch in range(nch):
        ii = [ch * N_PAIRS + t for t in range(N_PAIRS)]
        s_b = [s.astype(bf16) for s in s_cur]
        o_rows.append(jnp.concatenate(
            [mm_nt(g_m[i], bd(s_b[t])) + o_in[i] for t, i in enumerate(ii)], axis=1))
        s_upd = [mm(s_b[t], neg_m[i]) for t, i in enumerate(ii)]
        s_cur = [s_cur[t] * g_l[ch][:, sl[t]] + s_upd[t] + c_pair[i] for t, i in enumerate(ii)]
    for t in range(N_PAIRS):
        s_scr[t] = s_cur[t]

    o = jnp.concatenate(o_rows, axis=0) if nch > 1 else o_rows[0]
    (o_sum,) = seg_sums([o])
    d = o - o_sum * (1.0 / RW_HEAD)
    (d2,) = seg_sums([d * d])
    on = d * lax.rsqrt(d2 * (1.0 / RW_HEAD) + GN_EPS) * lng_ref[...] + lnb_ref[...]
    o_ref[...] = ((on + opf_scr[3] * opf_scr[1]) * opf_scr[2]).astype(bf16)

    for i in range(N_OPS_BF16):
        opb_scr[i] = new_b[i]
    for i in range(N_OPS_F32):
        opf_scr[i] = new_f[i]
    for ch in range(nch):
        cl_scr[ch] = new_cl[ch]

    @pl.when(c == n_steps)
    def _():
        sfin_ref[...] = s_scr[...]


RWKV_CHUNKS_PER_STEP = 2


def _rwkv(prw, shift0, s0_pair, wts, *, t_valid):
    b, tp, _ = prw.shape
    nch = RWKV_CHUNKS_PER_STEP if tp % (RWKV_CHUNKS_PER_STEP * CHUNK) == 0 else 1
    rows = nch * CHUNK
    n_steps = tp // rows
    kern = functools.partial(_rwkv_step_kernel, t_valid=t_valid, n_steps=n_steps, nch=nch)
    const2 = lambda bi, c: (0, 0)
    row_spec = pl.BlockSpec((1, RW_W), const2)
    return pl.pallas_call(
        kern,
        grid=(b, n_steps + 1),
        in_specs=[
            pl.BlockSpec((None, rows, RW_PAD), lambda bi, c: (bi, jnp.minimum(c, n_steps - 1), 0)),
            pl.BlockSpec((None, 1, RW_PAD), lambda bi, c: (bi, 0, 0)),
            pl.BlockSpec((None, N_PAIRS, RW_HEAD, LANES), lambda bi, c: (bi, 0, 0, 0)),
            pl.BlockSpec((1, RW_PAD), const2),
            row_spec,
            pl.BlockSpec((LORA_PAD, RW_W), const2),
            row_spec,
            pl.BlockSpec((LORA_PAD, RW_W), const2),
            pl.BlockSpec((LORA_PAD, RW_W), const2),
            row_spec, row_spec, row_spec, row_spec, row_spec,
        ],
        out_specs=[
            pl.BlockSpec((None, rows, RW_W), lambda bi, c: (bi, jnp.maximum(c - 1, 0), 0)),
            pl.BlockSpec((None, N_PAIRS, RW_HEAD, LANES), lambda bi, c: (bi, 0, 0, 0)),
        ],
        out_shape=[
            jax.ShapeDtypeStruct((b, tp, RW_W), bf16),
            jax.ShapeDtypeStruct((b, N_PAIRS, RW_HEAD, LANES), f32),
        ],
        scratch_shapes=[pltpu.VMEM((N_PAIRS, RW_HEAD, LANES), f32), pltpu.VMEM((1, RW_PAD), f32),
                        pltpu.VMEM((N_OPS_BF16, rows, RW_W), bf16),
                        pltpu.VMEM((N_OPS_F32, rows, RW_W), f32),
                        pltpu.VMEM((nch, 1, RW_W), f32)],
        compiler_params=_cparams(("parallel", "arbitrary")),
    )(prw, shift0, s0_pair, wts["mu"], wts["w0"], wts["wd"], wts["a0"], wts["wa"], wts["wg"],
      wts["k_k"], wts["k_a"], wts["r_k"], wts["lnx_g"], wts["lnx_b"])


def _layer_norm(z, g, b):
    mu = jnp.mean(z, axis=-1, keepdims=True)
    d = z - mu
    var = jnp.mean(d * d, axis=-1, keepdims=True)
    return d * lax.rsqrt(var + LN_EPS) * g + b


N_SHARED_OUT = 4
OUTPROJ_PARTS = 2


def _outproj_kernel(*refs, aliased, n_steps):
    att_ref, rw_ref, x_ref, wo_ref, g_ref, b_ref, wrh_ref, wrl_ref, br_ref, cnt0_ref = refs[:10]
    h_ref, idx_ref, gate_ref, rank_ref, cnt_ref, carry = refs[10 + N_SHARED_OUT * aliased:]
    step = pl.program_id(0)

    @pl.when(step == 0)
    def _():
        carry[...] = cnt0_ref[...]

    tm = x_ref.shape[0]
    n_part = OUTPROJ_PARTS if tm % (OUTPROJ_PARTS * 8) == 0 else 1
    pm = tm // n_part
    parts = [slice(q * pm, (q + 1) * pm) for q in range(n_part)]
    mix = [jnp.dot(att_ref[r, :], wo_ref[0:ATT_W], preferred_element_type=f32)
           + jnp.dot(rw_ref[r, :], wo_ref[ATT_W:ATT_W + RW_W], preferred_element_type=f32)
           for r in parts]
    h = [_layer_norm(DN_ALPHA * x_ref[r, :] + mix[q], g_ref[...], b_ref[...])
         for q, r in enumerate(parts)]
    for q, r in enumerate(parts):
        h_ref[r, :] = h[q]
    hh = [x.astype(bf16) for x in h]
    hl = [(h[q] - hh[q].astype(f32)).astype(bf16) for q in range(n_part)]
    logits = [(jnp.dot(hh[q], wrh_ref[...], preferred_element_type=f32)
               + jnp.dot(hl[q], wrh_ref[...], preferred_element_type=f32)
               + jnp.dot(hh[q], wrl_ref[...], preferred_element_type=f32)) + br_ref[...]
              for q in range(n_part)]
    lane = lax.broadcasted_iota(jnp.int32, (pm, LANES), 1).astype(f32)
    ti = lax.broadcasted_iota(jnp.int32, (pm, pm), 0)
    si = lax.broadcasted_iota(jnp.int32, (pm, pm), 1)
    before = jnp.where(si < ti, 1.0, 0.0).astype(bf16)
    counts = carry[...]
    for q, r in enumerate(parts):
        vals, idxs = [], []
        cur = logits[q]
        for _ in range(TOP_K):
            m = jnp.max(cur, axis=-1, keepdims=True)
            i = jnp.min(jnp.where(cur == m, lane, float(LANES)), axis=-1, keepdims=True)
            vals.append(m)
            idxs.append(i)
            cur = jnp.where(lane == i, -jnp.inf, cur)
        es = [jnp.exp(vv - vals[0]) for vv in vals]
        tot = es[0] + es[1] + es[2] + es[3]
        idx_ref[r, :] = jnp.concatenate(idxs, axis=-1).astype(jnp.int32)
        gate_ref[r, :] = jnp.concatenate([e / tot for e in es], axis=-1)
        hits = [jnp.where(lane == i, 1.0, 0.0) for i in idxs]
        multi = hits[0] + hits[1] + hits[2] + hits[3]
        base = counts + jnp.dot(before, multi.astype(bf16), preferred_element_type=f32)
        ranks = [jnp.sum(hh_ * base, axis=-1, keepdims=True) for hh_ in hits]
        rank_ref[r, :] = jnp.concatenate(ranks, axis=-1).astype(jnp.int32)
        counts = counts + jnp.sum(multi, axis=0, keepdims=True)
    carry[...] = counts

    @pl.when(step == n_steps - 1)
    def _():
        cnt_ref[...] = carry[...]


def _outproj(att2, rw2, x2, wts, *, tm, row0, total_rows, prev=None, counts0=None):
    rows = x2.shape[0]
    assert rows % tm == 0 and row0 % tm == 0
    blk0 = row0 // tm
    aliased = prev is not None
    if counts0 is None:
        counts0 = jnp.zeros((1, LANES), f32)
    const = lambda i: (0, 0)
    in_specs = [
        pl.BlockSpec((tm, ATT_W), lambda i: (i, 0)),
        pl.BlockSpec((tm, RW_W), lambda i: (i, 0)),
        pl.BlockSpec((tm, D_MODEL), lambda i: (i, 0)),
        pl.BlockSpec((D_MODEL, D_MODEL), const, pipeline_mode=pl.Buffered(1)),
        pl.BlockSpec((1, D_MODEL), const),
        pl.BlockSpec((1, D_MODEL), const),
        pl.BlockSpec((D_MODEL, LANES), const),
        pl.BlockSpec((D_MODEL, LANES), const),
        pl.BlockSpec((1, LANES), const),
        pl.BlockSpec((1, LANES), const),
    ]
    args = [att2, rw2, x2, wts["w_out"], wts["ln1_g"], wts["ln1_b"], wts["wr_hi"], wts["wr_lo"],
            wts["b_router"], counts0]
    aliases = {}
    if aliased:
        in_specs += [pl.BlockSpec(memory_space=pl.ANY)] * N_SHARED_OUT
        args += list(prev)
        aliases = {len(args) - N_SHARED_OUT + k: k for k in range(N_SHARED_OUT)}
    n_steps = rows // tm
    return pl.pallas_call(
        functools.partial(_outproj_kernel, aliased=int(aliased), n_steps=n_steps),
        grid=(n_steps,),
        in_specs=in_specs,
        out_specs=[
            pl.BlockSpec((tm, D_MODEL), lambda i: (blk0 + i, 0)),
            pl.BlockSpec((tm, TOP_K), lambda i: (blk0 + i, 0)),
            pl.BlockSpec((tm, TOP_K), lambda i: (blk0 + i, 0)),
            pl.BlockSpec((tm, TOP_K), lambda i: (blk0 + i, 0)),
            pl.BlockSpec((1, LANES), const),
        ],
        out_shape=[
            jax.ShapeDtypeStruct((total_rows, D_MODEL), f32),
            jax.ShapeDtypeStruct((total_rows, TOP_K), jnp.int32),
            jax.ShapeDtypeStruct((total_rows, TOP_K), f32),
            jax.ShapeDtypeStruct((total_rows, TOP_K), jnp.int32),
            jax.ShapeDtypeStruct((1, LANES), f32),
        ],
        scratch_shapes=[pltpu.VMEM((1, LANES), f32)],
        input_output_aliases=aliases,
        compiler_params=_cparams(("arbitrary",)),
    )(*args)


MOE_SUB = 256
MOE_NSUB = 5
MOE_SUPER = MOE_SUB * MOE_NSUB
MOE_FF_TILE = 256
MOE_DOWN_N = 512
MOE_MAX_PAD = N_EXPERTS * (MOE_SUB - 1)
X_SUBL = D_MODEL // (2 * LANES)
HALF_D = D_MODEL // 2
DMA_LOOP_UNROLL = 4


def _pack_bf16_pairs(x):
    hi = lax.bitcast_convert_type(x[:, :HALF_D].astype(bf16).astype(f32), jnp.uint32)
    lo = lax.bitcast_convert_type(x[:, HALF_D:].astype(bf16).astype(f32), jnp.uint32)
    return hi | (lo >> 16)


def _unpack_bf16_pairs(u):
    hi = lax.bitcast_convert_type(u & jnp.uint32(0xFFFF0000), f32)
    lo = lax.bitcast_convert_type(u << 16, f32)
    return hi, lo


def _routing(top_idx, rank, counts_f):
    n = top_idx.shape[0]
    n_assign = n * TOP_K
    counts = counts_f[0, :N_EXPERTS].astype(jnp.int32)
    padded = (counts + MOE_SUPER - 1) // MOE_SUPER * MOE_SUPER
    pad_end = jnp.cumsum(padded)
    start = pad_end - padded
    experts = jnp.arange(N_EXPERTS, dtype=jnp.int32)

    def lookup(table, idx):
        return jnp.sum(jnp.where(idx[..., None] == experts, table, 0), axis=-1)

    def bucket(edges, x):
        return jnp.minimum(jnp.sum((edges <= x[..., None]).astype(jnp.int32), axis=-1),
                           N_EXPERTS - 1)

    dest = (lookup(start, top_idx) + rank).astype(jnp.int32)
    n_super = (n_assign + N_EXPERTS * (MOE_SUPER - 1) + MOE_SUPER - 1) // MOE_SUPER
    s_row0 = jnp.arange(n_super, dtype=jnp.int32) * MOE_SUPER
    super_e = bucket(pad_end, s_row0)
    rows_here = jnp.clip(lookup(counts, super_e) - (s_row0 - lookup(start, super_e)), 0, MOE_SUPER)
    rows_here = jnp.where(s_row0 < pad_end[-1], rows_here, 0)
    n_sub = ((rows_here + MOE_SUB - 1) // MOE_SUB).astype(jnp.int32)
    n_used = (pad_end[-1] // MOE_SUPER).astype(jnp.int32).reshape(1)
    n_pad_e = (counts + MOE_SUB - 1) // MOE_SUB * MOE_SUB - counts
    pad_cum = jnp.cumsum(n_pad_e)
    kk = jnp.arange(MOE_MAX_PAD, dtype=jnp.int32)
    pe = bucket(pad_cum, kk)
    pad_dest = (lookup(start + counts - (pad_cum - n_pad_e), pe) + kk).astype(jnp.int32)
    pad_dest = jnp.where(kk < pad_cum[-1], pad_dest, 0)
    n_pad = pad_cum[-1].astype(jnp.int32).reshape(1)
    return dest, super_e, n_sub, n_used, pad_dest, n_pad, n_super


def _tile_copy(src_ref, src_tok, dst_ref, dst_tok, subl, sem):
    s0 = pl.multiple_of(src_tok * subl, subl)
    d0 = pl.multiple_of(dst_tok * subl, subl)
    return pltpu.make_async_copy(src_ref.at[pl.ds(s0, subl)], dst_ref.at[pl.ds(d0, subl)], sem)


def _scatter_kernel(npad_ref, dest_ref, pad_ref, h_ref, xs_hbm, stage, zero, sems, pad_sem,
                    *, tm, n_steps):
    i = pl.program_id(0)
    slot = i % 2

    def drain(sl):
        for _ in range(TOP_K):
            pltpu.make_async_copy(stage.at[sl], stage.at[sl], sems.at[sl]).wait()

    @pl.when(i >= 2)
    def _():
        drain(slot)

    packed = _pack_bf16_pairs(h_ref[...])
    for l in range(X_SUBL):
        stage[slot, pl.ds(l, tm, stride=X_SUBL), :] = packed[:, LANES * l:LANES * (l + 1)]

    def issue(t, carry):
        for jx in range(TOP_K):
            _tile_copy(stage.at[slot], t, xs_hbm, dest_ref[0, 0, jx * tm + t], X_SUBL,
                       sems.at[slot]).start()
        return carry

    lax.fori_loop(0, tm, issue, 0, unroll=DMA_LOOP_UNROLL)

    @pl.when(i == 0)
    def _():
        zero[...] = jnp.zeros_like(zero)

        def fill(k, carry):
            _tile_copy(zero, 0, xs_hbm, pad_ref[k], X_SUBL, pad_sem).start()
            return carry

        lax.fori_loop(0, npad_ref[0], fill, 0)

        def fill_wait(k, carry):
            _tile_copy(zero, 0, xs_hbm, 0, X_SUBL, pad_sem).wait()
            return carry

        lax.fori_loop(0, npad_ref[0], fill_wait, 0)

    @pl.when(i == n_steps - 1)
    def _():
        drain(slot)
        if n_steps > 1:
            drain(1 - slot)


def _scatter_rows(h_all, dest, pad_dest, n_pad, n_rows, tm):
    n = h_all.shape[0]
    assert n % tm == 0
    n_steps = n // tm
    dest_blk = dest.reshape(n_steps, tm, TOP_K).transpose(0, 2, 1).reshape(n_steps, 1, TOP_K * tm)
    kern = functools.partial(_scatter_kernel, tm=tm, n_steps=n_steps)
    return pl.pallas_call(
        kern,
        grid_spec=pltpu.PrefetchScalarGridSpec(
            num_scalar_prefetch=1,
            grid=(n_steps,),
            in_specs=[
                pl.BlockSpec((1, 1, TOP_K * tm), lambda i, npad: (i, 0, 0), memory_space=pltpu.SMEM),
                pl.BlockSpec(memory_space=pltpu.SMEM),
                pl.BlockSpec((tm, D_MODEL), lambda i, npad: (i, 0)),
            ],
            out_specs=pl.BlockSpec(memory_space=pl.ANY),
            scratch_shapes=[
                pltpu.VMEM((2, tm * X_SUBL, LANES), jnp.uint32),
                pltpu.VMEM((X_SUBL, LANES), jnp.uint32),
                pltpu.SemaphoreType.DMA((2,)),
                pltpu.SemaphoreType.DMA(()),
            ],
        ),
        out_shape=jax.ShapeDtypeStruct((n_rows * X_SUBL, LANES), jnp.uint32),
        compiler_params=_cparams(("arbitrary",)),
    )(n_pad, dest_blk, pad_dest, h_all)


def _expert_kernel(se_ref, nsub_ref, nused_ref, x_ref, wg_ref, wl_ref, bg_ref, bl_ref, wd_ref,
                   bd_ref, o_ref, xb_scr, acc_scr, *, nf):
    s = pl.program_id(0)
    j = pl.program_id(1)
    n_sub = nsub_ref[s]

    def step(m):
        rows = slice(0, m)

        @pl.when(j == 0)
        def _():
            for l in range(X_SUBL):
                hi, lo = _unpack_bf16_pairs(x_ref[pl.ds(l, m, stride=X_SUBL), :])
                xb_scr[rows, LANES * l:LANES * (l + 1)] = hi.astype(bf16)
                xb_scr[rows, HALF_D + LANES * l:HALF_D + LANES * (l + 1)] = lo.astype(bf16)
            acc_scr[rows, :] = jnp.broadcast_to(bd_ref[...], (m, D_MODEL))

        xb = xb_scr[rows, :]
        hg = jnp.dot(xb, wg_ref[...].astype(bf16), preferred_element_type=f32) + bg_ref[...]
        hl = jnp.dot(xb, wl_ref[...].astype(bf16), preferred_element_type=f32) + bl_ref[...]
        glu = jnp.minimum(hg, SWIGLU_LIMIT)
        lin = jnp.clip(hl, -SWIGLU_LIMIT, SWIGLU_LIMIT)
        act = (glu * (1.0 / (1.0 + jnp.exp(-SWIGLU_ALPHA * glu))) * (lin + 1.0)).astype(bf16)
        wd = wd_ref[...].astype(bf16)
        for n0 in range(0, D_MODEL, MOE_DOWN_N):
            acc_scr[rows, n0:n0 + MOE_DOWN_N] += jnp.dot(
                act, wd[:, n0:n0 + MOE_DOWN_N], preferred_element_type=f32)

        @pl.when(j == nf - 1)
        def _():
            packed = _pack_bf16_pairs(acc_scr[rows, :])
            for l in range(X_SUBL):
                o_ref[pl.ds(l, m, stride=X_SUBL), :] = packed[:, LANES * l:LANES * (l + 1)]

    for k in range(1, MOE_NSUB + 1):
        pl.when(n_sub == k)(functools.partial(step, k * MOE_SUB))


def _experts(xs, super_e, n_sub, n_used, w_up, b_up, w_down, b_down, n_super):
    tf = MOE_FF_TILE
    nf = D_FF // tf
    last = lambda s, nu: jnp.minimum(s, nu[0] - 1)
    b_up3 = b_up.reshape(N_EXPERTS, 1, 2 * D_FF)
    b_down3 = b_down.reshape(N_EXPERTS, 1, D_MODEL)
    e_of = lambda s, se, nu: se[last(s, nu)]
    return pl.pallas_call(
        functools.partial(_expert_kernel, nf=nf),
        grid_spec=pltpu.PrefetchScalarGridSpec(
            num_scalar_prefetch=3,
            grid=(n_super, nf),
            in_specs=[
                pl.BlockSpec((MOE_SUPER * X_SUBL, LANES), lambda s, j, se, ns, nu: (last(s, nu), 0)),
                pl.BlockSpec((None, D_MODEL, tf), lambda s, j, se, ns, nu: (e_of(s, se, nu), 0, j)),
                pl.BlockSpec((None, D_MODEL, tf),
                             lambda s, j, se, ns, nu: (e_of(s, se, nu), 0, nf + j)),
                pl.BlockSpec((None, 1, tf), lambda s, j, se, ns, nu: (e_of(s, se, nu), 0, j)),
                pl.BlockSpec((None, 1, tf), lambda s, j, se, ns, nu: (e_of(s, se, nu), 0, nf + j)),
                pl.BlockSpec((None, tf, D_MODEL), lambda s, j, se, ns, nu: (e_of(s, se, nu), j, 0)),
                pl.BlockSpec((None, 1, D_MODEL), lambda s, j, se, ns, nu: (e_of(s, se, nu), 0, 0)),
            ],
            out_specs=pl.BlockSpec((MOE_SUPER * X_SUBL, LANES),
                                   lambda s, j, se, ns, nu: (last(s, nu), 0)),
            scratch_shapes=[pltpu.VMEM((MOE_SUPER, D_MODEL), bf16),
                            pltpu.VMEM((MOE_SUPER, D_MODEL), f32)],
        ),
        out_shape=jax.ShapeDtypeStruct((n_super * MOE_SUPER * X_SUBL, LANES), jnp.uint32),
        compiler_params=_cparams(("arbitrary", "arbitrary")),
    )(super_e, n_sub, n_used, xs, w_up, w_up, b_up3, b_up3, w_down, b_down3)


def _combine_kernel(dest_ref, nxt_ref, gate_ref, h_ref, rows_hbm, g_ref, b_ref, yp_ref, ys_ref,
                    buf, sems, *, tm, n_first, n_steps):
    i = pl.program_id(0)
    slot = i % 2

    def fetch(idx_ref, sl):
        def body(t, carry):
            for jx in range(TOP_K):
                _tile_copy(rows_hbm, idx_ref[0, 0, jx * tm + t], buf.at[sl, jx], t, X_SUBL,
                           sems.at[sl]).start()
            return carry
        lax.fori_loop(0, tm, body, 0, unroll=DMA_LOOP_UNROLL)

    @pl.when(i == 0)
    def _():
        fetch(dest_ref, slot)

    @pl.when(i + 1 < n_steps)
    def _():
        fetch(nxt_ref, 1 - slot)

    for jx in range(TOP_K):
        pltpu.make_async_copy(buf.at[slot, jx], buf.at[slot, jx], sems.at[slot]).wait()
    gate = gate_ref[...]
    cols_hi, cols_lo = [], []
    for l in range(X_SUBL):
        acc_hi = acc_lo = None
        for jx in range(TOP_K):
            hi, lo = _unpack_bf16_pairs(buf[slot, jx, pl.ds(l, tm, stride=X_SUBL), :])
            gj = gate[:, jx:jx + 1]
            acc_hi = gj * hi if acc_hi is None else acc_hi + gj * hi
            acc_lo = gj * lo if acc_lo is None else acc_lo + gj * lo
        cols_hi.append(acc_hi)
        cols_lo.append(acc_lo)
    y = jnp.concatenate(cols_hi + cols_lo, axis=-1)
    out = _layer_norm(DN_ALPHA * h_ref[...] + y, g_ref[...], b_ref[...])

    @pl.when(i < n_first)
    def _():
        yp_ref[...] = out

    @pl.when(i >= n_first)
    def _():
        ys_ref[...] = out


def _combine(rows_out, dest, gate, h_all, ln_g, ln_b, *, tm, n_first_rows):
    n = h_all.shape[0]
    assert n % tm == 0 and n_first_rows % tm == 0
    nblk = n // tm
    n_first = n_first_rows // tm
    dest_blk = dest.reshape(nblk, tm, TOP_K).transpose(0, 2, 1).reshape(nblk, 1, TOP_K * tm)
    kern = functools.partial(_combine_kernel, tm=tm, n_first=n_first, n_steps=nblk)
    const = lambda i: (0, 0)
    return pl.pallas_call(
        kern,
        grid=(nblk,),
        in_specs=[
            pl.BlockSpec((1, 1, TOP_K * tm), lambda i: (i, 0, 0), memory_space=pltpu.SMEM),
            pl.BlockSpec((1, 1, TOP_K * tm), lambda i: (jnp.minimum(i + 1, nblk - 1), 0, 0),
                         memory_space=pltpu.SMEM),
            pl.BlockSpec((tm, TOP_K), lambda i: (i, 0)),
            pl.BlockSpec((tm, D_MODEL), lambda i: (i, 0)),
            pl.BlockSpec(memory_space=pl.ANY),
            pl.BlockSpec((1, D_MODEL), const),
            pl.BlockSpec((1, D_MODEL), const),
        ],
        out_specs=[
            pl.BlockSpec((tm, D_MODEL), lambda i: (jnp.minimum(i, n_first - 1), 0)),
            pl.BlockSpec((tm, D_MODEL), lambda i: (jnp.maximum(i - n_first, 0), 0)),
        ],
        out_shape=[
            jax.ShapeDtypeStruct((n_first_rows, D_MODEL), f32),
            jax.ShapeDtypeStruct((n - n_first_rows, D_MODEL), f32),
        ],
        scratch_shapes=[pltpu.VMEM((2, TOP_K, tm * X_SUBL, LANES), jnp.uint32),
                        pltpu.SemaphoreType.DMA((2,))],
        compiler_params=_cparams(("arbitrary",)),
    )(dest_blk, dest_blk, gate, h_all, rows_out, ln_g, ln_b)


def _t5_bucket(rel):
    half = NUM_BUCKETS // 2
    exact = half // 2
    n = jnp.abs(rel)
    log_part = exact + (jnp.log(jnp.maximum(n, 1).astype(jnp.float32) / exact)
                        / math.log(MAX_DISTANCE / exact) * (half - exact)).astype(jnp.int32)
    log_part = jnp.minimum(log_part, half - 1)
    return jnp.where(rel > 0, half, 0) + jnp.where(n < exact, n, log_part)


def _band_bias(rel_bias):
    qi = jnp.arange(CHUNK)[:, None]
    km = jnp.arange(BAND)[None, :]
    bucket = _t5_bucket(km - WINDOW - qi)
    onehot = (bucket[..., None] == jnp.arange(NUM_BUCKETS)).astype(f32)
    return jnp.einsum("imb,bh->him", onehot, rel_bias.astype(f32),
                      precision=lax.Precision.HIGHEST)


def _pad_cols(a, width):
    return jnp.pad(a, ((0, 0), (0, width - a.shape[-1])))


def _pair_state(s):
    b = s.shape[0]
    return s.reshape(b, N_PAIRS, 2, RW_HEAD, RW_HEAD).transpose(0, 1, 3, 2, 4).reshape(
        b, N_PAIRS, RW_HEAD, LANES)


def _unpair_state(s):
    b = s.shape[0]
    return s.reshape(b, N_PAIRS, RW_HEAD, 2, RW_HEAD).transpose(0, 1, 3, 2, 4).reshape(
        b, RW_HEADS, RW_HEAD, RW_HEAD)


def _mix_group(x, k_hist, v_hist, hist_valid, wkv0, shift0, bias, sinks3, w_in_pad, rw_wts,
               *, in_tm, attn_nc):
    b, t, _ = x.shape
    q, kv, prw = _inproj(x.reshape(b * t, D_MODEL), w_in_pad, in_tm)
    tp = -(-t // (attn_nc * CHUNK)) * (attn_nc * CHUNK)
    q4 = q if b == 1 else q.reshape(ATT_HEADS, b, t, HEAD_DIM)
    kv3 = kv.reshape(b, t, 2 * KV_W)
    prw3 = prw.reshape(b, t, RW_PAD)
    hist = jnp.concatenate([k_hist.reshape(b, WINDOW, KV_W), v_hist.reshape(b, WINDOW, KV_W)], axis=-1)
    if t >= WINDOW:
        new_kv = kv3[:, t - WINDOW:]
    else:
        new_kv = jnp.concatenate([hist[:, t:], kv3], axis=1)
    if tp != t:
        q4 = jnp.pad(q4, ((0, 0), (0, 0), (0, tp - t), (0, 0)))
        prw3 = jnp.pad(prw3, ((0, 0), (0, tp - t), (0, 0)))
        kv3 = jnp.pad(kv3, ((0, 0), (0, tp - t), (0, 0)))
    att = _attention(q4, hist, kv3, bias, sinks3, nc=attn_nc, t_valid=t, hist_valid=hist_valid)
    rw, s_fin = _rwkv(prw3, _pad_cols(shift0.reshape(b, RW_PROJ), RW_PAD).reshape(b, 1, RW_PAD),
                      _pair_state(wkv0.astype(f32)), rw_wts, t_valid=t)
    new_k = new_kv[..., :KV_W].reshape(b, WINDOW, ATT_KV_HEADS, HEAD_DIM)
    new_v = new_kv[..., KV_W:].reshape(b, WINDOW, ATT_KV_HEADS, HEAD_DIM)
    shift = prw3[:, t - 1:t, :RW_PROJ]
    return (att[:, :t].reshape(b * t, ATT_W), rw[:, :t].reshape(b * t, RW_W),
            new_k, new_v, _unpair_state(s_fin), shift)


SCATTER_TM = 128
COMBINE_TM = 128


def kernel(x_prompt, x_sample, cache_k, cache_v, state_wkv, state_shift, rel_bias, w_in, attn_sinks, rw_mu, rw_w0, rw_decay_up, rw_a0, rw_iclr_up, rw_gate_up, rw_k_k, rw_k_a, rw_r_k, rw_lnx_g, rw_lnx_b, w_out, ln1_g, ln1_b, w_router, b_router, w_up, b_up, w_down, b_down, ln2_g, ln2_b):
    assert w_in.shape[0] == DEPTH == 1
    l = 0
    bp, tp_, _ = x_prompt.shape
    bs, ts, _ = x_sample.shape
    bias = _band_bias(rel_bias)
    sinks3 = attn_sinks[l].astype(f32).reshape(ATT_HEADS, 1, 1)

    w_in_pad = _pad_cols(w_in[l], IN_PAD).astype(bf16)

    def lora_rows(w, row0):
        return jnp.zeros((LORA_PAD, RW_W), f32).at[row0:row0 + w.shape[0]].set(w).astype(bf16)

    rw_wts = {
        "mu": _pad_cols(rw_mu[l].reshape(1, RW_PROJ), RW_PAD),
        "w0": rw_w0[l].reshape(1, RW_W),
        "wd": lora_rows(rw_decay_up[l], 0),
        "a0": rw_a0[l].reshape(1, RW_W),
        "wa": lora_rows(rw_iclr_up[l], DECAY_LORA),
        "wg": lora_rows(rw_gate_up[l], DECAY_LORA + ICLR_LORA),
        "k_k": rw_k_k[l].reshape(1, RW_W),
        "k_a": rw_k_a[l].reshape(1, RW_W),
        "r_k": rw_r_k[l].reshape(1, RW_W),
        "lnx_g": rw_lnx_g[l].reshape(1, RW_W),
        "lnx_b": rw_lnx_b[l].reshape(1, RW_W),
    }
    wr = _pad_cols(w_router[l], LANES)
    wr_hi = wr.astype(bf16)
    op_wts = {
        "w_out": w_out[l].astype(bf16),
        "ln1_g": ln1_g[l].reshape(1, D_MODEL),
        "ln1_b": ln1_b[l].reshape(1, D_MODEL),
        "wr_hi": wr_hi,
        "wr_lo": (wr - wr_hi.astype(f32)).astype(bf16),
        "b_router": jnp.concatenate([b_router[l].astype(f32),
                                     jnp.full((LANES - N_EXPERTS,), NEG_BIG, f32)]).reshape(1, LANES),
    }

    zero_kv = jnp.zeros((bp, WINDOW, ATT_KV_HEADS, HEAD_DIM), f32)
    att_p, rwo_p, k1, v1, w1, s1 = _mix_group(
        x_prompt, zero_kv, zero_kv, False, jnp.zeros((bp, RW_HEADS, RW_HEAD, RW_HEAD), f32),
        jnp.zeros((bp, 1, RW_PROJ), f32), bias, sinks3, w_in_pad, rw_wts,
        in_tm=min(256, bp * tp_), attn_nc=min(8, -(-tp_ // CHUNK)))
    att_s, rwo_s, k2, v2, w2, s2 = _mix_group(
        x_sample, cache_k[l], cache_v[l], True, state_wkv[l], state_shift[l], bias, sinks3,
        w_in_pad, rw_wts, in_tm=min(256, bs * ts), attn_nc=1)

    n_p, n_s = bp * tp_, bs * ts
    n_all = n_p + n_s
    tm_p, tm_s = min(512, n_p), min(128, n_s)
    *outs, counts_p = _outproj(att_p, rwo_p, x_prompt.reshape(n_p, D_MODEL), op_wts,
                               tm=tm_p, row0=0, total_rows=n_all)
    h_all, top_idx, gate, rank, counts = _outproj(
        att_s, rwo_s, x_sample.reshape(n_s, D_MODEL), op_wts,
        tm=tm_s, row0=n_p, total_rows=n_all, prev=outs, counts0=counts_p)

    dest, super_e, n_sub, n_used, pad_dest, n_pad, n_super = _routing(top_idx, rank, counts)
    xs = _scatter_rows(h_all, dest, pad_dest, n_pad, n_super * MOE_SUPER, min(SCATTER_TM, n_s))
    rows_out = _experts(xs, super_e, n_sub, n_used, w_up[l], b_up[l], w_down[l], b_down[l],
                        n_super)
    y_p, y_s = _combine(rows_out, dest, gate, h_all, ln2_g[l].reshape(1, D_MODEL),
                        ln2_b[l].reshape(1, D_MODEL), tm=min(COMBINE_TM, n_s), n_first_rows=n_p)

    return (y_p.reshape(bp, tp_, D_MODEL), y_s.reshape(bs, ts, D_MODEL),
            k1[None], v1[None], w1[None], s1[None], k2[None], v2[None], w2[None], s2[None])
```

```python
import functools
import math

import jax
import jax.numpy as jnp
from jax import lax
from jax.experimental import pallas as pl
from jax.experimental.pallas import tpu as pltpu

f32 = jnp.float32
bf16 = jnp.bfloat16

D_MODEL = 2048
CHUNK = 64
ATT_HEADS = 16
ATT_KV_HEADS = 2
HEAD_DIM = 64
ATT_GROUP = ATT_HEADS // ATT_KV_HEADS
ATT_W = ATT_HEADS * HEAD_DIM
KV_W = ATT_KV_HEADS * HEAD_DIM
ATT_PROJ = ATT_W + 2 * KV_W
WINDOW = 128
BAND = WINDOW + CHUNK
NUM_BUCKETS = 32
MAX_DISTANCE = 128
RW_HEAD = 64
RW_W = 1024
RW_HEADS = RW_W // RW_HEAD
DECAY_LORA = 96
ICLR_LORA = 96
GATE_LORA = 128
RW_PROJ = 3 * RW_W + DECAY_LORA + ICLR_LORA + GATE_LORA
GN_EPS = 64e-5
LN_EPS = 1e-5
N_EXPERTS = 32
TOP_K = 4
D_FF = D_MODEL
SWIGLU_LIMIT = 7.0
SWIGLU_ALPHA = 1.702
DEPTH = 1
DN_ALPHA = (2 * DEPTH) ** 0.25

LANES = 128
VMEM_LIMIT = 56 * 1024 * 1024

LORA_W = DECAY_LORA + ICLR_LORA + GATE_LORA
LORA_PAD = -(-LORA_W // LANES) * LANES
RW_PAD = 3 * RW_W + LORA_PAD
IN_PAD = ATT_PROJ + RW_PAD
N_PAIRS = RW_HEADS // 2
NEG_BIG = -1e30


def _cparams(sem):
    return pltpu.CompilerParams(dimension_semantics=sem, vmem_limit_bytes=VMEM_LIMIT)


def _inproj_kernel(x_ref, w_ref, q_ref, kv_ref, rw_ref):
    acc = jnp.dot(x_ref[...].astype(bf16), w_ref[...], preferred_element_type=f32)
    scale = HEAD_DIM ** -0.5
    for h in range(ATT_HEADS):
        q_ref[h] = (acc[:, h * HEAD_DIM:(h + 1) * HEAD_DIM] * scale).astype(bf16)
    kv_ref[...] = acc[:, ATT_W:ATT_PROJ]
    rw_ref[...] = acc[:, ATT_PROJ:IN_PAD]


def _inproj(x2, w_pad, tm):
    rows = x2.shape[0]
    assert rows % tm == 0
    return pl.pallas_call(
        _inproj_kernel,
        grid=(rows // tm,),
        in_specs=[
            pl.BlockSpec((tm, D_MODEL), lambda i: (i, 0)),
            pl.BlockSpec((D_MODEL, IN_PAD), lambda i: (0, 0), pipeline_mode=pl.Buffered(1)),
        ],
        out_specs=[
            pl.BlockSpec((ATT_HEADS, None, tm, HEAD_DIM), lambda i: (0, 0, i, 0)),
            pl.BlockSpec((tm, 2 * KV_W), lambda i: (i, 0)),
            pl.BlockSpec((tm, RW_PAD), lambda i: (i, 0)),
        ],
        out_shape=[
            jax.ShapeDtypeStruct((ATT_HEADS, 1, rows, HEAD_DIM), bf16),
            jax.ShapeDtypeStruct((rows, 2 * KV_W), f32),
            jax.ShapeDtypeStruct((rows, RW_PAD), f32),
        ],
        compiler_params=_cparams(("parallel",)),
    )(x2, w_pad)


ATT_UNROLL = 4


def _attn_kernel(q_ref, hist_ref, kva_ref, kvb_ref, kvm_ref, bias_ref, sink_ref, o_ref, kvbuf,
                 *, nc, t_valid, hist_valid):
    j = pl.program_id(1)
    first = j == 0
    kvbuf[0:CHUNK] = jnp.where(first, hist_ref[0:CHUNK], kva_ref[...]).astype(bf16)
    kvbuf[CHUNK:WINDOW] = jnp.where(first, hist_ref[CHUNK:WINDOW], kvb_ref[...]).astype(bf16)
    kvbuf[WINDOW:WINDOW + nc * CHUNK] = kvm_ref[...].astype(bf16)
    m_idx = lax.broadcasted_iota(jnp.int32, (1, 1, BAND), 2)
    for c0 in range(0, nc, ATT_UNROLL):
        items = [(c, g) for c in range(c0, min(c0 + ATT_UNROLL, nc)) for g in range(ATT_KV_HEADS)]
        bands = {c: kvbuf[c * CHUNK:c * CHUNK + BAND, :] for c, _ in items}
        valid = {}
        for c in bands:
            idx = (j * nc + c) * CHUNK + m_idx
            v = idx - WINDOW < t_valid
            valid[c] = v if hist_valid else jnp.logical_and(v, idx >= WINDOW)
        s = [lax.dot_general(
                q_ref[g * ATT_GROUP:(g + 1) * ATT_GROUP, c * CHUNK:(c + 1) * CHUNK, :].reshape(
                    ATT_GROUP * CHUNK, HEAD_DIM),
                bands[c][:, g * HEAD_DIM:(g + 1) * HEAD_DIM],
                (((1,), (1,)), ((), ())), preferred_element_type=f32) for c, g in items]
        s = [jnp.where(valid[c], s[i].reshape(ATT_GROUP, CHUNK, BAND)
                       + bias_ref[g * ATT_GROUP:(g + 1) * ATT_GROUP], NEG_BIG)
             for i, (c, g) in enumerate(items)]
        sk = [sink_ref[g * ATT_GROUP:(g + 1) * ATT_GROUP] for _, g in items]
        m = [jnp.maximum(jnp.max(s[i], axis=-1, keepdims=True), sk[i]) for i in range(len(items))]
        p = [jnp.exp(s[i] - m[i]) for i in range(len(items))]
        den = [jnp.sum(p[i], axis=-1, keepdims=True) + jnp.exp(sk[i] - m[i])
               for i in range(len(items))]
        o = [jnp.dot(p[i].reshape(ATT_GROUP * CHUNK, BAND).astype(bf16),
                     bands[c][:, KV_W + g * HEAD_DIM:KV_W + (g + 1) * HEAD_DIM],
                     preferred_element_type=f32).reshape(ATT_GROUP, CHUNK, HEAD_DIM) / den[i]
             for i, (c, g) in enumerate(items)]
        for c in bands:
            heads = [o[i][h] for i, (ci, _) in enumerate(items) if ci == c for h in range(ATT_GROUP)]
            o_ref[c * CHUNK:(c + 1) * CHUNK, :] = jnp.concatenate(heads, axis=-1).astype(bf16)


def _attention(q4, hist, kv, bias, sinks3, *, nc, t_valid, hist_valid):
    _, b, tp, _ = q4.shape
    assert tp % (nc * CHUNK) == 0 and kv.shape[1] == tp
    nblk = tp // (nc * CHUNK)
    kern = functools.partial(_attn_kernel, nc=nc, t_valid=t_valid, hist_valid=hist_valid)
    return pl.pallas_call(
        kern,
        grid=(b, nblk),
        in_specs=[
            pl.BlockSpec((ATT_HEADS, None, nc * CHUNK, HEAD_DIM), lambda bi, j: (0, bi, j, 0)),
            pl.BlockSpec((None, WINDOW, 2 * KV_W), lambda bi, j: (bi, 0, 0)),
            pl.BlockSpec((None, CHUNK, 2 * KV_W), lambda bi, j: (bi, jnp.maximum(j * nc - 2, 0), 0)),
            pl.BlockSpec((None, CHUNK, 2 * KV_W), lambda bi, j: (bi, jnp.maximum(j * nc - 1, 0), 0)),
            pl.BlockSpec((None, nc * CHUNK, 2 * KV_W), lambda bi, j: (bi, j, 0)),
            pl.BlockSpec((ATT_HEADS, CHUNK, BAND), lambda bi, j: (0, 0, 0)),
            pl.BlockSpec((ATT_HEADS, 1, 1), lambda bi, j: (0, 0, 0)),
        ],
        out_specs=pl.BlockSpec((None, nc * CHUNK, ATT_W), lambda bi, j: (bi, j, 0)),
        out_shape=jax.ShapeDtypeStruct((b, tp, ATT_W), bf16),
        scratch_shapes=[pltpu.VMEM(((nc + 2) * CHUNK, 2 * KV_W), bf16)],
        compiler_params=_cparams(("parallel", "parallel")),
    )(q4, hist, kv, kv, kv, bias, sinks3)


N_OPS_BF16 = 7
N_OPS_F32 = 4


def _rwkv_step_kernel(p_ref, shift0_ref, s0_ref, mu_ref, w0_ref, wd_ref, a0_ref, wa_ref, wg_ref,
                      kk_ref, ka_ref, rk_ref, lng_ref, lnb_ref, o_ref, sfin_ref,
                      s_scr, last_scr, opb_scr, opf_scr, cl_scr, *, t_valid, n_steps, nch):
    c = pl.program_id(1)
    L = CHUNK
    R = nch * L

    @pl.when(c == 0)
    def _():
        s_scr[...] = s0_ref[...]
        last_scr[...] = shift0_ref[...]
        opb_scr[...] = jnp.zeros_like(opb_scr)
        opf_scr[...] = jnp.zeros_like(opf_scr)
        cl_scr[...] = jnp.zeros_like(cl_scr)

    kq, rq, bt, kt, bh, kh, vb = range(N_OPS_BF16)
    g_l = [jnp.exp(cl_scr[ch]) for ch in range(nch)]

    lane = lax.broadcasted_iota(jnp.int32, (1, LANES), 1)
    lo_half = lane < RW_HEAD
    rr = lax.broadcasted_iota(jnp.int32, (LANES, LANES), 0)
    cc = lax.broadcasted_iota(jnp.int32, (LANES, LANES), 1)
    same_head = (rr // RW_HEAD) == (cc // RW_HEAD)
    r4 = lax.broadcasted_iota(jnp.int32, (2 * LANES, 2 * LANES), 0)
    c4 = lax.broadcasted_iota(jnp.int32, (2 * LANES, 2 * LANES), 1)
    ones_bd = jnp.where((r4 // RW_HEAD) == (c4 // RW_HEAD), 1.0, 0.0).astype(bf16)
    rs = [slice(L * ch, L * (ch + 1)) for ch in range(nch)]
    sl = [slice(LANES * t, LANES * (t + 1)) for t in range(N_PAIRS)]
    n_quad = RW_W // (2 * LANES)

    def seg_sums(xs):
        n = len(xs)
        x = jnp.concatenate(xs, axis=0) if n > 1 else xs[0]
        hi = x.astype(bf16)
        lo = (x - hi.astype(f32)).astype(bf16)
        both = jnp.concatenate([hi, lo], axis=0)
        m = 2 * n * R
        tiles = jnp.concatenate([both[:, 2 * LANES * t:2 * LANES * (t + 1)] for t in range(n_quad)],
                                axis=0)
        res = jnp.dot(tiles, ones_bd, preferred_element_type=f32)
        y = jnp.concatenate([res[m * t:m * (t + 1)] for t in range(n_quad)], axis=1)
        y = y[:n * R] + y[n * R:]
        return [y[i * R:(i + 1) * R] for i in range(n)]

    def sigmoid(z):
        return 1.0 / (1.0 + jnp.exp(-z))

    p = p_ref[...]
    row = lax.broadcasted_iota(jnp.int32, (R, 1), 0)
    shifted = jnp.where(row == 0, last_scr[...], pltpu.roll(p, 1, axis=0))
    last_scr[...] = p[R - 1:R, :]
    xm = p + (shifted - p) * mu_ref[...]
    r = xm[:, 0:RW_W]
    k = xm[:, RW_W:2 * RW_W]
    v = xm[:, 2 * RW_W:3 * RW_W]
    tail = xm[:, 3 * RW_W:RW_PAD]
    w_log = w0_ref[...] + jnp.dot(jnp.tanh(tail).astype(bf16), wd_ref[...],
                                  preferred_element_type=f32)
    a = sigmoid(a0_ref[...] + jnp.dot(tail.astype(bf16), wa_ref[...], preferred_element_type=f32))
    g = jnp.dot(sigmoid(tail).astype(bf16), wg_ref[...], preferred_element_type=f32)

    def prep_mid():
        z = -w_log
        softplus = jnp.maximum(z, 0.0) + jnp.log(1.0 + jnp.exp(-jnp.abs(z)))
        ld = -jnp.exp(-softplus - 0.5)
        kk = k * kk_ref[...]
        k_mod = k * (1.0 + (a - 1.0) * ka_ref[...])
        nrm2, bonus_s = seg_sums([kk * kk, r * k_mod * rk_ref[...]])
        kk = kk / jnp.maximum(jnp.sqrt(nrm2), 1e-12)
        b = kk * a
        if t_valid % R != 0:
            live = (jnp.minimum(c, n_steps - 1) * R + row) < t_valid
            ld = jnp.where(live, ld, 0.0)
            b = jnp.where(live, b, 0.0)
            k_mod = jnp.where(live, k_mod, 0.0)
        h1 = ld.astype(bf16)
        r1 = ld - h1.astype(f32)
        h2 = r1.astype(bf16)
        h3 = (r1 - h2.astype(f32)).astype(bf16)
        ti = lax.broadcasted_iota(jnp.int32, (L, 3 * L), 0)
        si = lax.broadcasted_iota(jnp.int32, (L, 3 * L), 1) % L
        tri3 = jnp.where(si <= ti, 1.0, 0.0).astype(bf16)
        cums = [jnp.dot(tri3, jnp.concatenate([h1[rs[ch]], h2[rs[ch]], h3[rs[ch]]], axis=0),
                        preferred_element_type=f32) for ch in range(nch)]
        return kk, k_mod, b, ld, cums, bonus_s

    def prep_tail(kk, k_mod, b, ld, cums, bonus_s):
        cum = jnp.concatenate(cums, axis=0) if nch > 1 else cums[0]
        cum_ls = [cums[ch][L - 1:L, :] for ch in range(nch)]
        cum_l = (jnp.concatenate([jnp.broadcast_to(x, (L, RW_W)) for x in cum_ls], axis=0)
                 if nch > 1 else cum_ls[0])
        g_inv = jnp.exp(-cum)
        g_rest = jnp.exp(cum_l - cum)
        n_rq = r * jnp.exp(cum)
        ops_b = [(kk * jnp.exp(cum - ld)).astype(bf16), n_rq.astype(bf16),
                 (b * g_inv).astype(bf16), (k_mod * g_inv).astype(bf16),
                 (b * g_rest).astype(bf16), (k_mod * g_rest).astype(bf16), v.astype(bf16)]
        return ops_b, [n_rq, v, g, bonus_s], cum_ls

    def bd(x):
        zero = jnp.zeros_like(x)
        return jnp.concatenate([jnp.where(lo_half, x, zero), jnp.where(lo_half, zero, x)], axis=0)

    def mm(x, y):
        return jnp.dot(x, y, preferred_element_type=f32)

    def mm_nt(x, y):
        return lax.dot_general(x, y, (((1,), (1,)), ((), ())), preferred_element_type=f32)

    def mm_tn(x, y):
        return lax.dot_general(x, y, (((0,), (0,)), ((), ())), preferred_element_type=f32)

    tt = lax.broadcasted_iota(jnp.int32, (L, LANES), 0)
    ss = lax.broadcasted_iota(jnp.int32, (L, LANES), 1) % RW_HEAD
    strict = ss < tt
    incl = ss <= tt
    eye_pair = jnp.where(ss == tt, 1.0, 0.0).astype(f32)

    items = [(ch, t) for ch in range(nch) for t in range(N_PAIRS)]
    I = range(len(items))

    def tile(slot, i):
        ch, t = items[i]
        return opb_scr[slot, rs[ch], sl[t]]

    a_all = [mm_nt(jnp.concatenate([tile(kq, i), tile(rq, i)], axis=0),
                   jnp.concatenate([bd(tile(bt, i)), bd(tile(kt, i))], axis=0)) for i in I]
    a_bk = [jnp.where(strict, a_all[i][:L, :LANES], 0.0) for i in I]
    a_kk = [jnp.where(strict, a_all[i][:L, LANES:], 0.0).astype(bf16) for i in I]
    a_rb = [jnp.where(incl, a_all[i][L:, :LANES], 0.0).astype(bf16) for i in I]
    a_rk = [jnp.where(incl, a_all[i][L:, LANES:], 0.0).astype(bf16) for i in I]
    bd_v = [bd(tile(vb, i)) for i in I]
    akv = [mm(a_kk[i], bd_v[i]).astype(bf16) for i in I]
    w_inv = [eye_pair - a_bk[i] for i in I]
    pw = [a_bk[i].astype(bf16) for i in I]
    pw_bd = [bd(pw[i]) for i in I]
    for it in range(5):
        pw = [mm(pw[i], pw_bd[i]).astype(bf16) for i in I]
        pw_bd = [bd(pw[i]) for i in I]
        w_inv = [w_inv[i] + mm(w_inv[i].astype(bf16), pw_bd[i]) for i in I]
        if it == 2:
            mid = prep_mid()
    qu = [mm(w_inv[i].astype(bf16), jnp.concatenate([bd(tile(kq, i)), bd(akv[i])], axis=1))
          for i in I]
    q_m = [qu[i][:, :LANES].astype(bf16) for i in I]
    u_m = [qu[i][:, LANES:].astype(bf16) for i in I]
    m_full = [mm_tn(q_m[i], tile(bh, i)) for i in I]
    neg_m = [jnp.where(same_head, -m_full[i], 0.0).astype(bf16) for i in I]
    c_full = [mm_tn(jnp.concatenate([tile(vb, i), u_m[i]], axis=0),
                    jnp.concatenate([tile(kh, i), -tile(bh, i)], axis=0)) for i in I]
    go = [mm(a_rb[i], jnp.concatenate([bd(q_m[i]), bd(u_m[i])], axis=1)) for i in I]
    o_rk = [mm(a_rk[i], bd_v[i]) for i in I]
    new_b, new_f, new_cl = prep_tail(*mid)
    g_m = [(opf_scr[0, rs[items[i][0]], sl[items[i][1]]] - go[i][:, :LANES]).astype(bf16)
           for i in I]
    o_in = [o_rk[i] - go[i][:, LANES:] for i in I]
    c_pair = [jnp.where(lo_half, c_full[i][:RW_HEAD], c_full[i][RW_HEAD:]) for i in I]
    s_cur = [s_scr[t] for t in range(N_PAIRS)]
    o_rows = []
    for ch in range(nch):
        ii = [ch * N_PAIRS + t for t in range(N_PAIRS)]
        s_b = [s.astype(bf16) for s in s_cur]
        o_rows.append(jnp.concatenate(
            [mm_nt(g_m[i], bd(s_b[t])) + o_in[i] for t, i in enumerate(ii)], axis=1))
        s_upd = [mm(s_b[t], neg_m[i]) for t, i in enumerate(ii)]
        s_cur = [s_cur[t] * g_l[ch][:, sl[t]] + s_upd[t] + c_pair[i] for t, i in enumerate(ii)]
    for t in range(N_PAIRS):
        s_scr[t] = s_cur[t]

    o = jnp.concatenate(o_rows, axis=0) if nch > 1 else o_rows[0]
    (o_sum,) = seg_sums([o])
    d = o - o_sum * (1.0 / RW_HEAD)
    (d2,) = seg_sums([d * d])
    on = d * lax.rsqrt(d2 * (1.0 / RW_HEAD) + GN_EPS) * lng_ref[...] + lnb_ref[...]
    o_ref[...] = ((on + opf_scr[3] * opf_scr[1]) * opf_scr[2]).astype(bf16)

    for i in range(N_OPS_BF16):
        opb_scr[i] = new_b[i]
    for i in range(N_OPS_F32):
        opf_scr[i] = new_f[i]
    for ch in range(nch):
        cl_scr[ch] = new_cl[ch]

    @pl.when(c == n_steps)
    def _():
        sfin_ref[...] = s_scr[...]


RWKV_CHUNKS_PER_STEP = 2


def _rwkv(prw, shift0, s0_pair, wts, *, t_valid):
    b, tp, _ = prw.shape
    nch = RWKV_CHUNKS_PER_STEP if tp % (RWKV_CHUNKS_PER_STEP * CHUNK) == 0 else 1
    rows = nch * CHUNK
    n_steps = tp // rows
    kern = functools.partial(_rwkv_step_kernel, t_valid=t_valid, n_steps=n_steps, nch=nch)
    const2 = lambda bi, c: (0, 0)
    row_spec = pl.BlockSpec((1, RW_W), const2)
    return pl.pallas_call(
        kern,
        grid=(b, n_steps + 1),
        in_specs=[
            pl.BlockSpec((None, rows, RW_PAD), lambda bi, c: (bi, jnp.minimum(c, n_steps - 1), 0)),
            pl.BlockSpec((None, 1, RW_PAD), lambda bi, c: (bi, 0, 0)),
            pl.BlockSpec((None, N_PAIRS, RW_HEAD, LANES), lambda bi, c: (bi, 0, 0, 0)),
            pl.BlockSpec((1, RW_PAD), const2),
            row_spec,
            pl.BlockSpec((LORA_PAD, RW_W), const2),
            row_spec,
            pl.BlockSpec((LORA_PAD, RW_W), const2),
            pl.BlockSpec((LORA_PAD, RW_W), const2),
            row_spec, row_spec, row_spec, row_spec, row_spec,
        ],
        out_specs=[
            pl.BlockSpec((None, rows, RW_W), lambda bi, c: (bi, jnp.maximum(c - 1, 0), 0)),
            pl.BlockSpec((None, N_PAIRS, RW_HEAD, LANES), lambda bi, c: (bi, 0, 0, 0)),
        ],
        out_shape=[
            jax.ShapeDtypeStruct((b, tp, RW_W), bf16),
            jax.ShapeDtypeStruct((b, N_PAIRS, RW_HEAD, LANES), f32),
        ],
        scratch_shapes=[pltpu.VMEM((N_PAIRS, RW_HEAD, LANES), f32), pltpu.VMEM((1, RW_PAD), f32),
                        pltpu.VMEM((N_OPS_BF16, rows, RW_W), bf16),
                        pltpu.VMEM((N_OPS_F32, rows, RW_W), f32),
                        pltpu.VMEM((nch, 1, RW_W), f32)],
        compiler_params=_cparams(("parallel", "arbitrary")),
    )(prw, shift0, s0_pair, wts["mu"], wts["w0"], wts["wd"], wts["a0"], wts["wa"], wts["wg"],
      wts["k_k"], wts["k_a"], wts["r_k"], wts["lnx_g"], wts["lnx_b"])


def _layer_norm(z, g, b):
    mu = jnp.mean(z, axis=-1, keepdims=True)
    d = z - mu
    var = jnp.mean(d * d, axis=-1, keepdims=True)
    return d * lax.rsqrt(var + LN_EPS) * g + b


N_SHARED_OUT = 4
OUTPROJ_PARTS = 2


def _outproj_kernel(*refs, aliased, n_steps):
    att_ref, rw_ref, x_ref, wo_ref, g_ref, b_ref, wrh_ref, wrl_ref, br_ref, cnt0_ref = refs[:10]
    h_ref, idx_ref, gate_ref, rank_ref, cnt_ref, carry = refs[10 + N_SHARED_OUT * aliased:]
    step = pl.program_id(0)

    @pl.when(step == 0)
    def _():
        carry[...] = cnt0_ref[...]

    tm = x_ref.shape[0]
    n_part = OUTPROJ_PARTS if tm % (OUTPROJ_PARTS * 8) == 0 else 1
    pm = tm // n_part
    parts = [slice(q * pm, (q + 1) * pm) for q in range(n_part)]
    mix = [jnp.dot(att_ref[r, :], wo_ref[0:ATT_W], preferred_element_type=f32)
           + jnp.dot(rw_ref[r, :], wo_ref[ATT_W:ATT_W + RW_W], preferred_element_type=f32)
           for r in parts]
    h = [_layer_norm(DN_ALPHA * x_ref[r, :] + mix[q], g_ref[...], b_ref[...])
         for q, r in enumerate(parts)]
    for q, r in enumerate(parts):
        h_ref[r, :] = h[q]
    hh = [x.astype(bf16) for x in h]
    hl = [(h[q] - hh[q].astype(f32)).astype(bf16) for q in range(n_part)]
    logits = [(jnp.dot(hh[q], wrh_ref[...], preferred_element_type=f32)
               + jnp.dot(hl[q], wrh_ref[...], preferred_element_type=f32)
               + jnp.dot(hh[q], wrl_ref[...], preferred_element_type=f32)) + br_ref[...]
              for q in range(n_part)]
    lane = lax.broadcasted_iota(jnp.int32, (pm, LANES), 1).astype(f32)
    ti = lax.broadcasted_iota(jnp.int32, (pm, pm), 0)
    si = lax.broadcasted_iota(jnp.int32, (pm, pm), 1)
    before = jnp.where(si < ti, 1.0, 0.0).astype(bf16)
    counts = carry[...]
    for q, r in enumerate(parts):
        vals, idxs = [], []
        cur = logits[q]
        for _ in range(TOP_K):
            m = jnp.max(cur, axis=-1, keepdims=True)
            i = jnp.min(jnp.where(cur == m, lane, float(LANES)), axis=-1, keepdims=True)
            vals.append(m)
            idxs.append(i)
            cur = jnp.where(lane == i, -jnp.inf, cur)
        es = [jnp.exp(vv - vals[0]) for vv in vals]
        tot = es[0] + es[1] + es[2] + es[3]
        idx_ref[r, :] = jnp.concatenate(idxs, axis=-1).astype(jnp.int32)
        gate_ref[r, :] = jnp.concatenate([e / tot for e in es], axis=-1)
        hits = [jnp.where(lane == i, 1.0, 0.0) for i in idxs]
        multi = hits[0] + hits[1] + hits[2] + hits[3]
        base = counts + jnp.dot(before, multi.astype(bf16), preferred_element_type=f32)
        ranks = [jnp.sum(hh_ * base, axis=-1, keepdims=True) for hh_ in hits]
        rank_ref[r, :] = jnp.concatenate(ranks, axis=-1).astype(jnp.int32)
        counts = counts + jnp.sum(multi, axis=0, keepdims=True)
    carry[...] = counts

    @pl.when(step == n_steps - 1)
    def _():
        cnt_ref[...] = carry[...]


def _outproj(att2, rw2, x2, wts, *, tm, row0, total_rows, prev=None, counts0=None):
    rows = x2.shape[0]
    assert rows % tm == 0 and row0 % tm == 0
    blk0 = row0 // tm
    aliased = prev is not None
    if counts0 is None:
        counts0 = jnp.zeros((1, LANES), f32)
    const = lambda i: (0, 0)
    in_specs = [
        pl.BlockSpec((tm, ATT_W), lambda i: (i, 0)),
        pl.BlockSpec((tm, RW_W), lambda i: (i, 0)),
        pl.BlockSpec((tm, D_MODEL), lambda i: (i, 0)),
        pl.BlockSpec((D_MODEL, D_MODEL), const, pipeline_mode=pl.Buffered(1)),
        pl.BlockSpec((1, D_MODEL), const),
        pl.BlockSpec((1, D_MODEL), const),
        pl.BlockSpec((D_MODEL, LANES), const),
        pl.BlockSpec((D_MODEL, LANES), const),
        pl.BlockSpec((1, LANES), const),
        pl.BlockSpec((1, LANES), const),
    ]
    args = [att2, rw2, x2, wts["w_out"], wts["ln1_g"], wts["ln1_b"], wts["wr_hi"], wts["wr_lo"],
            wts["b_router"], counts0]
    aliases = {}
    if aliased:
        in_specs += [pl.BlockSpec(memory_space=pl.ANY)] * N_SHARED_OUT
        args += list(prev)
        aliases = {len(args) - N_SHARED_OUT + k: k for k in range(N_SHARED_OUT)}
    n_steps = rows // tm
    return pl.pallas_call(
        functools.partial(_outproj_kernel, aliased=int(aliased), n_steps=n_steps),
        grid=(n_steps,),
        in_specs=in_specs,
        out_specs=[
            pl.BlockSpec((tm, D_MODEL), lambda i: (blk0 + i, 0)),
            pl.BlockSpec((tm, TOP_K), lambda i: (blk0 + i, 0)),
            pl.BlockSpec((tm, TOP_K), lambda i: (blk0 + i, 0)),
            pl.BlockSpec((tm, TOP_K), lambda i: (blk0 + i, 0)),
            pl.BlockSpec((1, LANES), const),
        ],
        out_shape=[
            jax.ShapeDtypeStruct((total_rows, D_MODEL), f32),
            jax.ShapeDtypeStruct((total_rows, TOP_K), jnp.int32),
            jax.ShapeDtypeStruct((total_rows, TOP_K), f32),
            jax.ShapeDtypeStruct((total_rows, TOP_K), jnp.int32),
            jax.ShapeDtypeStruct((1, LANES), f32),
        ],
        scratch_shapes=[pltpu.VMEM((1, LANES), f32)],
        input_output_aliases=aliases,
        compiler_params=_cparams(("arbitrary",)),
    )(*args)


MOE_SUB = 256
MOE_NSUB = 5
MOE_SUPER = MOE_SUB * MOE_NSUB
MOE_FF_TILE = 256
MOE_DOWN_N = 512
MOE_MAX_PAD = N_EXPERTS * (MOE_SUB - 1)
X_SUBL = D_MODEL // (2 * LANES)
HALF_D = D_MODEL // 2
DMA_LOOP_UNROLL = 4


def _pack_bf16_pairs(x):
    hi = lax.bitcast_convert_type(x[:, :HALF_D].astype(bf16).astype(f32), jnp.uint32)
    lo = lax.bitcast_convert_type(x[:, HALF_D:].astype(bf16).astype(f32), jnp.uint32)
    return hi | (lo >> 16)


def _unpack_bf16_pairs(u):
    hi = lax.bitcast_convert_type(u & jnp.uint32(0xFFFF0000), f32)
    lo = lax.bitcast_convert_type(u << 16, f32)
    return hi, lo


def _routing(top_idx, rank, counts_f):
    n = top_idx.shape[0]
    n_assign = n * TOP_K
    counts = counts_f[0, :N_EXPERTS].astype(jnp.int32)
    padded = (counts + MOE_SUPER - 1) // MOE_SUPER * MOE_SUPER
    pad_end = jnp.cumsum(padded)
    start = pad_end - padded
    experts = jnp.arange(N_EXPERTS, dtype=jnp.int32)

    def lookup(table, idx):
        return jnp.sum(jnp.where(idx[..., None] == experts, table, 0), axis=-1)

    def bucket(edges, x):
        return jnp.minimum(jnp.sum((edges <= x[..., None]).astype(jnp.int32), axis=-1),
                           N_EXPERTS - 1)

    dest = (lookup(start, top_idx) + rank).astype(jnp.int32)
    n_super = (n_assign + N_EXPERTS * (MOE_SUPER - 1) + MOE_SUPER - 1) // MOE_SUPER
    s_row0 = jnp.arange(n_super, dtype=jnp.int32) * MOE_SUPER
    super_e = bucket(pad_end, s_row0)
    rows_here = jnp.clip(lookup(counts, super_e) - (s_row0 - lookup(start, super_e)), 0, MOE_SUPER)
    rows_here = jnp.where(s_row0 < pad_end[-1], rows_here, 0)
    n_sub = ((rows_here + MOE_SUB - 1) // MOE_SUB).astype(jnp.int32)
    n_used = (pad_end[-1] // MOE_SUPER).astype(jnp.int32).reshape(1)
    n_pad_e = (counts + MOE_SUB - 1) // MOE_SUB * MOE_SUB - counts
    pad_cum = jnp.cumsum(n_pad_e)
    kk = jnp.arange(MOE_MAX_PAD, dtype=jnp.int32)
    pe = bucket(pad_cum, kk)
    pad_dest = (lookup(start + counts - (pad_cum - n_pad_e), pe) + kk).astype(jnp.int32)
    pad_dest = jnp.where(kk < pad_cum[-1], pad_dest, 0)
    n_pad = pad_cum[-1].astype(jnp.int32).reshape(1)
    return dest, super_e, n_sub, n_used, pad_dest, n_pad, n_super


def _tile_copy(src_ref, src_tok, dst_ref, dst_tok, subl, sem):
    s0 = pl.multiple_of(src_tok * subl, subl)
    d0 = pl.multiple_of(dst_tok * subl, subl)
    return pltpu.make_async_copy(src_ref.at[pl.ds(s0, subl)], dst_ref.at[pl.ds(d0, subl)], sem)


def _scatter_kernel(npad_ref, dest_ref, pad_ref, h_ref, xs_hbm, stage, zero, sems, pad_sem,
                    *, tm, n_steps):
    i = pl.program_id(0)
    slot = i % 2

    def drain(sl):
        for _ in range(TOP_K):
            pltpu.make_async_copy(stage.at[sl], stage.at[sl], sems.at[sl]).wait()

    @pl.when(i >= 2)
    def _():
        drain(slot)

    packed = _pack_bf16_pairs(h_ref[...])
    for l in range(X_SUBL):
        stage[slot, pl.ds(l, tm, stride=X_SUBL), :] = packed[:, LANES * l:LANES * (l + 1)]

    def issue(t, carry):
        for jx in range(TOP_K):
            _tile_copy(stage.at[slot], t, xs_hbm, dest_ref[0, 0, jx * tm + t], X_SUBL,
                       sems.at[slot]).start(priority=jx % 2)
        return carry

    lax.fori_loop(0, tm, issue, 0, unroll=DMA_LOOP_UNROLL)

    @pl.when(i == 0)
    def _():
        zero[...] = jnp.zeros_like(zero)

        def fill(k, carry):
            _tile_copy(zero, 0, xs_hbm, pad_ref[k], X_SUBL, pad_sem).start()
            return carry

        lax.fori_loop(0, npad_ref[0], fill, 0)

        def fill_wait(k, carry):
            _tile_copy(zero, 0, xs_hbm, 0, X_SUBL, pad_sem).wait()
            return carry

        lax.fori_loop(0, npad_ref[0], fill_wait, 0)

    @pl.when(i == n_steps - 1)
    def _():
        drain(slot)
        if n_steps > 1:
            drain(1 - slot)


def _scatter_rows(h_all, dest, pad_dest, n_pad, n_rows, tm):
    n = h_all.shape[0]
    assert n % tm == 0
    n_steps = n // tm
    dest_blk = dest.reshape(n_steps, tm, TOP_K).transpose(0, 2, 1).reshape(n_steps, 1, TOP_K * tm)
    kern = functools.partial(_scatter_kernel, tm=tm, n_steps=n_steps)
    return pl.pallas_call(
        kern,
        grid_spec=pltpu.PrefetchScalarGridSpec(
            num_scalar_prefetch=1,
            grid=(n_steps,),
            in_specs=[
                pl.BlockSpec((1, 1, TOP_K * tm), lambda i, npad: (i, 0, 0), memory_space=pltpu.SMEM),
                pl.BlockSpec(memory_space=pltpu.SMEM),
                pl.BlockSpec((tm, D_MODEL), lambda i, npad: (i, 0)),
            ],
            out_specs=pl.BlockSpec(memory_space=pl.ANY),
            scratch_shapes=[
                pltpu.VMEM((2, tm * X_SUBL, LANES), jnp.uint32),
                pltpu.VMEM((X_SUBL, LANES), jnp.uint32),
                pltpu.SemaphoreType.DMA((2,)),
                pltpu.SemaphoreType.DMA(()),
            ],
        ),
        out_shape=jax.ShapeDtypeStruct((n_rows * X_SUBL, LANES), jnp.uint32),
        compiler_params=_cparams(("arbitrary",)),
    )(n_pad, dest_blk, pad_dest, h_all)


def _expert_kernel(se_ref, nsub_ref, nused_ref, x_ref, wg_ref, wl_ref, bg_ref, bl_ref, wd_ref,
                   bd_ref, o_ref, xb_scr, acc_scr, *, nf):
    s = pl.program_id(0)
    j = pl.program_id(1)
    n_sub = nsub_ref[s]

    def step(m):
        rows = slice(0, m)

        @pl.when(j == 0)
        def _():
            for l in range(X_SUBL):
                hi, lo = _unpack_bf16_pairs(x_ref[pl.ds(l, m, stride=X_SUBL), :])
                xb_scr[rows, LANES * l:LANES * (l + 1)] = hi.astype(bf16)
                xb_scr[rows, HALF_D + LANES * l:HALF_D + LANES * (l + 1)] = lo.astype(bf16)
            acc_scr[rows, :] = jnp.broadcast_to(bd_ref[...], (m, D_MODEL))

        xb = xb_scr[rows, :]
        hg = jnp.dot(xb, wg_ref[...].astype(bf16), preferred_element_type=f32) + bg_ref[...]
        hl = jnp.dot(xb, wl_ref[...].astype(bf16), preferred_element_type=f32) + bl_ref[...]
        glu = jnp.minimum(hg, SWIGLU_LIMIT)
        lin = jnp.clip(hl, -SWIGLU_LIMIT, SWIGLU_LIMIT)
        act = (glu * (1.0 / (1.0 + jnp.exp(-SWIGLU_ALPHA * glu))) * (lin + 1.0)).astype(bf16)
        wd = wd_ref[...].astype(bf16)
        for n0 in range(0, D_MODEL, MOE_DOWN_N):
            acc_scr[rows, n0:n0 + MOE_DOWN_N] += jnp.dot(
                act, wd[:, n0:n0 + MOE_DOWN_N], preferred_element_type=f32)

        @pl.when(j == nf - 1)
        def _():
            packed = _pack_bf16_pairs(acc_scr[rows, :])
            for l in range(X_SUBL):
                o_ref[pl.ds(l, m, stride=X_SUBL), :] = packed[:, LANES * l:LANES * (l + 1)]

    for k in range(1, MOE_NSUB + 1):
        pl.when(n_sub == k)(functools.partial(step, k * MOE_SUB))


def _experts(xs, super_e, n_sub, n_used, w_up, b_up, w_down, b_down, n_super):
    tf = MOE_FF_TILE
    nf = D_FF // tf
    last = lambda s, nu: jnp.minimum(s, nu[0] - 1)
    b_up3 = b_up.reshape(N_EXPERTS, 1, 2 * D_FF)
    b_down3 = b_down.reshape(N_EXPERTS, 1, D_MODEL)
    e_of = lambda s, se, nu: se[last(s, nu)]
    return pl.pallas_call(
        functools.partial(_expert_kernel, nf=nf),
        grid_spec=pltpu.PrefetchScalarGridSpec(
            num_scalar_prefetch=3,
            grid=(n_super, nf),
            in_specs=[
                pl.BlockSpec((MOE_SUPER * X_SUBL, LANES), lambda s, j, se, ns, nu: (last(s, nu), 0)),
                pl.BlockSpec((None, D_MODEL, tf), lambda s, j, se, ns, nu: (e_of(s, se, nu), 0, j)),
                pl.BlockSpec((None, D_MODEL, tf),
                             lambda s, j, se, ns, nu: (e_of(s, se, nu), 0, nf + j)),
                pl.BlockSpec((None, 1, tf), lambda s, j, se, ns, nu: (e_of(s, se, nu), 0, j)),
                pl.BlockSpec((None, 1, tf), lambda s, j, se, ns, nu: (e_of(s, se, nu), 0, nf + j)),
                pl.BlockSpec((None, tf, D_MODEL), lambda s, j, se, ns, nu: (e_of(s, se, nu), j, 0)),
                pl.BlockSpec((None, 1, D_MODEL), lambda s, j, se, ns, nu: (e_of(s, se, nu), 0, 0)),
            ],
            out_specs=pl.BlockSpec((MOE_SUPER * X_SUBL, LANES),
                                   lambda s, j, se, ns, nu: (last(s, nu), 0)),
            scratch_shapes=[pltpu.VMEM((MOE_SUPER, D_MODEL), bf16),
                            pltpu.VMEM((MOE_SUPER, D_MODEL), f32)],
        ),
        out_shape=jax.ShapeDtypeStruct((n_super * MOE_SUPER * X_SUBL, LANES), jnp.uint32),
        compiler_params=_cparams(("arbitrary", "arbitrary")),
    )(super_e, n_sub, n_used, xs, w_up, w_up, b_up3, b_up3, w_down, b_down3)


def _combine_kernel(dest_ref, nxt_ref, gate_ref, h_ref, rows_hbm, g_ref, b_ref, yp_ref, ys_ref,
                    buf, sems, *, tm, n_first, n_steps):
    i = pl.program_id(0)
    slot = i % 2

    def fetch(idx_ref, sl):
        def body(t, carry):
            for jx in range(TOP_K):
                _tile_copy(rows_hbm, idx_ref[0, 0, jx * tm + t], buf.at[sl, jx], t, X_SUBL,
                           sems.at[sl]).start(priority=jx % 2)
            return carry
        lax.fori_loop(0, tm, body, 0, unroll=DMA_LOOP_UNROLL)

    @pl.when(i == 0)
    def _():
        fetch(dest_ref, slot)

    @pl.when(i + 1 < n_steps)
    def _():
        fetch(nxt_ref, 1 - slot)

    for jx in range(TOP_K):
        pltpu.make_async_copy(buf.at[slot, jx], buf.at[slot, jx], sems.at[slot]).wait()
    gate = gate_ref[...]
    cols_hi, cols_lo = [], []
    for l in range(X_SUBL):
        acc_hi = acc_lo = None
        for jx in range(TOP_K):
            hi, lo = _unpack_bf16_pairs(buf[slot, jx, pl.ds(l, tm, stride=X_SUBL), :])
            gj = gate[:, jx:jx + 1]
            acc_hi = gj * hi if acc_hi is None else acc_hi + gj * hi
            acc_lo = gj * lo if acc_lo is None else acc_lo + gj * lo
        cols_hi.append(acc_hi)
        cols_lo.append(acc_lo)
    y = jnp.concatenate(cols_hi + cols_lo, axis=-1)
    out = _layer_norm(DN_ALPHA * h_ref[...] + y, g_ref[...], b_ref[...])

    @pl.when(i < n_first)
    def _():
        yp_ref[...] = out

    @pl.when(i >= n_first)
    def _():
        ys_ref[...] = out


def _combine(rows_out, dest, gate, h_all, ln_g, ln_b, *, tm, n_first_rows):
    n = h_all.shape[0]
    assert n % tm == 0 and n_first_rows % tm == 0
    nblk = n // tm
    n_first = n_first_rows // tm
    dest_blk = dest.reshape(nblk, tm, TOP_K).transpose(0, 2, 1).reshape(nblk, 1, TOP_K * tm)
    kern = functools.partial(_combine_kernel, tm=tm, n_first=n_first, n_steps=nblk)
    const = lambda i: (0, 0)
    return pl.pallas_call(
        kern,
        grid=(nblk,),
        in_specs=[
            pl.BlockSpec((1, 1, TOP_K * tm), lambda i: (i, 0, 0), memory_space=pltpu.SMEM),
            pl.BlockSpec((1, 1, TOP_K * tm), lambda i: (jnp.minimum(i + 1, nblk - 1), 0, 0),
                         memory_space=pltpu.SMEM),
            pl.BlockSpec((tm, TOP_K), lambda i: (i, 0)),
            pl.BlockSpec((tm, D_MODEL), lambda i: (i, 0)),
            pl.BlockSpec(memory_space=pl.ANY),
            pl.BlockSpec((1, D_MODEL), const),
            pl.BlockSpec((1, D_MODEL), const),
        ],
        out_specs=[
            pl.BlockSpec((tm, D_MODEL), lambda i: (jnp.minimum(i, n_first - 1), 0)),
            pl.BlockSpec((tm, D_MODEL), lambda i: (jnp.maximum(i - n_first, 0), 0)),
        ],
        out_shape=[
            jax.ShapeDtypeStruct((n_first_rows, D_MODEL), f32),
            jax.ShapeDtypeStruct((n - n_first_rows, D_MODEL), f32),
        ],
        scratch_shapes=[pltpu.VMEM((2, TOP_K, tm * X_SUBL, LANES), jnp.uint32),
                        pltpu.SemaphoreType.DMA((2,))],
        compiler_params=_cparams(("arbitrary",)),
    )(dest_blk, dest_blk, gate, h_all, rows_out, ln_g, ln_b)


def _t5_bucket(rel):
    half = NUM_BUCKETS // 2
    exact = half // 2
    n = jnp.abs(rel)
    log_part = exact + (jnp.log(jnp.maximum(n, 1).astype(jnp.float32) / exact)
                        / math.log(MAX_DISTANCE / exact) * (half - exact)).astype(jnp.int32)
    log_part = jnp.minimum(log_part, half - 1)
    return jnp.where(rel > 0, half, 0) + jnp.where(n < exact, n, log_part)


def _band_bias(rel_bias):
    qi = jnp.arange(CHUNK)[:, None]
    km = jnp.arange(BAND)[None, :]
    bucket = _t5_bucket(km - WINDOW - qi)
    onehot = (bucket[..., None] == jnp.arange(NUM_BUCKETS)).astype(f32)
    return jnp.einsum("imb,bh->him", onehot, rel_bias.astype(f32),
                      precision=lax.Precision.HIGHEST)


def _pad_cols(a, width):
    return jnp.pad(a, ((0, 0), (0, width - a.shape[-1])))


def _pair_state(s):
    b = s.shape[0]
    return s.reshape(b, N_PAIRS, 2, RW_HEAD, RW_HEAD).transpose(0, 1, 3, 2, 4).reshape(
        b, N_PAIRS, RW_HEAD, LANES)


def _unpair_state(s):
    b = s.shape[0]
    return s.reshape(b, N_PAIRS, RW_HEAD, 2, RW_HEAD).transpose(0, 1, 3, 2, 4).reshape(
        b, RW_HEADS, RW_HEAD, RW_HEAD)


def _mix_group(x, k_hist, v_hist, hist_valid, wkv0, shift0, bias, sinks3, w_in_pad, rw_wts,
               *, in_tm, attn_nc):
    b, t, _ = x.shape
    q, kv, prw = _inproj(x.reshape(b * t, D_MODEL), w_in_pad, in_tm)
    tp = -(-t // (attn_nc * CHUNK)) * (attn_nc * CHUNK)
    q4 = q if b == 1 else q.reshape(ATT_HEADS, b, t, HEAD_DIM)
    kv3 = kv.reshape(b, t, 2 * KV_W)
    prw3 = prw.reshape(b, t, RW_PAD)
    hist = jnp.concatenate([k_hist.reshape(b, WINDOW, KV_W), v_hist.reshape(b, WINDOW, KV_W)], axis=-1)
    if t >= WINDOW:
        new_kv = kv3[:, t - WINDOW:]
    else:
        new_kv = jnp.concatenate([hist[:, t:], kv3], axis=1)
    if tp != t:
        q4 = jnp.pad(q4, ((0, 0), (0, 0), (0, tp - t), (0, 0)))
        prw3 = jnp.pad(prw3, ((0, 0), (0, tp - t), (0, 0)))
        kv3 = jnp.pad(kv3, ((0, 0), (0, tp - t), (0, 0)))
    att = _attention(q4, hist, kv3, bias, sinks3, nc=attn_nc, t_valid=t, hist_valid=hist_valid)
    rw, s_fin = _rwkv(prw3, _pad_cols(shift0.reshape(b, RW_PROJ), RW_PAD).reshape(b, 1, RW_PAD),
                      _pair_state(wkv0.astype(f32)), rw_wts, t_valid=t)
    new_k = new_kv[..., :KV_W].reshape(b, WINDOW, ATT_KV_HEADS, HEAD_DIM)
    new_v = new_kv[..., KV_W:].reshape(b, WINDOW, ATT_KV_HEADS, HEAD_DIM)
    shift = prw3[:, t - 1:t, :RW_PROJ]
    return (att[:, :t].reshape(b * t, ATT_W), rw[:, :t].reshape(b * t, RW_W),
            new_k, new_v, _unpair_state(s_fin), shift)


SCATTER_TM = 128
COMBINE_TM = 128


def kernel(x_prompt, x_sample, cache_k, cache_v, state_wkv, state_shift, rel_bias, w_in, attn_sinks, rw_mu, rw_w0, rw_decay_up, rw_a0, rw_iclr_up, rw_gate_up, rw_k_k, rw_k_a, rw_r_k, rw_lnx_g, rw_lnx_b, w_out, ln1_g, ln1_b, w_router, b_router, w_up, b_up, w_down, b_down, ln2_g, ln2_b):
    assert w_in.shape[0] == DEPTH == 1
    l = 0
    bp, tp_, _ = x_prompt.shape
    bs, ts, _ = x_sample.shape
    bias = _band_bias(rel_bias)
    sinks3 = attn_sinks[l].astype(f32).reshape(ATT_HEADS, 1, 1)

    w_in_pad = _pad_cols(w_in[l], IN_PAD).astype(bf16)

    def lora_rows(w, row0):
        return jnp.zeros((LORA_PAD, RW_W), f32).at[row0:row0 + w.shape[0]].set(w).astype(bf16)

    rw_wts = {
        "mu": _pad_cols(rw_mu[l].reshape(1, RW_PROJ), RW_PAD),
        "w0": rw_w0[l].reshape(1, RW_W),
        "wd": lora_rows(rw_decay_up[l], 0),
        "a0": rw_a0[l].reshape(1, RW_W),
        "wa": lora_rows(rw_iclr_up[l], DECAY_LORA),
        "wg": lora_rows(rw_gate_up[l], DECAY_LORA + ICLR_LORA),
        "k_k": rw_k_k[l].reshape(1, RW_W),
        "k_a": rw_k_a[l].reshape(1, RW_W),
        "r_k": rw_r_k[l].reshape(1, RW_W),
        "lnx_g": rw_lnx_g[l].reshape(1, RW_W),
        "lnx_b": rw_lnx_b[l].reshape(1, RW_W),
    }
    wr = _pad_cols(w_router[l], LANES)
    wr_hi = wr.astype(bf16)
    op_wts = {
        "w_out": w_out[l].astype(bf16),
        "ln1_g": ln1_g[l].reshape(1, D_MODEL),
        "ln1_b": ln1_b[l].reshape(1, D_MODEL),
        "wr_hi": wr_hi,
        "wr_lo": (wr - wr_hi.astype(f32)).astype(bf16),
        "b_router": jnp.concatenate([b_router[l].astype(f32),
                                     jnp.full((LANES - N_EXPERTS,), NEG_BIG, f32)]).reshape(1, LANES),
    }

    zero_kv = jnp.zeros((bp, WINDOW, ATT_KV_HEADS, HEAD_DIM), f32)
    att_p, rwo_p, k1, v1, w1, s1 = _mix_group(
        x_prompt, zero_kv, zero_kv, False, jnp.zeros((bp, RW_HEADS, RW_HEAD, RW_HEAD), f32),
        jnp.zeros((bp, 1, RW_PROJ), f32), bias, sinks3, w_in_pad, rw_wts,
        in_tm=min(256, bp * tp_), attn_nc=min(8, -(-tp_ // CHUNK)))
    att_s, rwo_s, k2, v2, w2, s2 = _mix_group(
        x_sample, cache_k[l], cache_v[l], True, state_wkv[l], state_shift[l], bias, sinks3,
        w_in_pad, rw_wts, in_tm=min(256, bs * ts), attn_nc=1)

    n_p, n_s = bp * tp_, bs * ts
    n_all = n_p + n_s
    tm_p, tm_s = min(512, n_p), min(128, n_s)
    *outs, counts_p = _outproj(att_p, rwo_p, x_prompt.reshape(n_p, D_MODEL), op_wts,
                               tm=tm_p, row0=0, total_rows=n_all)
    h_all, top_idx, gate, rank, counts = _outproj(
        att_s, rwo_s, x_sample.reshape(n_s, D_MODEL), op_wts,
        tm=tm_s, row0=n_p, total_rows=n_all, prev=outs, counts0=counts_p)

    dest, super_e, n_sub, n_used, pad_dest, n_pad, n_super = _routing(top_idx, rank, counts)
    xs = _scatter_rows(h_all, dest, pad_dest, n_pad, n_super * MOE_SUPER, min(SCATTER_TM, n_s))
    rows_out = _experts(xs, super_e, n_sub, n_used, w_up[l], b_up[l], w_down[l], b_down[l],
                        n_super)
    y_p, y_s = _combine(rows_out, dest, gate, h_all, ln2_g[l].reshape(1, D_MODEL),
                        ln2_b[l].reshape(1, D_MODEL), tm=min(COMBINE_TM, n_s), n_first_rows=n_p)

    return (y_p.reshape(bp, tp_, D_MODEL), y_s.reshape(bs, ts, D_MODEL),
            k1[None], v1[None], w1[None], s1[None], k2[None], v2[None], w2[None], s2[None])
```

```python
import functools
import math

import jax
import jax.numpy as jnp
from jax import lax
from jax.experimental import pallas as pl
from jax.experimental.pallas import tpu as pltpu

f32 = jnp.float32
bf16 = jnp.bfloat16

D_MODEL = 2048
CHUNK = 64
ATT_HEADS = 16
ATT_KV_HEADS = 2
HEAD_DIM = 64
ATT_GROUP = ATT_HEADS // ATT_KV_HEADS
ATT_W = ATT_HEADS * HEAD_DIM
KV_W = ATT_KV_HEADS * HEAD_DIM
ATT_PROJ = ATT_W + 2 * KV_W
WINDOW = 128
BAND = WINDOW + CHUNK
NUM_BUCKETS = 32
MAX_DISTANCE = 128
RW_HEAD = 64
RW_W = 1024
RW_HEADS = RW_W // RW_HEAD
DECAY_LORA = 96
ICLR_LORA = 96
GATE_LORA = 128
RW_PROJ = 3 * RW_W + DECAY_LORA + ICLR_LORA + GATE_LORA
GN_EPS = 64e-5
LN_EPS = 1e-5
N_EXPERTS = 32
TOP_K = 4
D_FF = D_MODEL
SWIGLU_LIMIT = 7.0
SWIGLU_ALPHA = 1.702
DEPTH = 1
DN_ALPHA = (2 * DEPTH) ** 0.25

LANES = 128
VMEM_LIMIT = 56 * 1024 * 1024

LORA_W = DECAY_LORA + ICLR_LORA + GATE_LORA
LORA_PAD = -(-LORA_W // LANES) * LANES
RW_PAD = 3 * RW_W + LORA_PAD
IN_PAD = ATT_PROJ + RW_PAD
N_PAIRS = RW_HEADS // 2
NEG_BIG = -1e30


def _cparams(sem):
    return pltpu.CompilerParams(dimension_semantics=sem, vmem_limit_bytes=VMEM_LIMIT)


def _inproj_kernel(x_ref, w_ref, q_ref, kv_ref, rw_ref):
    acc = jnp.dot(x_ref[...].astype(bf16), w_ref[...], preferred_element_type=f32)
    scale = HEAD_DIM ** -0.5
    for h in range(ATT_HEADS):
        q_ref[h] = (acc[:, h * HEAD_DIM:(h + 1) * HEAD_DIM] * scale).astype(bf16)
    kv_ref[...] = acc[:, ATT_W:ATT_PROJ]
    rw_ref[...] = acc[:, ATT_PROJ:IN_PAD]


def _inproj(x2, w_pad, tm):
    rows = x2.shape[0]
    assert rows % tm == 0
    return pl.pallas_call(
        _inproj_kernel,
        grid=(rows // tm,),
        in_specs=[
            pl.BlockSpec((tm, D_MODEL), lambda i: (i, 0)),
            pl.BlockSpec((D_MODEL, IN_PAD), lambda i: (0, 0), pipeline_mode=pl.Buffered(1)),
        ],
        out_specs=[
            pl.BlockSpec((ATT_HEADS, None, tm, HEAD_DIM), lambda i: (0, 0, i, 0)),
            pl.BlockSpec((tm, 2 * KV_W), lambda i: (i, 0)),
            pl.BlockSpec((tm, RW_PAD), lambda i: (i, 0)),
        ],
        out_shape=[
            jax.ShapeDtypeStruct((ATT_HEADS, 1, rows, HEAD_DIM), bf16),
            jax.ShapeDtypeStruct((rows, 2 * KV_W), f32),
            jax.ShapeDtypeStruct((rows, RW_PAD), f32),
        ],
        compiler_params=_cparams(("parallel",)),
    )(x2, w_pad)


ATT_UNROLL = 8


def _attn_kernel(q_ref, hist_ref, kva_ref, kvb_ref, kvm_ref, bias_ref, sink_ref, o_ref, kvbuf,
                 *, nc, t_valid, hist_valid):
    j = pl.program_id(1)
    first = j == 0
    kvbuf[0:CHUNK] = jnp.where(first, hist_ref[0:CHUNK], kva_ref[...]).astype(bf16)
    kvbuf[CHUNK:WINDOW] = jnp.where(first, hist_ref[CHUNK:WINDOW], kvb_ref[...]).astype(bf16)
    kvbuf[WINDOW:WINDOW + nc * CHUNK] = kvm_ref[...].astype(bf16)
    m_idx = lax.broadcasted_iota(jnp.int32, (1, 1, BAND), 2)
    for c0 in range(0, nc, ATT_UNROLL):
        items = [(c, g) for c in range(c0, min(c0 + ATT_UNROLL, nc)) for g in range(ATT_KV_HEADS)]
        bands = {c: kvbuf[c * CHUNK:c * CHUNK + BAND, :] for c, _ in items}
        valid = {}
        for c in bands:
            idx = (j * nc + c) * CHUNK + m_idx
            v = idx - WINDOW < t_valid
            valid[c] = v if hist_valid else jnp.logical_and(v, idx >= WINDOW)
        s = [lax.dot_general(
                q_ref[g * ATT_GROUP:(g + 1) * ATT_GROUP, c * CHUNK:(c + 1) * CHUNK, :].reshape(
                    ATT_GROUP * CHUNK, HEAD_DIM),
                bands[c][:, g * HEAD_DIM:(g + 1) * HEAD_DIM],
                (((1,), (1,)), ((), ())), preferred_element_type=f32) for c, g in items]
        s = [jnp.where(valid[c], s[i].reshape(ATT_GROUP, CHUNK, BAND)
                       + bias_ref[g * ATT_GROUP:(g + 1) * ATT_GROUP], NEG_BIG)
             for i, (c, g) in enumerate(items)]
        sk = [sink_ref[g * ATT_GROUP:(g + 1) * ATT_GROUP] for _, g in items]
        m = [jnp.maximum(jnp.max(s[i], axis=-1, keepdims=True), sk[i]) for i in range(len(items))]
        p = [jnp.exp(s[i] - m[i]) for i in range(len(items))]
        den = [jnp.sum(p[i], axis=-1, keepdims=True) + jnp.exp(sk[i] - m[i])
               for i in range(len(items))]
        o = [jnp.dot(p[i].reshape(ATT_GROUP * CHUNK, BAND).astype(bf16),
                     bands[c][:, KV_W + g * HEAD_DIM:KV_W + (g + 1) * HEAD_DIM],
                     preferred_element_type=f32).reshape(ATT_GROUP, CHUNK, HEAD_DIM) / den[i]
             for i, (c, g) in enumerate(items)]
        for c in bands:
            heads = [o[i][h] for i, (ci, _) in enumerate(items) if ci == c for h in range(ATT_GROUP)]
            o_ref[c * CHUNK:(c + 1) * CHUNK, :] = jnp.concatenate(heads, axis=-1).astype(bf16)


def _attention(q4, hist, kv, bias, sinks3, *, nc, t_valid, hist_valid):
    _, b, tp, _ = q4.shape
    assert tp % (nc * CHUNK) == 0 and kv.shape[1] == tp
    nblk = tp // (nc * CHUNK)
    kern = functools.partial(_attn_kernel, nc=nc, t_valid=t_valid, hist_valid=hist_valid)
    return pl.pallas_call(
        kern,
        grid=(b, nblk),
        in_specs=[
            pl.BlockSpec((ATT_HEADS, None, nc * CHUNK, HEAD_DIM), lambda bi, j: (0, bi, j, 0)),
            pl.BlockSpec((None, WINDOW, 2 * KV_W), lambda bi, j: (bi, 0, 0)),
            pl.BlockSpec((None, CHUNK, 2 * KV_W), lambda bi, j: (bi, jnp.maximum(j * nc - 2, 0), 0)),
            pl.BlockSpec((None, CHUNK, 2 * KV_W), lambda bi, j: (bi, jnp.maximum(j * nc - 1, 0), 0)),
            pl.BlockSpec((None, nc * CHUNK, 2 * KV_W), lambda bi, j: (bi, j, 0)),
            pl.BlockSpec((ATT_HEADS, CHUNK, BAND), lambda bi, j: (0, 0, 0)),
            pl.BlockSpec((ATT_HEADS, 1, 1), lambda bi, j: (0, 0, 0)),
        ],
        out_specs=pl.BlockSpec((None, nc * CHUNK, ATT_W), lambda bi, j: (bi, j, 0)),
        out_shape=jax.ShapeDtypeStruct((b, tp, ATT_W), bf16),
        scratch_shapes=[pltpu.VMEM(((nc + 2) * CHUNK, 2 * KV_W), bf16)],
        compiler_params=_cparams(("parallel", "parallel")),
    )(q4, hist, kv, kv, kv, bias, sinks3)


N_OPS_BF16 = 7
N_OPS_F32 = 4


def _rwkv_step_kernel(p_ref, shift0_ref, s0_ref, mu_ref, w0_ref, wd_ref, a0_ref, wa_ref, wg_ref,
                      kk_ref, ka_ref, rk_ref, lng_ref, lnb_ref, o_ref, sfin_ref,
                      s_scr, last_scr, opb_scr, opf_scr, cl_scr, *, t_valid, n_steps, nch):
    c = pl.program_id(1)
    L = CHUNK
    R = nch * L

    @pl.when(c == 0)
    def _():
        s_scr[...] = s0_ref[...]
        last_scr[...] = shift0_ref[...]
        opb_scr[...] = jnp.zeros_like(opb_scr)
        opf_scr[...] = jnp.zeros_like(opf_scr)
        cl_scr[...] = jnp.zeros_like(cl_scr)

    kq, rq, bt, kt, bh, kh, vb = range(N_OPS_BF16)
    g_l = [jnp.exp(cl_scr[ch]) for ch in range(nch)]

    lane = lax.broadcasted_iota(jnp.int32, (1, LANES), 1)
    lo_half = lane < RW_HEAD
    rr = lax.broadcasted_iota(jnp.int32, (LANES, LANES), 0)
    cc = lax.broadcasted_iota(jnp.int32, (LANES, LANES), 1)
    same_head = (rr // RW_HEAD) == (cc // RW_HEAD)
    r4 = lax.broadcasted_iota(jnp.int32, (2 * LANES, 2 * LANES), 0)
    c4 = lax.broadcasted_iota(jnp.int32, (2 * LANES, 2 * LANES), 1)
    ones_bd = jnp.where((r4 // RW_HEAD) == (c4 // RW_HEAD), 1.0, 0.0).astype(bf16)
    rs = [slice(L * ch, L * (ch + 1)) for ch in range(nch)]
    sl = [slice(LANES * t, LANES * (t + 1)) for t in range(N_PAIRS)]
    n_quad = RW_W // (2 * LANES)

    def seg_sums(xs):
        n = len(xs)
        x = jnp.concatenate(xs, axis=0) if n > 1 else xs[0]
        hi = x.astype(bf16)
        lo = (x - hi.astype(f32)).astype(bf16)
        both = jnp.concatenate([hi, lo], axis=0)
        m = 2 * n * R
        tiles = jnp.concatenate([both[:, 2 * LANES * t:2 * LANES * (t + 1)] for t in range(n_quad)],
                                axis=0)
        res = jnp.dot(tiles, ones_bd, preferred_element_type=f32)
        y = jnp.concatenate([res[m * t:m * (t + 1)] for t in range(n_quad)], axis=1)
        y = y[:n * R] + y[n * R:]
        return [y[i * R:(i + 1) * R] for i in range(n)]

    def sigmoid(z):
        return 1.0 / (1.0 + jnp.exp(-z))

    p = p_ref[...]
    row = lax.broadcasted_iota(jnp.int32, (R, 1), 0)
    shifted = jnp.where(row == 0, last_scr[...], pltpu.roll(p, 1, axis=0))
    last_scr[...] = p[R - 1:R, :]
    xm = p + (shifted - p) * mu_ref[...]
    r = xm[:, 0:RW_W]
    k = xm[:, RW_W:2 * RW_W]
    v = xm[:, 2 * RW_W:3 * RW_W]
    tail = xm[:, 3 * RW_W:RW_PAD]
    w_log = w0_ref[...] + jnp.dot(jnp.tanh(tail).astype(bf16), wd_ref[...],
                                  preferred_element_type=f32)
    a = sigmoid(a0_ref[...] + jnp.dot(tail.astype(bf16), wa_ref[...], preferred_element_type=f32))
    g = jnp.dot(sigmoid(tail).astype(bf16), wg_ref[...], preferred_element_type=f32)

    def prep_mid():
        z = -w_log
        softplus = jnp.maximum(z, 0.0) + jnp.log(1.0 + jnp.exp(-jnp.abs(z)))
        ld = -jnp.exp(-softplus - 0.5)
        kk = k * kk_ref[...]
        k_mod = k * (1.0 + (a - 1.0) * ka_ref[...])
        nrm2, bonus_s = seg_sums([kk * kk, r * k_mod * rk_ref[...]])
        kk = kk / jnp.maximum(jnp.sqrt(nrm2), 1e-12)
        b = kk * a
        if t_valid % R != 0:
            live = (jnp.minimum(c, n_steps - 1) * R + row) < t_valid
            ld = jnp.where(live, ld, 0.0)
            b = jnp.where(live, b, 0.0)
            k_mod = jnp.where(live, k_mod, 0.0)
        h1 = ld.astype(bf16)
        r1 = ld - h1.astype(f32)
        h2 = r1.astype(bf16)
        h3 = (r1 - h2.astype(f32)).astype(bf16)
        ti = lax.broadcasted_iota(jnp.int32, (L, 3 * L), 0)
        si = lax.broadcasted_iota(jnp.int32, (L, 3 * L), 1) % L
        tri3 = jnp.where(si <= ti, 1.0, 0.0).astype(bf16)
        cums = [jnp.dot(tri3, jnp.concatenate([h1[rs[ch]], h2[rs[ch]], h3[rs[ch]]], axis=0),
                        preferred_element_type=f32) for ch in range(nch)]
        return kk, k_mod, b, ld, cums, bonus_s

    def prep_tail(kk, k_mod, b, ld, cums, bonus_s):
        cum = jnp.concatenate(cums, axis=0) if nch > 1 else cums[0]
        cum_ls = [cums[ch][L - 1:L, :] for ch in range(nch)]
        cum_l = (jnp.concatenate([jnp.broadcast_to(x, (L, RW_W)) for x in cum_ls], axis=0)
                 if nch > 1 else cum_ls[0])
        g_inv = jnp.exp(-cum)
        g_rest = jnp.exp(cum_l - cum)
        n_rq = r * jnp.exp(cum)
        ops_b = [(kk * jnp.exp(cum - ld)).astype(bf16), n_rq.astype(bf16),
                 (b * g_inv).astype(bf16), (k_mod * g_inv).astype(bf16),
                 (b * g_rest).astype(bf16), (k_mod * g_rest).astype(bf16), v.astype(bf16)]
        return ops_b, [n_rq, v, g, bonus_s], cum_ls

    def bd(x):
        zero = jnp.zeros_like(x)
        return jnp.concatenate([jnp.where(lo_half, x, zero), jnp.where(lo_half, zero, x)], axis=0)

    def mm(x, y):
        return jnp.dot(x, y, preferred_element_type=f32)

    def mm_nt(x, y):
        return lax.dot_general(x, y, (((1,), (1,)), ((), ())), preferred_element_type=f32)

    def mm_tn(x, y):
        return lax.dot_general(x, y, (((0,), (0,)), ((), ())), preferred_element_type=f32)

    tt = lax.broadcasted_iota(jnp.int32, (L, LANES), 0)
    ss = lax.broadcasted_iota(jnp.int32, (L, LANES), 1) % RW_HEAD
    strict = ss < tt
    incl = ss <= tt
    eye_pair = jnp.where(ss == tt, 1.0, 0.0).astype(f32)

    items = [(ch, t) for ch in range(nch) for t in range(N_PAIRS)]
    I = range(len(items))

    def tile(slot, i):
        ch, t = items[i]
        return opb_scr[slot, rs[ch], sl[t]]

    a_all = [mm_nt(jnp.concatenate([tile(kq, i), tile(rq, i)], axis=0),
                   jnp.concatenate([bd(tile(bt, i)), bd(tile(kt, i))], axis=0)) for i in I]
    a_bk = [jnp.where(strict, a_all[i][:L, :LANES], 0.0) for i in I]
    a_kk = [jnp.where(strict, a_all[i][:L, LANES:], 0.0).astype(bf16) for i in I]
    a_rb = [jnp.where(incl, a_all[i][L:, :LANES], 0.0).astype(bf16) for i in I]
    a_rk = [jnp.where(incl, a_all[i][L:, LANES:], 0.0).astype(bf16) for i in I]
    bd_v = [bd(tile(vb, i)) for i in I]
    akv = [mm(a_kk[i], bd_v[i]).astype(bf16) for i in I]
    w_inv = [eye_pair - a_bk[i] for i in I]
    pw = [a_bk[i].astype(bf16) for i in I]
    pw_bd = [bd(pw[i]) for i in I]
    for it in range(5):
        pw = [mm(pw[i], pw_bd[i]).astype(bf16) for i in I]
        pw_bd = [bd(pw[i]) for i in I]
        w_inv = [w_inv[i] + mm(w_inv[i].astype(bf16), pw_bd[i]) for i in I]
        if it == 2:
            mid = prep_mid()
    qu = [mm(w_inv[i].astype(bf16), jnp.concatenate([bd(tile(kq, i)), bd(akv[i])], axis=1))
          for i in I]
    q_m = [qu[i][:, :LANES].astype(bf16) for i in I]
    u_m = [qu[i][:, LANES:].astype(bf16) for i in I]
    m_full = [mm_tn(q_m[i], tile(bh, i)) for i in I]
    neg_m = [jnp.where(same_head, -m_full[i], 0.0).astype(bf16) for i in I]
    c_full = [mm_tn(jnp.concatenate([tile(vb, i), u_m[i]], axis=0),
                    jnp.concatenate([tile(kh, i), -tile(bh, i)], axis=0)) for i in I]
    go = [mm(a_rb[i], jnp.concatenate([bd(q_m[i]), bd(u_m[i])], axis=1)) for i in I]
    o_rk = [mm(a_rk[i], bd_v[i]) for i in I]
    new_b, new_f, new_cl = prep_tail(*mid)
    g_m = [(opf_scr[0, rs[items[i][0]], sl[items[i][1]]] - go[i][:, :LANES]).astype(bf16)
           for i in I]
    o_in = [o_rk[i] - go[i][:, LANES:] for i in I]
    c_pair = [jnp.where(lo_half, c_full[i][:RW_HEAD], c_full[i][RW_HEAD:]) for i in I]
    s_cur = [s_scr[t] for t in range(N_PAIRS)]
    o_rows = []
    for ch in range(nch):
        ii = [ch * N_PAIRS + t for t in range(N_PAIRS)]
        s_b = [s.astype(bf16) for s in s_cur]
        o_rows.append(jnp.concatenate(
            [mm_nt(g_m[i], bd(s_b[t])) + o_in[i] for t, i in enumerate(ii)], axis=1))
        s_upd = [mm(s_b[t], neg_m[i]) for t, i in enumerate(ii)]
        s_cur = [s_cur[t] * g_l[ch][:, sl[t]] + s_upd[t] + c_pair[i] for t, i in enumerate(ii)]
    for t in range(N_PAIRS):
        s_scr[t] = s_cur[t]

    o = jnp.concatenate(o_rows, axis=0) if nch > 1 else o_rows[0]
    (o_sum,) = seg_sums([o])
    d = o - o_sum * (1.0 / RW_HEAD)
    (d2,) = seg_sums([d * d])
    on = d * lax.rsqrt(d2 * (1.0 / RW_HEAD) + GN_EPS) * lng_ref[...] + lnb_ref[...]
    o_ref[...] = ((on + opf_scr[3] * opf_scr[1]) * opf_scr[2]).astype(bf16)

    for i in range(N_OPS_BF16):
        opb_scr[i] = new_b[i]
    for i in range(N_OPS_F32):
        opf_scr[i] = new_f[i]
    for ch in range(nch):
        cl_scr[ch] = new_cl[ch]

    @pl.when(c == n_steps)
    def _():
        sfin_ref[...] = s_scr[...]


RWKV_CHUNKS_PER_STEP = 2


def _rwkv(prw, shift0, s0_pair, wts, *, t_valid):
    b, tp, _ = prw.shape
    nch = RWKV_CHUNKS_PER_STEP if tp % (RWKV_CHUNKS_PER_STEP * CHUNK) == 0 else 1
    rows = nch * CHUNK
    n_steps = tp // rows
    kern = functools.partial(_rwkv_step_kernel, t_valid=t_valid, n_steps=n_steps, nch=nch)
    const2 = lambda bi, c: (0, 0)
    row_spec = pl.BlockSpec((1, RW_W), const2)
    return pl.pallas_call(
        kern,
        grid=(b, n_steps + 1),
        in_specs=[
            pl.BlockSpec((None, rows, RW_PAD), lambda bi, c: (bi, jnp.minimum(c, n_steps - 1), 0)),
            pl.BlockSpec((None, 1, RW_PAD), lambda bi, c: (bi, 0, 0)),
            pl.BlockSpec((None, N_PAIRS, RW_HEAD, LANES), lambda bi, c: (bi, 0, 0, 0)),
            pl.BlockSpec((1, RW_PAD), const2),
            row_spec,
            pl.BlockSpec((LORA_PAD, RW_W), const2),
            row_spec,
            pl.BlockSpec((LORA_PAD, RW_W), const2),
            pl.BlockSpec((LORA_PAD, RW_W), const2),
            row_spec, row_spec, row_spec, row_spec, row_spec,
        ],
        out_specs=[
            pl.BlockSpec((None, rows, RW_W), lambda bi, c: (bi, jnp.maximum(c - 1, 0), 0)),
            pl.BlockSpec((None, N_PAIRS, RW_HEAD, LANES), lambda bi, c: (bi, 0, 0, 0)),
        ],
        out_shape=[
            jax.ShapeDtypeStruct((b, tp, RW_W), bf16),
            jax.ShapeDtypeStruct((b, N_PAIRS, RW_HEAD, LANES), f32),
        ],
        scratch_shapes=[pltpu.VMEM((N_PAIRS, RW_HEAD, LANES), f32), pltpu.VMEM((1, RW_PAD), f32),
                        pltpu.VMEM((N_OPS_BF16, rows, RW_W), bf16),
                        pltpu.VMEM((N_OPS_F32, rows, RW_W), f32),
                        pltpu.VMEM((nch, 1, RW_W), f32)],
        compiler_params=_cparams(("parallel", "arbitrary")),
    )(prw, shift0, s0_pair, wts["mu"], wts["w0"], wts["wd"], wts["a0"], wts["wa"], wts["wg"],
      wts["k_k"], wts["k_a"], wts["r_k"], wts["lnx_g"], wts["lnx_b"])


def _layer_norm(z, g, b):
    mu = jnp.mean(z, axis=-1, keepdims=True)
    d = z - mu
    var = jnp.mean(d * d, axis=-1, keepdims=True)
    return d * lax.rsqrt(var + LN_EPS) * g + b


N_SHARED_OUT = 4
OUTPROJ_PARTS = 2


def _outproj_kernel(*refs, aliased, n_steps):
    att_ref, rw_ref, x_ref, wo_ref, g_ref, b_ref, wrh_ref, wrl_ref, br_ref, cnt0_ref = refs[:10]
    h_ref, idx_ref, gate_ref, rank_ref, cnt_ref, carry = refs[10 + N_SHARED_OUT * aliased:]
    step = pl.program_id(0)

    @pl.when(step == 0)
    def _():
        carry[...] = cnt0_ref[...]

    tm = x_ref.shape[0]
    n_part = OUTPROJ_PARTS if tm % (OUTPROJ_PARTS * 8) == 0 else 1
    pm = tm // n_part
    parts = [slice(q * pm, (q + 1) * pm) for q in range(n_part)]
    mix = [jnp.dot(att_ref[r, :], wo_ref[0:ATT_W], preferred_element_type=f32)
           + jnp.dot(rw_ref[r, :], wo_ref[ATT_W:ATT_W + RW_W], preferred_element_type=f32)
           for r in parts]
    h = [_layer_norm(DN_ALPHA * x_ref[r, :] + mix[q], g_ref[...], b_ref[...])
         for q, r in enumerate(parts)]
    for q, r in enumerate(parts):
        h_ref[r, :] = h[q]
    hh = [x.astype(bf16) for x in h]
    hl = [(h[q] - hh[q].astype(f32)).astype(bf16) for q in range(n_part)]
    logits = [(jnp.dot(hh[q], wrh_ref[...], preferred_element_type=f32)
               + jnp.dot(hl[q], wrh_ref[...], preferred_element_type=f32)
               + jnp.dot(hh[q], wrl_ref[...], preferred_element_type=f32)) + br_ref[...]
              for q in range(n_part)]
    lane = lax.broadcasted_iota(jnp.int32, (pm, LANES), 1).astype(f32)
    ti = lax.broadcasted_iota(jnp.int32, (pm, pm), 0)
    si = lax.broadcasted_iota(jnp.int32, (pm, pm), 1)
    before = jnp.where(si < ti, 1.0, 0.0).astype(bf16)
    counts = carry[...]
    for q, r in enumerate(parts):
        vals, idxs = [], []
        cur = logits[q]
        for _ in range(TOP_K):
            m = jnp.max(cur, axis=-1, keepdims=True)
            i = jnp.min(jnp.where(cur == m, lane, float(LANES)), axis=-1, keepdims=True)
            vals.append(m)
            idxs.append(i)
            cur = jnp.where(lane == i, -jnp.inf, cur)
        es = [jnp.exp(vv - vals[0]) for vv in vals]
        tot = es[0] + es[1] + es[2] + es[3]
        idx_ref[r, :] = jnp.concatenate(idxs, axis=-1).astype(jnp.int32)
        gate_ref[r, :] = jnp.concatenate([e / tot for e in es], axis=-1)
        hits = [jnp.where(lane == i, 1.0, 0.0) for i in idxs]
        multi = hits[0] + hits[1] + hits[2] + hits[3]
        base = counts + jnp.dot(before, multi.astype(bf16), preferred_element_type=f32)
        ranks = [jnp.sum(hh_ * base, axis=-1, keepdims=True) for hh_ in hits]
        rank_ref[r, :] = jnp.concatenate(ranks, axis=-1).astype(jnp.int32)
        counts = counts + jnp.sum(multi, axis=0, keepdims=True)
    carry[...] = counts

    @pl.when(step == n_steps - 1)
    def _():
        cnt_ref[...] = carry[...]


def _outproj(att2, rw2, x2, wts, *, tm, row0, total_rows, prev=None, counts0=None):
    rows = x2.shape[0]
    assert rows % tm == 0 and row0 % tm == 0
    blk0 = row0 // tm
    aliased = prev is not None
    if counts0 is None:
        counts0 = jnp.zeros((1, LANES), f32)
    const = lambda i: (0, 0)
    in_specs = [
        pl.BlockSpec((tm, ATT_W), lambda i: (i, 0)),
        pl.BlockSpec((tm, RW_W), lambda i: (i, 0)),
        pl.BlockSpec((tm, D_MODEL), lambda i: (i, 0)),
        pl.BlockSpec((D_MODEL, D_MODEL), const, pipeline_mode=pl.Buffered(1)),
        pl.BlockSpec((1, D_MODEL), const),
        pl.BlockSpec((1, D_MODEL), const),
        pl.BlockSpec((D_MODEL, LANES), const),
        pl.BlockSpec((D_MODEL, LANES), const),
        pl.BlockSpec((1, LANES), const),
        pl.BlockSpec((1, LANES), const),
    ]
    args = [att2, rw2, x2, wts["w_out"], wts["ln1_g"], wts["ln1_b"], wts["wr_hi"], wts["wr_lo"],
            wts["b_router"], counts0]
    aliases = {}
    if aliased:
        in_specs += [pl.BlockSpec(memory_space=pl.ANY)] * N_SHARED_OUT
        args += list(prev)
        aliases = {len(args) - N_SHARED_OUT + k: k for k in range(N_SHARED_OUT)}
    n_steps = rows // tm
    return pl.pallas_call(
        functools.partial(_outproj_kernel, aliased=int(aliased), n_steps=n_steps),
        grid=(n_steps,),
        in_specs=in_specs,
        out_specs=[
            pl.BlockSpec((tm, D_MODEL), lambda i: (blk0 + i, 0)),
            pl.BlockSpec((tm, TOP_K), lambda i: (blk0 + i, 0)),
            pl.BlockSpec((tm, TOP_K), lambda i: (blk0 + i, 0)),
            pl.BlockSpec((tm, TOP_K), lambda i: (blk0 + i, 0)),
            pl.BlockSpec((1, LANES), const),
        ],
        out_shape=[
            jax.ShapeDtypeStruct((total_rows, D_MODEL), f32),
            jax.ShapeDtypeStruct((total_rows, TOP_K), jnp.int32),
            jax.ShapeDtypeStruct((total_rows, TOP_K), f32),
            jax.ShapeDtypeStruct((total_rows, TOP_K), jnp.int32),
            jax.ShapeDtypeStruct((1, LANES), f32),
        ],
        scratch_shapes=[pltpu.VMEM((1, LANES), f32)],
        input_output_aliases=aliases,
        compiler_params=_cparams(("arbitrary",)),
    )(*args)


MOE_SUB = 256
MOE_NSUB = 5
MOE_SUPER = MOE_SUB * MOE_NSUB
MOE_FF_TILE = 256
MOE_DOWN_N = 512
MOE_MAX_PAD = N_EXPERTS * (MOE_SUB - 1)
X_SUBL = D_MODEL // (2 * LANES)
HALF_D = D_MODEL // 2
DMA_LOOP_UNROLL = 4


def _pack_bf16_pairs(x):
    hi = lax.bitcast_convert_type(x[:, :HALF_D].astype(bf16).astype(f32), jnp.uint32)
    lo = lax.bitcast_convert_type(x[:, HALF_D:].astype(bf16).astype(f32), jnp.uint32)
    return hi | (lo >> 16)


def _unpack_bf16_pairs(u):
    hi = lax.bitcast_convert_type(u & jnp.uint32(0xFFFF0000), f32)
    lo = lax.bitcast_convert_type(u << 16, f32)
    return hi, lo


def _routing(top_idx, rank, counts_f):
    n = top_idx.shape[0]
    n_assign = n * TOP_K
    counts = counts_f[0, :N_EXPERTS].astype(jnp.int32)
    padded = (counts + MOE_SUPER - 1) // MOE_SUPER * MOE_SUPER
    pad_end = jnp.cumsum(padded)
    start = pad_end - padded
    experts = jnp.arange(N_EXPERTS, dtype=jnp.int32)

    def lookup(table, idx):
        return jnp.sum(jnp.where(idx[..., None] == experts, table, 0), axis=-1)

    def bucket(edges, x):
        return jnp.minimum(jnp.sum((edges <= x[..., None]).astype(jnp.int32), axis=-1),
                           N_EXPERTS - 1)

    dest = (lookup(start, top_idx) + rank).astype(jnp.int32)
    n_super = (n_assign + N_EXPERTS * (MOE_SUPER - 1) + MOE_SUPER - 1) // MOE_SUPER
    s_row0 = jnp.arange(n_super, dtype=jnp.int32) * MOE_SUPER
    super_e = bucket(pad_end, s_row0)
    rows_here = jnp.clip(lookup(counts, super_e) - (s_row0 - lookup(start, super_e)), 0, MOE_SUPER)
    rows_here = jnp.where(s_row0 < pad_end[-1], rows_here, 0)
    n_sub = ((rows_here + MOE_SUB - 1) // MOE_SUB).astype(jnp.int32)
    n_used = (pad_end[-1] // MOE_SUPER).astype(jnp.int32).reshape(1)
    n_pad_e = (counts + MOE_SUB - 1) // MOE_SUB * MOE_SUB - counts
    pad_cum = jnp.cumsum(n_pad_e)
    kk = jnp.arange(MOE_MAX_PAD, dtype=jnp.int32)
    pe = bucket(pad_cum, kk)
    pad_dest = (lookup(start + counts - (pad_cum - n_pad_e), pe) + kk).astype(jnp.int32)
    pad_dest = jnp.where(kk < pad_cum[-1], pad_dest, 0)
    n_pad = pad_cum[-1].astype(jnp.int32).reshape(1)
    return dest, super_e, n_sub, n_used, pad_dest, n_pad, n_super


def _tile_copy(src_ref, src_tok, dst_ref, dst_tok, subl, sem):
    s0 = pl.multiple_of(src_tok * subl, subl)
    d0 = pl.multiple_of(dst_tok * subl, subl)
    return pltpu.make_async_copy(src_ref.at[pl.ds(s0, subl)], dst_ref.at[pl.ds(d0, subl)], sem)


def _scatter_kernel(npad_ref, dest_ref, pad_ref, h_ref, xs_hbm, stage, zero, sems, pad_sem,
                    *, tm, n_steps):
    i = pl.program_id(0)
    slot = i % 2

    def drain(sl):
        for _ in range(TOP_K):
            pltpu.make_async_copy(stage.at[sl], stage.at[sl], sems.at[sl]).wait()

    @pl.when(i >= 2)
    def _():
        drain(slot)

    packed = _pack_bf16_pairs(h_ref[...])
    for l in range(X_SUBL):
        stage[slot, pl.ds(l, tm, stride=X_SUBL), :] = packed[:, LANES * l:LANES * (l + 1)]

    def issue(t, carry):
        for jx in range(TOP_K):
            _tile_copy(stage.at[slot], t, xs_hbm, dest_ref[0, 0, jx * tm + t], X_SUBL,
                       sems.at[slot]).start(priority=jx % 2)
        return carry

    lax.fori_loop(0, tm, issue, 0, unroll=DMA_LOOP_UNROLL)

    @pl.when(i == 0)
    def _():
        zero[...] = jnp.zeros_like(zero)

        def fill(k, carry):
            _tile_copy(zero, 0, xs_hbm, pad_ref[k], X_SUBL, pad_sem).start()
            return carry

        lax.fori_loop(0, npad_ref[0], fill, 0)

        def fill_wait(k, carry):
            _tile_copy(zero, 0, xs_hbm, 0, X_SUBL, pad_sem).wait()
            return carry

        lax.fori_loop(0, npad_ref[0], fill_wait, 0)

    @pl.when(i == n_steps - 1)
    def _():
        drain(slot)
        if n_steps > 1:
            drain(1 - slot)


def _scatter_rows(h_all, dest, pad_dest, n_pad, n_rows, tm):
    n = h_all.shape[0]
    assert n % tm == 0
    n_steps = n // tm
    dest_blk = dest.reshape(n_steps, tm, TOP_K).transpose(0, 2, 1).reshape(n_steps, 1, TOP_K * tm)
    kern = functools.partial(_scatter_kernel, tm=tm, n_steps=n_steps)
    return pl.pallas_call(
        kern,
        grid_spec=pltpu.PrefetchScalarGridSpec(
            num_scalar_prefetch=1,
            grid=(n_steps,),
            in_specs=[
                pl.BlockSpec((1, 1, TOP_K * tm), lambda i, npad: (i, 0, 0), memory_space=pltpu.SMEM),
                pl.BlockSpec(memory_space=pltpu.SMEM),
                pl.BlockSpec((tm, D_MODEL), lambda i, npad: (i, 0)),
            ],
            out_specs=pl.BlockSpec(memory_space=pl.ANY),
            scratch_shapes=[
                pltpu.VMEM((2, tm * X_SUBL, LANES), jnp.uint32),
                pltpu.VMEM((X_SUBL, LANES), jnp.uint32),
                pltpu.SemaphoreType.DMA((2,)),
                pltpu.SemaphoreType.DMA(()),
            ],
        ),
        out_shape=jax.ShapeDtypeStruct((n_rows * X_SUBL, LANES), jnp.uint32),
        compiler_params=_cparams(("arbitrary",)),
    )(n_pad, dest_blk, pad_dest, h_all)


def _expert_kernel(se_ref, nsub_ref, nused_ref, x_ref, wg_ref, wl_ref, bg_ref, bl_ref, wd_ref,
                   bd_ref, o_ref, xb_scr, acc_scr, *, nf):
    s = pl.program_id(0)
    j = pl.program_id(1)
    n_sub = nsub_ref[s]

    def step(m):
        rows = slice(0, m)

        @pl.when(j == 0)
        def _():
            for l in range(X_SUBL):
                hi, lo = _unpack_bf16_pairs(x_ref[pl.ds(l, m, stride=X_SUBL), :])
                xb_scr[rows, LANES * l:LANES * (l + 1)] = hi.astype(bf16)
                xb_scr[rows, HALF_D + LANES * l:HALF_D + LANES * (l + 1)] = lo.astype(bf16)
            acc_scr[rows, :] = jnp.broadcast_to(bd_ref[...], (m, D_MODEL))

        xb = xb_scr[rows, :]
        hg = jnp.dot(xb, wg_ref[...].astype(bf16), preferred_element_type=f32) + bg_ref[...]
        hl = jnp.dot(xb, wl_ref[...].astype(bf16), preferred_element_type=f32) + bl_ref[...]
        glu = jnp.minimum(hg, SWIGLU_LIMIT)
        lin = jnp.clip(hl, -SWIGLU_LIMIT, SWIGLU_LIMIT)
        act = (glu * (1.0 / (1.0 + jnp.exp(-SWIGLU_ALPHA * glu))) * (lin + 1.0)).astype(bf16)
        wd = wd_ref[...].astype(bf16)
        for n0 in range(0, D_MODEL, MOE_DOWN_N):
            acc_scr[rows, n0:n0 + MOE_DOWN_N] += jnp.dot(
                act, wd[:, n0:n0 + MOE_DOWN_N], preferred_element_type=f32)

        @pl.when(j == nf - 1)
        def _():
            packed = _pack_bf16_pairs(acc_scr[rows, :])
            for l in range(X_SUBL):
                o_ref[pl.ds(l, m, stride=X_SUBL), :] = packed[:, LANES * l:LANES * (l + 1)]

    for k in range(1, MOE_NSUB + 1):
        pl.when(n_sub == k)(functools.partial(step, k * MOE_SUB))


def _experts(xs, super_e, n_sub, n_used, w_up, b_up, w_down, b_down, n_super):
    tf = MOE_FF_TILE
    nf = D_FF // tf
    last = lambda s, nu: jnp.minimum(s, nu[0] - 1)
    b_up3 = b_up.reshape(N_EXPERTS, 1, 2 * D_FF)
    b_down3 = b_down.reshape(N_EXPERTS, 1, D_MODEL)
    e_of = lambda s, se, nu: se[last(s, nu)]
    return pl.pallas_call(
        functools.partial(_expert_kernel, nf=nf),
        grid_spec=pltpu.PrefetchScalarGridSpec(
            num_scalar_prefetch=3,
            grid=(n_super, nf),
            in_specs=[
                pl.BlockSpec((MOE_SUPER * X_SUBL, LANES), lambda s, j, se, ns, nu: (last(s, nu), 0)),
                pl.BlockSpec((None, D_MODEL, tf), lambda s, j, se, ns, nu: (e_of(s, se, nu), 0, j)),
                pl.BlockSpec((None, D_MODEL, tf),
                             lambda s, j, se, ns, nu: (e_of(s, se, nu), 0, nf + j)),
                pl.BlockSpec((None, 1, tf), lambda s, j, se, ns, nu: (e_of(s, se, nu), 0, j)),
                pl.BlockSpec((None, 1, tf), lambda s, j, se, ns, nu: (e_of(s, se, nu), 0, nf + j)),
                pl.BlockSpec((None, tf, D_MODEL), lambda s, j, se, ns, nu: (e_of(s, se, nu), j, 0)),
                pl.BlockSpec((None, 1, D_MODEL), lambda s, j, se, ns, nu: (e_of(s, se, nu), 0, 0)),
            ],
            out_specs=pl.BlockSpec((MOE_SUPER * X_SUBL, LANES),
                                   lambda s, j, se, ns, nu: (last(s, nu), 0)),
            scratch_shapes=[pltpu.VMEM((MOE_SUPER, D_MODEL), bf16),
                            pltpu.VMEM((MOE_SUPER, D_MODEL), f32)],
        ),
        out_shape=jax.ShapeDtypeStruct((n_super * MOE_SUPER * X_SUBL, LANES), jnp.uint32),
        compiler_params=_cparams(("arbitrary", "arbitrary")),
    )(super_e, n_sub, n_used, xs, w_up, w_up, b_up3, b_up3, w_down, b_down3)


def _combine_kernel(dest_ref, nxt_ref, gate_ref, h_ref, rows_hbm, g_ref, b_ref, yp_ref, ys_ref,
                    buf, sems, *, tm, n_first, n_steps):
    i = pl.program_id(0)
    slot = i % 2

    def fetch(idx_ref, sl):
        def body(t, carry):
            for jx in range(TOP_K):
                _tile_copy(rows_hbm, idx_ref[0, 0, jx * tm + t], buf.at[sl, jx], t, X_SUBL,
                           sems.at[sl]).start(priority=jx % 2)
            return carry
        lax.fori_loop(0, tm, body, 0, unroll=DMA_LOOP_UNROLL)

    @pl.when(i == 0)
    def _():
        fetch(dest_ref, slot)

    @pl.when(i + 1 < n_steps)
    def _():
        fetch(nxt_ref, 1 - slot)

    for jx in range(TOP_K):
        pltpu.make_async_copy(buf.at[slot, jx], buf.at[slot, jx], sems.at[slot]).wait()
    gate = gate_ref[...]
    cols_hi, cols_lo = [], []
    for l in range(X_SUBL):
        acc_hi = acc_lo = None
        for jx in range(TOP_K):
            hi, lo = _unpack_bf16_pairs(buf[slot, jx, pl.ds(l, tm, stride=X_SUBL), :])
            gj = gate[:, jx:jx + 1]
            acc_hi = gj * hi if acc_hi is None else acc_hi + gj * hi
            acc_lo = gj * lo if acc_lo is None else acc_lo + gj * lo
        cols_hi.append(acc_hi)
        cols_lo.append(acc_lo)
    y = jnp.concatenate(cols_hi + cols_lo, axis=-1)
    out = _layer_norm(DN_ALPHA * h_ref[...] + y, g_ref[...], b_ref[...])

    @pl.when(i < n_first)
    def _():
        yp_ref[...] = out

    @pl.when(i >= n_first)
    def _():
        ys_ref[...] = out


def _combine(rows_out, dest, gate, h_all, ln_g, ln_b, *, tm, n_first_rows):
    n = h_all.shape[0]
    assert n % tm == 0 and n_first_rows % tm == 0
    nblk = n // tm
    n_first = n_first_rows // tm
    dest_blk = dest.reshape(nblk, tm, TOP_K).transpose(0, 2, 1).reshape(nblk, 1, TOP_K * tm)
    kern = functools.partial(_combine_kernel, tm=tm, n_first=n_first, n_steps=nblk)
    const = lambda i: (0, 0)
    return pl.pallas_call(
        kern,
        grid=(nblk,),
        in_specs=[
            pl.BlockSpec((1, 1, TOP_K * tm), lambda i: (i, 0, 0), memory_space=pltpu.SMEM),
            pl.BlockSpec((1, 1, TOP_K * tm), lambda i: (jnp.minimum(i + 1, nblk - 1), 0, 0),
                         memory_space=pltpu.SMEM),
            pl.BlockSpec((tm, TOP_K), lambda i: (i, 0)),
            pl.BlockSpec((tm, D_MODEL), lambda i: (i, 0)),
            pl.BlockSpec(memory_space=pl.ANY),
            pl.BlockSpec((1, D_MODEL), const),
            pl.BlockSpec((1, D_MODEL), const),
        ],
        out_specs=[
            pl.BlockSpec((tm, D_MODEL), lambda i: (jnp.minimum(i, n_first - 1), 0)),
            pl.BlockSpec((tm, D_MODEL), lambda i: (jnp.maximum(i - n_first, 0), 0)),
        ],
        out_shape=[
            jax.ShapeDtypeStruct((n_first_rows, D_MODEL), f32),
            jax.ShapeDtypeStruct((n - n_first_rows, D_MODEL), f32),
        ],
        scratch_shapes=[pltpu.VMEM((2, TOP_K, tm * X_SUBL, LANES), jnp.uint32),
                        pltpu.SemaphoreType.DMA((2,))],
        compiler_params=_cparams(("arbitrary",)),
    )(dest_blk, dest_blk, gate, h_all, rows_out, ln_g, ln_b)


def _t5_bucket(rel):
    half = NUM_BUCKETS // 2
    exact = half // 2
    n = jnp.abs(rel)
    log_part = exact + (jnp.log(jnp.maximum(n, 1).astype(jnp.float32) / exact)
                        / math.log(MAX_DISTANCE / exact) * (half - exact)).astype(jnp.int32)
    log_part = jnp.minimum(log_part, half - 1)
    return jnp.where(rel > 0, half, 0) + jnp.where(n < exact, n, log_part)


def _band_bias(rel_bias):
    qi = jnp.arange(CHUNK)[:, None]
    km = jnp.arange(BAND)[None, :]
    bucket = _t5_bucket(km - WINDOW - qi)
    onehot = (bucket[..., None] == jnp.arange(NUM_BUCKETS)).astype(f32)
    return jnp.einsum("imb,bh->him", onehot, rel_bias.astype(f32),
                      precision=lax.Precision.HIGHEST)


def _pad_cols(a, width):
    return jnp.pad(a, ((0, 0), (0, width - a.shape[-1])))


def _pair_state(s):
    b = s.shape[0]
    return s.reshape(b, N_PAIRS, 2, RW_HEAD, RW_HEAD).transpose(0, 1, 3, 2, 4).reshape(
        b, N_PAIRS, RW_HEAD, LANES)


def _unpair_state(s):
    b = s.shape[0]
    return s.reshape(b, N_PAIRS, RW_HEAD, 2, RW_HEAD).transpose(0, 1, 3, 2, 4).reshape(
        b, RW_HEADS, RW_HEAD, RW_HEAD)


def _mix_group(x, k_hist, v_hist, hist_valid, wkv0, shift0, bias, sinks3, w_in_pad, rw_wts,
               *, in_tm, attn_nc):
    b, t, _ = x.shape
    q, kv, prw = _inproj(x.reshape(b * t, D_MODEL), w_in_pad, in_tm)
    tp = -(-t // (attn_nc * CHUNK)) * (attn_nc * CHUNK)
    q4 = q if b == 1 else q.reshape(ATT_HEADS, b, t, HEAD_DIM)
    kv3 = kv.reshape(b, t, 2 * KV_W)
    prw3 = prw.reshape(b, t, RW_PAD)
    hist = jnp.concatenate([k_hist.reshape(b, WINDOW, KV_W), v_hist.reshape(b, WINDOW, KV_W)], axis=-1)
    if t >= WINDOW:
        new_kv = kv3[:, t - WINDOW:]
    else:
        new_kv = jnp.concatenate([hist[:, t:], kv3], axis=1)
    if tp != t:
        q4 = jnp.pad(q4, ((0, 0), (0, 0), (0, tp - t), (0, 0)))
        prw3 = jnp.pad(prw3, ((0, 0), (0, tp - t), (0, 0)))
        kv3 = jnp.pad(kv3, ((0, 0), (0, tp - t), (0, 0)))
    att = _attention(q4, hist, kv3, bias, sinks3, nc=attn_nc, t_valid=t, hist_valid=hist_valid)
    rw, s_fin = _rwkv(prw3, _pad_cols(shift0.reshape(b, RW_PROJ), RW_PAD).reshape(b, 1, RW_PAD),
                      _pair_state(wkv0.astype(f32)), rw_wts, t_valid=t)
    new_k = new_kv[..., :KV_W].reshape(b, WINDOW, ATT_KV_HEADS, HEAD_DIM)
    new_v = new_kv[..., KV_W:].reshape(b, WINDOW, ATT_KV_HEADS, HEAD_DIM)
    shift = prw3[:, t - 1:t, :RW_PROJ]
    return (att[:, :t].reshape(b * t, ATT_W), rw[:, :t].reshape(b * t, RW_W),
            new_k, new_v, _unpair_state(s_fin), shift)


SCATTER_TM = 128
COMBINE_TM = 128


def kernel(x_prompt, x_sample, cache_k, cache_v, state_wkv, state_shift, rel_bias, w_in, attn_sinks, rw_mu, rw_w0, rw_decay_up, rw_a0, rw_iclr_up, rw_gate_up, rw_k_k, rw_k_a, rw_r_k, rw_lnx_g, rw_lnx_b, w_out, ln1_g, ln1_b, w_router, b_router, w_up, b_up, w_down, b_down, ln2_g, ln2_b):
    assert w_in.shape[0] == DEPTH == 1
    l = 0
    bp, tp_, _ = x_prompt.shape
    bs, ts, _ = x_sample.shape
    bias = _band_bias(rel_bias)
    sinks3 = attn_sinks[l].astype(f32).reshape(ATT_HEADS, 1, 1)

    w_in_pad = _pad_cols(w_in[l], IN_PAD).astype(bf16)

    def lora_rows(w, row0):
        return jnp.zeros((LORA_PAD, RW_W), f32).at[row0:row0 + w.shape[0]].set(w).astype(bf16)

    rw_wts = {
        "mu": _pad_cols(rw_mu[l].reshape(1, RW_PROJ), RW_PAD),
        "w0": rw_w0[l].reshape(1, RW_W),
        "wd": lora_rows(rw_decay_up[l], 0),
        "a0": rw_a0[l].reshape(1, RW_W),
        "wa": lora_rows(rw_iclr_up[l], DECAY_LORA),
        "wg": lora_rows(rw_gate_up[l], DECAY_LORA + ICLR_LORA),
        "k_k": rw_k_k[l].reshape(1, RW_W),
        "k_a": rw_k_a[l].reshape(1, RW_W),
        "r_k": rw_r_k[l].reshape(1, RW_W),
        "lnx_g": rw_lnx_g[l].reshape(1, RW_W),
        "lnx_b": rw_lnx_b[l].reshape(1, RW_W),
    }
    wr = _pad_cols(w_router[l], LANES)
    wr_hi = wr.astype(bf16)
    op_wts = {
        "w_out": w_out[l].astype(bf16),
        "ln1_g": ln1_g[l].reshape(1, D_MODEL),
        "ln1_b": ln1_b[l].reshape(1, D_MODEL),
        "wr_hi": wr_hi,
        "wr_lo": (wr - wr_hi.astype(f32)).astype(bf16),
        "b_router": jnp.concatenate([b_router[l].astype(f32),
                                     jnp.full((LANES - N_EXPERTS,), NEG_BIG, f32)]).reshape(1, LANES),
    }

    zero_kv = jnp.zeros((bp, WINDOW, ATT_KV_HEADS, HEAD_DIM), f32)
    att_p, rwo_p, k1, v1, w1, s1 = _mix_group(
        x_prompt, zero_kv, zero_kv, False, jnp.zeros((bp, RW_HEADS, RW_HEAD, RW_HEAD), f32),
        jnp.zeros((bp, 1, RW_PROJ), f32), bias, sinks3, w_in_pad, rw_wts,
        in_tm=min(512, bp * tp_), attn_nc=min(8, -(-tp_ // CHUNK)))
    att_s, rwo_s, k2, v2, w2, s2 = _mix_group(
        x_sample, cache_k[l], cache_v[l], True, state_wkv[l], state_shift[l], bias, sinks3,
        w_in_pad, rw_wts, in_tm=min(256, bs * ts), attn_nc=1)

    n_p, n_s = bp * tp_, bs * ts
    n_all = n_p + n_s
    tm_p, tm_s = min(512, n_p), min(128, n_s)
    *outs, counts_p = _outproj(att_p, rwo_p, x_prompt.reshape(n_p, D_MODEL), op_wts,
                               tm=tm_p, row0=0, total_rows=n_all)
    h_all, top_idx, gate, rank, counts = _outproj(
        att_s, rwo_s, x_sample.reshape(n_s, D_MODEL), op_wts,
        tm=tm_s, row0=n_p, total_rows=n_all, prev=outs, counts0=counts_p)

    dest, super_e, n_sub, n_used, pad_dest, n_pad, n_super = _routing(top_idx, rank, counts)
    xs = _scatter_rows(h_all, dest, pad_dest, n_pad, n_super * MOE_SUPER, min(SCATTER_TM, n_s))
    rows_out = _experts(xs, super_e, n_sub, n_used, w_up[l], b_up[l], w_down[l], b_down[l],
                        n_super)
    y_p, y_s = _combine(rows_out, dest, gate, h_all, ln2_g[l].reshape(1, D_MODEL),
                        ln2_b[l].reshape(1, D_MODEL), tm=min(COMBINE_TM, n_s), n_first_rows=n_p)

    return (y_p.reshape(bp, tp_, D_MODEL), y_s.reshape(bs, ts, D_MODEL),
            k1[None], v1[None], w1[None], s1[None], k2[None], v2[None], w2[None], s2[None])
```
